```python
import math
import jax
import jax.numpy as jnp
from jax import lax
import numpy as np

D_MODEL = 1024
BATCH = 4
SEQ = 4096
DEPTH = 4

HEAD_DIM = 64
RMS_EPS = 1e-6
ROPE_THETA = 10000.0
Q_BLOCK = 128
N_BRANCHES = 3
BRANCH_WIDTH = 512
SSM_WIDTH = BRANCH_WIDTH
SSM_GROUP = 16
SSM_GROUPS = SSM_WIDTH // SSM_GROUP
SSM_STATE = 64
DT_MIN = 0.001
DT_MAX = 0.1
NSA_HEADS = 8
NSA_KV_HEADS = 2
NSA_GROUP = NSA_HEADS // NSA_KV_HEADS
NSA_WIDTH = NSA_HEADS * HEAD_DIM
NSA_KV_COLS = 3 * 2 * NSA_KV_HEADS * HEAD_DIM
NSA_GATE_COLS = 3 * NSA_HEADS
CMP_LEN = 32
CMP_STRIDE = 16
CMP_HIDDEN = 128
SEL_LEN = 64
SEL_TOPK = 16
WINDOW = 512
FORCE_BONUS = 1e6
SB_HEADS = 8
SB_WIDTH = SB_HEADS * HEAD_DIM
D_FF = 2816
CONV_WIDTH = 3
IN_WIDTH = SSM_WIDTH + NSA_WIDTH + NSA_KV_COLS + NSA_GATE_COLS + 3 * SB_WIDTH + N_BRANCHES * D_MODEL

kernel_name = "hybrid_s5_nsa_stickbreak_gated_merge"


def rms_norm(x, g):
    xf = x.astype(jnp.float32)
    y = xf * lax.rsqrt(jnp.mean(xf * xf, axis=-1, keepdims=True) + RMS_EPS)
    return (y * g.astype(jnp.float32)).astype(x.dtype)


def rope_tables(seq_len):
    inv_freq = 1.0 / (ROPE_THETA ** (jnp.arange(0, HEAD_DIM, 2, dtype=jnp.float32) / HEAD_DIM))
    ang = jnp.arange(seq_len, dtype=jnp.float32)[:, None] * inv_freq[None, :]
    return jnp.cos(ang), jnp.sin(ang)


def apply_rope(t, cos, sin):
    t1, t2 = jnp.split(t.astype(jnp.float32), 2, axis=-1)
    c = cos[None, :, None, :]
    s = sin[None, :, None, :]
    return jnp.concatenate([t1 * c - t2 * s, t1 * s + t2 * c], axis=-1).astype(t.dtype)


def split_columns(proj):
    sizes = (SSM_WIDTH, NSA_WIDTH, NSA_KV_COLS, NSA_GATE_COLS, 3 * SB_WIDTH, N_BRANCHES * D_MODEL)
    points = [int(p) for p in np.cumsum(sizes)[:-1]]
    return jnp.split(proj, points, axis=-1)


def masked_softmax(s, mask):
    s = jnp.where(mask, s, -jnp.inf)
    m = jnp.max(s, axis=-1, keepdims=True)
    m = jnp.where(jnp.isfinite(m), m, 0.0)
    e = jnp.exp(s - m)
    return e / jnp.maximum(jnp.sum(e, axis=-1, keepdims=True), 1e-30)


def s5_mixer(u, lam_re, lam_im, log_step, b_re, b_im, c_re, c_im, d_skip, w_glu):
    bsz, seq, _ = u.shape
    f32 = jnp.float32
    ug = u.reshape(bsz, seq, SSM_GROUPS, SSM_GROUP).astype(f32)
    step = jnp.exp(log_step.astype(f32))[:, None]
    lr = lam_re.astype(f32)
    li = lam_im.astype(f32)
    mag = jnp.exp(lr * step)
    a_re = mag * jnp.cos(li * step)
    a_im = mag * jnp.sin(li * step)
    den = lr * lr + li * li
    nr = a_re - 1.0
    f_re = (nr * lr + a_im * li) / den
    f_im = (a_im * lr - nr * li) / den
    br = b_re.astype(f32)
    bi = b_im.astype(f32)
    bb_re = f_re[..., None] * br - f_im[..., None] * bi
    bb_im = f_re[..., None] * bi + f_im[..., None] * br
    x_re = jnp.einsum('bsgi,gpi->bsgp', ug, bb_re)
    x_im = jnp.einsum('bsgi,gpi->bsgp', ug, bb_im)
    a_re_b = jnp.broadcast_to(a_re, x_re.shape)
    a_im_b = jnp.broadcast_to(a_im, x_re.shape)

    def combine(e1, e2):
        a1r, a1i, b1r, b1i = e1
        a2r, a2i, b2r, b2i = e2
        return (a2r * a1r - a2i * a1i,
                a2r * a1i + a2i * a1r,
                a2r * b1r - a2i * b1i + b2r,
                a2r * b1i + a2i * b1r + b2i)

    _, _, h_re, h_im = lax.associative_scan(combine, (a_re_b, a_im_b, x_re, x_im), axis=1)
    y = (jnp.einsum('gip,bsgp->bsgi', c_re.astype(f32), h_re)
         - jnp.einsum('gip,bsgp->bsgi', c_im.astype(f32), h_im)
         + d_skip.astype(f32) * ug)
    y = jax.nn.gelu(y.reshape(bsz, seq, SSM_WIDTH))
    y = y * jax.nn.sigmoid(y @ w_glu.astype(f32))
    return y.astype(u.dtype)


def compress_blocks(t, w1, w2, pos):
    seq = t.shape[1]
    n_cmp = (seq - CMP_LEN) // CMP_STRIDE + 1
    idx = jnp.arange(n_cmp)[:, None] * CMP_STRIDE + jnp.arange(CMP_LEN)[None, :]
    blocks = t[:, idx] + pos[None, None, :, None, :]
    hid = jax.nn.gelu(jnp.einsum('bnlhd,lde->bnhe', blocks, w1))
    return jnp.einsum('bnhe,ed->bnhd', hid, w2)


def selected_attention(qg, k, v, sel_idx):
    bsz, seq, n_kv, n_grp, hd = qg.shape
    n_sel = seq // SEL_LEN
    n_top = sel_idx.shape[-1]
    scale = HEAD_DIM ** -0.5
    kb = k.reshape(bsz, n_sel, SEL_LEN, n_kv, hd).transpose(0, 3, 1, 2, 4)
    vb = v.reshape(bsz, n_sel, SEL_LEN, n_kv, hd).transpose(0, 3, 1, 2, 4)
    gather = jax.vmap(jax.vmap(lambda blocks, ix: blocks[ix]))
    offs = jnp.arange(SEL_LEN)
    n_keys = n_top * SEL_LEN

    def block(i):
        t0 = i * Q_BLOCK
        qi = lax.dynamic_slice_in_dim(qg, t0, Q_BLOCK, axis=1)
        ix = lax.dynamic_slice_in_dim(sel_idx, t0, Q_BLOCK, axis=2)
        kg = gather(kb, ix).reshape(bsz, n_kv, Q_BLOCK, n_keys, hd)
        vg = gather(vb, ix).reshape(bsz, n_kv, Q_BLOCK, n_keys, hd)
        kpos = (ix[..., None] * SEL_LEN + offs).reshape(bsz, n_kv, 1, Q_BLOCK, n_keys)
        tq = (t0 + jnp.arange(Q_BLOCK))[:, None]
        s = jnp.einsum('bqhgd,bhqmd->bhgqm', qi, kg).astype(jnp.float32) * scale
        p = masked_softmax(s, kpos <= tq)
        return jnp.einsum('bhgqm,bhqmd->bqhgd', p, vg.astype(jnp.float32))

    out = lax.map(block, jnp.arange(seq // Q_BLOCK))
    return jnp.moveaxis(out, 0, 1).reshape(bsz, seq, n_kv, n_grp, hd)


def window_attention(qg, k, v):
    bsz, seq, n_kv, n_grp, hd = qg.shape
    scale = HEAD_DIM ** -0.5
    span = WINDOW + Q_BLOCK
    kp = jnp.pad(k, ((0, 0), (WINDOW, 0), (0, 0), (0, 0)))
    vp = jnp.pad(v, ((0, 0), (WINDOW, 0), (0, 0), (0, 0)))

    def block(i):
        t0 = i * Q_BLOCK
        qi = lax.dynamic_slice_in_dim(qg, t0, Q_BLOCK, axis=1)
        ki = lax.dynamic_slice_in_dim(kp, t0, span, axis=1)
        vi = lax.dynamic_slice_in_dim(vp, t0, span, axis=1)
        kpos = t0 - WINDOW + jnp.arange(span)
        tq = t0 + jnp.arange(Q_BLOCK)
        diff = tq[:, None] - kpos[None, :]
        mask = (diff >= 0) & (diff < WINDOW) & (kpos[None, :] >= 0)
        s = jnp.einsum('bqhgd,bkhd->bhgqk', qi, ki).astype(jnp.float32) * scale
        p = masked_softmax(s, mask)
        return jnp.einsum('bhgqk,bkhd->bqhgd', p, vi.astype(jnp.float32))

    out = lax.map(block, jnp.arange(seq // Q_BLOCK))
    return jnp.moveaxis(out, 0, 1).reshape(bsz, seq, n_kv, n_grp, hd)


def nsa_mixer(q, k_all, v_all, gates, w1_k, w2_k, pos_k, w1_v, w2_v, pos_v):
    bsz, seq = q.shape[:2]
    f32 = jnp.float32
    scale = HEAD_DIM ** -0.5
    qg = q.reshape(bsz, seq, NSA_KV_HEADS, NSA_GROUP, HEAD_DIM)
    t_pos = jnp.arange(seq)
    kc = compress_blocks(k_all[:, :, 0], w1_k, w2_k, pos_k)
    vc = compress_blocks(v_all[:, :, 0], w1_v, w2_v, pos_v)
    n_cmp = kc.shape[1]
    cmp_start = jnp.arange(n_cmp) * CMP_STRIDE
    s_cmp = jnp.einsum('bthgd,bnhd->bhgtn', qg, kc).astype(f32) * scale
    p_cmp = masked_softmax(s_cmp, (cmp_start + CMP_LEN - 1)[None, :] <= t_pos[:, None])
    o_cmp = jnp.einsum('bhgtn,bnhd->bthgd', p_cmp, vc.astype(f32))
    n_sel = seq // SEL_LEN
    sel_start = jnp.arange(n_sel) * SEL_LEN
    overlap = ((cmp_start[:, None] < sel_start[None, :] + SEL_LEN)
               & (cmp_start[:, None] + CMP_LEN > sel_start[None, :])).astype(f32)
    p_sel = jnp.einsum('bhgtn,nj->bhtj', p_cmp, overlap)
    blk = jnp.arange(n_sel)[None, :]
    t_blk = (t_pos // SEL_LEN)[:, None]
    valid = blk <= t_blk
    forced = (blk == 0) | (blk == t_blk) | (blk == t_blk - 1)
    rank = jnp.where(valid, p_sel + FORCE_BONUS * forced.astype(f32), -FORCE_BONUS)
    _, sel_idx = lax.top_k(rank, min(SEL_TOPK, n_sel))
    o_slc = selected_attention(qg, k_all[:, :, 1], v_all[:, :, 1], sel_idx)
    o_win = window_attention(qg, k_all[:, :, 2], v_all[:, :, 2])
    g = gates.astype(f32)
    o = g[..., 0:1] * o_cmp + g[..., 1:2] * o_slc + g[..., 2:3] * o_win
    return o.reshape(bsz, seq, NSA_WIDTH).astype(q.dtype)


def stick_breaking_attention(q, k, v):
    bsz, seq, n_h, hd = q.shape
    scale = HEAD_DIM ** -0.5
    kpos = jnp.arange(seq)
    vf = v.astype(jnp.float32)

    def block(i):
        t0 = i * Q_BLOCK
        qi = lax.dynamic_slice_in_dim(q, t0, Q_BLOCK, axis=1)
        z = jnp.einsum('bqhd,bkhd->bhqk', qi, k).astype(jnp.float32) * scale
        tq = t0 + jnp.arange(Q_BLOCK)
        causal = kpos[None, :] < tq[:, None]
        log_rest = jnp.where(causal, jax.nn.log_sigmoid(-z), 0.0)
        after = lax.cumsum(log_rest, axis=3, reverse=True) - log_rest
        w = jnp.where(causal, jnp.exp(jax.nn.log_sigmoid(z) + after), 0.0)
        return jnp.einsum('bhqk,bkhd->bqhd', w, vf)

    out = lax.map(block, jnp.arange(seq // Q_BLOCK))
    return jnp.moveaxis(out, 0, 1).reshape(bsz, seq, n_h, hd).astype(q.dtype)


def conv_ffn(h, w_up, conv_w, conv_b, w_down):
    u = h @ w_up
    c = lax.conv_general_dilated(
        u, conv_w[:, None, :], window_strides=(1,), padding=((CONV_WIDTH - 1, 0),),
        dimension_numbers=('NWC', 'WIO', 'NWC'), feature_group_count=u.shape[-1]) + conv_b
    gate, val = jnp.split(c, 2, axis=-1)
    return (jax.nn.silu(gate) * val) @ w_down


def setup_inputs(seed: int = 0) -> dict:
    key = jax.random.key(seed)
    ks = iter(jax.random.split(key, 32))
    f32 = jnp.float32
    L = DEPTH
    G, P, I = SSM_GROUPS, SSM_STATE, SSM_GROUP

    def nrm(shape, scale):
        return jax.random.normal(next(ks), shape, f32) * scale

    return {
        'x': nrm((BATCH, SEQ, D_MODEL), 1.0),
        'norm_mix': 1.0 + nrm((L, D_MODEL), 0.02),
        'w_in': nrm((L, D_MODEL, IN_WIDTH), D_MODEL ** -0.5),
        'ssm_lam_re': -0.5 + nrm((L, G, P), 0.01),
        'ssm_lam_im': jnp.pi * jnp.arange(P, dtype=f32) + nrm((L, G, P), 0.01),
        'ssm_log_step': jax.random.uniform(next(ks), (L, G), f32, math.log(DT_MIN), math.log(DT_MAX)),
        'ssm_b_re': nrm((L, G, P, I), (2 * I) ** -0.5),
        'ssm_b_im': nrm((L, G, P, I), (2 * I) ** -0.5),
        'ssm_c_re': nrm((L, G, I, P), (2 * P) ** -0.5),
        'ssm_c_im': nrm((L, G, I, P), (2 * P) ** -0.5),
        'ssm_d': nrm((L, G, I), 1.0),
        'ssm_w_glu': nrm((L, SSM_WIDTH, SSM_WIDTH), SSM_WIDTH ** -0.5),
        'cmp_w1_k': nrm((L, CMP_LEN, HEAD_DIM, CMP_HIDDEN), (CMP_LEN * HEAD_DIM) ** -0.5),
        'cmp_w2_k': nrm((L, CMP_HIDDEN, HEAD_DIM), CMP_HIDDEN ** -0.5),
        'cmp_pos_k': nrm((L, CMP_LEN, HEAD_DIM), 0.1),
        'cmp_w1_v': nrm((L, CMP_LEN, HEAD_DIM, CMP_HIDDEN), (CMP_LEN * HEAD_DIM) ** -0.5),
        'cmp_w2_v': nrm((L, CMP_HIDDEN, HEAD_DIM), CMP_HIDDEN ** -0.5),
        'cmp_pos_v': nrm((L, CMP_LEN, HEAD_DIM), 0.1),
        'w_branch': nrm((L, N_BRANCHES, BRANCH_WIDTH, D_MODEL), BRANCH_WIDTH ** -0.5),
        'w_out': nrm((L, D_MODEL, D_MODEL), D_MODEL ** -0.5),
        'norm_ffn': 1.0 + nrm((L, D_MODEL), 0.02),
        'ffn_w_up': nrm((L, D_MODEL, 2 * D_FF), D_MODEL ** -0.5),
        'ffn_conv_w': nrm((L, CONV_WIDTH, 2 * D_FF), CONV_WIDTH ** -0.5),
        'ffn_conv_b': nrm((L, 2 * D_FF), 0.02),
        'ffn_w_down': nrm((L, D_FF, D_MODEL), D_FF ** -0.5),
        'norm_final': 1.0 + nrm((D_MODEL,), 0.02),
    }


def reference(x, norm_mix, w_in, ssm_lam_re, ssm_lam_im, ssm_log_step, ssm_b_re, ssm_b_im,
              ssm_c_re, ssm_c_im, ssm_d, ssm_w_glu, cmp_w1_k, cmp_w2_k, cmp_pos_k,
              cmp_w1_v, cmp_w2_v, cmp_pos_v, w_branch, w_out, norm_ffn, ffn_w_up,
              ffn_conv_w, ffn_conv_b, ffn_w_down, norm_final):
    bsz, seq, _ = x.shape
    cos, sin = rope_tables(seq)
    for l in range(DEPTH):
        h = rms_norm(x, norm_mix[l])
        proj = h @ w_in[l]
        u_ssm, q_nsa, kv_nsa, g_nsa, qkv_sb, g_merge = split_columns(proj)
        o_ssm = s5_mixer(u_ssm, ssm_lam_re[l], ssm_lam_im[l], ssm_log_step[l], ssm_b_re[l],
                         ssm_b_im[l], ssm_c_re[l], ssm_c_im[l], ssm_d[l], ssm_w_glu[l])
        q_b = apply_rope(q_nsa.reshape(bsz, seq, NSA_HEADS, HEAD_DIM), cos, sin)
        kv = kv_nsa.reshape(bsz, seq, 3, 2, NSA_KV_HEADS, HEAD_DIM)
        k_b = apply_rope(kv[:, :, :, 0].reshape(bsz, seq, 3 * NSA_KV_HEADS, HEAD_DIM), cos, sin)
        k_b = k_b.reshape(bsz, seq, 3, NSA_KV_HEADS, HEAD_DIM)
        v_b = kv[:, :, :, 1]
        gates_b = jax.nn.sigmoid(g_nsa).reshape(bsz, seq, NSA_KV_HEADS, NSA_GROUP, 3)
        o_nsa = nsa_mixer(q_b, k_b, v_b, gates_b, cmp_w1_k[l], cmp_w2_k[l], cmp_pos_k[l],
                          cmp_w1_v[l], cmp_w2_v[l], cmp_pos_v[l])
        qkv = qkv_sb.reshape(bsz, seq, 3, SB_HEADS, HEAD_DIM)
        o_sb = stick_breaking_attention(qkv[:, :, 0], qkv[:, :, 1], qkv[:, :, 2]).reshape(bsz, seq, SB_WIDTH)
        branches = jnp.stack([o_ssm, o_nsa, o_sb], axis=2).astype(x.dtype)
        y = jnp.einsum('bsnc,ncd->bsnd', branches, w_branch[l])
        gate = jax.nn.sigmoid(g_merge.reshape(bsz, seq, N_BRANCHES, D_MODEL))
        x = x + (jnp.sum(gate * y, axis=2) @ w_out[l]).astype(x.dtype)
        h = rms_norm(x, norm_ffn[l])
        x = x + conv_ffn(h, ffn_w_up[l], ffn_conv_w[l], ffn_conv_b[l], ffn_w_down[l]).astype(x.dtype)
    return rms_norm(x, norm_final)
```

```python
import functools
import math

import jax
import jax.numpy as jnp
from jax import lax
from jax.experimental import pallas as pl
from jax.experimental.pallas import tpu as pltpu

F32 = jnp.float32
BF16 = jnp.bfloat16

HEAD_DIM = 64
RMS_EPS = 1e-6
ROPE_THETA = 10000.0
SSM_GROUP = 16
SSM_STATE = 64
NSA_HEADS = 8
NSA_KV_HEADS = 2
NSA_GROUP = NSA_HEADS // NSA_KV_HEADS
CMP_LEN = 32
CMP_STRIDE = 16
CMP_HIDDEN = 128
SEL_LEN = 64
SEL_TOPK = 16
WINDOW = 512
FORCE_BONUS = 1e6
SB_HEADS = 8
CONV_WIDTH = 3
BRANCH_WIDTH = 512

LANES = 128
SUBLANES = 8
VMEM_LIMIT = 56 * 1024 * 1024

TM_PROJ = 512
TM_FFN = 512
FF_TILE = 256
S5_CHUNK = 128
S5_LANES = 256
NSA_TQ = 128
NSA_TK_SEL = 256
SB_TQ = 128

_LOG_HEAD = HEAD_DIM.bit_length() - 1
_LOG_SEL = SEL_LEN.bit_length() - 1
NEG = -1e30
SB_EXIT = -104.0


def _cparams(sem):
    return pltpu.CompilerParams(dimension_semantics=sem, vmem_limit_bytes=VMEM_LIMIT)


def _const_spec(shape):
    n = len(shape)
    return pl.BlockSpec(shape, lambda *_: (0,) * n, pipeline_mode=pl.Buffered(1))


def _rms(x, g):
    return x * lax.rsqrt(jnp.mean(x * x, axis=-1, keepdims=True) + RMS_EPS) * g


def _dot(a, b):
    return jnp.dot(a, b, preferred_element_type=F32)


def _dot_nt(a, b):
    return lax.dot_general(a, b, (((1,), (1,)), ((), ())), preferred_element_type=F32)


def _split3(x):
    h = x.astype(BF16)
    r = x - h.astype(F32)
    m = r.astype(BF16)
    lo = (r - m.astype(F32)).astype(BF16)
    return h, m, lo


_C_U = 0
_C_ROPE = 512
_C_V = 1408
_C_G = 1792
_C_SB = 1920
_C_END = 3456


def _proj_kernel(x_ref, g_ref, w_ref, cos_ref, sin_ref,
                 u_ref, q_ref, kc_ref, ks_ref, kw_ref, vc_ref, vs_ref, vw_ref, gt_ref,
                 sq_ref, sk_ref, sv_ref):
    h = _rms(x_ref[...], g_ref[...]).astype(BF16)

    def mm(a, b):
        return _dot(h, w_ref[:, a:b])

    u_ref[...] = mm(_C_U, _C_ROPE)

    r = mm(_C_ROPE, _C_V)
    cos = cos_ref[...]
    sin = sin_ref[...]
    lane = lax.broadcasted_iota(jnp.int32, cos.shape, 1)
    first = (lane & (HEAD_DIM - 1)) < (HEAD_DIM // 2)
    roped = []
    for c in range((_C_V - _C_ROPE) // LANES):
        rc = r[:, c * LANES:(c + 1) * LANES]
        partner = jnp.where(first, pltpu.roll(rc, LANES - HEAD_DIM // 2, 1),
                            pltpu.roll(rc, HEAD_DIM // 2, 1))
        roped.append(rc * cos + partner * sin)
    scale = HEAD_DIM ** -0.5
    for c in range(4):
        q_ref[:, c * LANES:(c + 1) * LANES] = (roped[c] * scale).astype(BF16)
    kc_ref[...] = roped[4]
    ks_ref[...] = roped[5].astype(BF16)
    kw_ref[...] = roped[6].astype(BF16)

    v = mm(_C_V, _C_G)
    vc_ref[...] = v[:, 0:LANES]
    vs_ref[...] = v[:, LANES:2 * LANES].astype(BF16)
    vw_ref[...] = v[:, 2 * LANES:3 * LANES].astype(BF16)

    gt_ref[...] = jax.nn.sigmoid(mm(_C_G, _C_SB))

    sb = mm(_C_SB, _C_END)
    sq_ref[...] = (sb[:, 0:512] * scale).astype(BF16)
    sk_ref[...] = sb[:, 512:1024].astype(BF16)
    sv_ref[...] = sb[:, 1024:1536].astype(BF16)


def _proj_in(x2, g, w_packed, cos_t, sin_t, seq):
    t, d = x2.shape
    tm = min(TM_PROJ, seq)
    n_pos = seq // tm
    row = lambda i: (i, 0)
    pos = lambda i: (i % n_pos, 0)
    widths = [(512, F32), (512, BF16), (128, F32), (128, BF16), (128, BF16),
              (128, F32), (128, BF16), (128, BF16), (128, F32),
              (512, BF16), (512, BF16), (512, BF16)]
    return pl.pallas_call(
        _proj_kernel,
        grid=(t // tm,),
        in_specs=[pl.BlockSpec((tm, d), row),
                  _const_spec((1, d)),
                  _const_spec(w_packed.shape),
                  pl.BlockSpec((tm, LANES), pos),
                  pl.BlockSpec((tm, LANES), pos)],
        out_specs=[pl.BlockSpec((tm, w), row) for w, _ in widths],
        out_shape=[jax.ShapeDtypeStruct((t, w), dt) for w, dt in widths],
        compiler_params=_cparams(("parallel",)),
        name="proj_in",
    )(x2, g, w_packed, cos_t, sin_t)


def _s5_kernel(u_ref, bcat_ref, ccat_ref, pw_re_ref, pw_im_ref, cy_re_ref, cy_im_ref,
               d_ref, wglu_ref, o_ref, car_re, car_im):
    @pl.when(pl.program_id(1) == 0)
    def _():
        car_re[...] = jnp.zeros_like(car_re)
        car_im[...] = jnp.zeros_like(car_im)

    u = u_ref[...]
    ub = u.astype(BF16)
    chunk = u.shape[0]
    n_steps = chunk.bit_length() - 1
    n_tiles = bcat_ref.shape[0]
    row = lax.broadcasted_iota(jnp.int32, (chunk, S5_LANES), 0)
    y_blocks = [None] * (n_tiles // 2)
    for j in range(n_tiles):
        blk = j // 2
        x = _dot(ub[:, blk * LANES:(blk + 1) * LANES], bcat_ref[j])
        hr = x[:, :S5_LANES]
        hi = x[:, S5_LANES:]
        for k in range(n_steps):
            sh = 1 << k
            keep = row >= sh
            sr = jnp.where(keep, pltpu.roll(hr, sh, 0), 0.0)
            si = jnp.where(keep, pltpu.roll(hi, sh, 0), 0.0)
            cr = pw_re_ref[j, k:k + 1, :]
            ci = pw_im_ref[j, k:k + 1, :]
            hr, hi = hr + cr * sr - ci * si, hi + cr * si + ci * sr
        pr = cy_re_ref[j]
        pi = cy_im_ref[j]
        c_re = car_re[j:j + 1, :]
        c_im = car_im[j:j + 1, :]
        hr, hi = hr + pr * c_re - pi * c_im, hi + pr * c_im + pi * c_re
        car_re[j:j + 1, :] = hr[chunk - 1:chunk, :]
        car_im[j:j + 1, :] = hi[chunk - 1:chunk, :]
        hcat = jnp.concatenate([hr, hi], axis=1).astype(BF16)
        yj = _dot(hcat, ccat_ref[j])
        y_blocks[blk] = yj if y_blocks[blk] is None else y_blocks[blk] + yj
    y = jnp.concatenate(y_blocks, axis=1) + d_ref[...] * u
    z = jax.nn.gelu(y)
    o_ref[...] = (z * jax.nn.sigmoid(_dot(z.astype(BF16), wglu_ref[...]))).astype(o_ref.dtype)


def _s5_tables(lam_re, lam_im, log_step, b_re, b_im, c_re, c_im, chunk):
    g_n, p_n = lam_re.shape
    i_n = b_re.shape[-1]
    width = g_n * i_n
    lanes_total = g_n * p_n
    n_tiles = lanes_total // S5_LANES
    step = jnp.exp(log_step.astype(F32))[:, None]
    lr = lam_re.astype(F32)
    li = lam_im.astype(F32)
    mag = jnp.exp(lr * step)
    a_re = mag * jnp.cos(li * step)
    a_im = mag * jnp.sin(li * step)
    den = lr * lr + li * li
    nr = a_re - 1.0
    f_re = (nr * lr + a_im * li) / den
    f_im = (a_im * lr - nr * li) / den
    br = b_re.astype(F32)
    bi = b_im.astype(F32)
    bb_re = f_re[..., None] * br - f_im[..., None] * bi
    bb_im = f_re[..., None] * bi + f_im[..., None] * br
    eye = jnp.eye(g_n, dtype=F32)

    def in_map(bb):
        return (eye[:, None, :, None] * bb.transpose(0, 2, 1)[:, :, None, :]).reshape(width, lanes_total)

    def out_map(c):
        return (eye[:, None, :, None] * c.transpose(0, 2, 1)[:, :, None, :]).reshape(lanes_total, width)

    bd_re, bd_im = in_map(bb_re), in_map(bb_im)
    cd_re, cd_im = out_map(c_re.astype(F32)), out_map(c_im.astype(F32))
    bcat, ccat = [], []
    for j in range(n_tiles):
        rows = slice((j // 2) * LANES, (j // 2 + 1) * LANES)
        cols = slice(j * S5_LANES, (j + 1) * S5_LANES)
        bcat.append(jnp.concatenate([bd_re[rows, cols], bd_im[rows, cols]], axis=1))
        ccat.append(jnp.concatenate([cd_re[cols, rows], -cd_im[cols, rows]], axis=0))
    bcat = jnp.stack(bcat).astype(BF16)
    ccat = jnp.stack(ccat).astype(BF16)

    log_mag = (lr * step).reshape(1, lanes_total)
    ang = (li * step).reshape(1, lanes_total)

    def powers(n):
        m = jnp.exp(n * log_mag)
        return m * jnp.cos(n * ang), m * jnp.sin(n * ang)

    n_steps = chunk.bit_length() - 1
    pw_re, pw_im = powers((2.0 ** jnp.arange(n_steps, dtype=F32))[:, None])
    cy_re, cy_im = powers(jnp.arange(1, chunk + 1, dtype=F32)[:, None])

    def tiles(a):
        return a.reshape(a.shape[0], n_tiles, S5_LANES).transpose(1, 0, 2)

    return bcat, ccat, tiles(pw_re), tiles(pw_im), tiles(cy_re), tiles(cy_im)


def _s5(u3, tables, d_row, wglu):
    bsz, seq, width = u3.shape
    bcat, ccat, pw_re, pw_im, cy_re, cy_im = tables
    chunk = cy_re.shape[1]
    n_tiles = bcat.shape[0]
    return pl.pallas_call(
        _s5_kernel,
        grid=(bsz, seq // chunk),
        in_specs=[pl.BlockSpec((None, chunk, width), lambda b, c: (b, c, 0)),
                  _const_spec(bcat.shape), _const_spec(ccat.shape),
                  _const_spec(pw_re.shape), _const_spec(pw_im.shape),
                  _const_spec(cy_re.shape), _const_spec(cy_im.shape),
                  _const_spec(d_row.shape), _const_spec(wglu.shape)],
        out_specs=pl.BlockSpec((None, chunk, width), lambda b, c: (b, c, 0)),
        out_shape=jax.ShapeDtypeStruct((bsz, seq, width), BF16),
        scratch_shapes=[pltpu.VMEM((n_tiles, S5_LANES), F32), pltpu.VMEM((n_tiles, S5_LANES), F32)],
        compiler_params=_cparams(("parallel", "arbitrary")),
        name="s5_scan",
    )(u3, bcat, ccat, pw_re, pw_im, cy_re, cy_im, d_row, wglu)


def _compress_kernel(k_ref, v_ref, w1a_k, w1b_k, pa_k, pb_k, w2_k, w1a_v, w1b_v, pa_v, pb_v, w2_v,
                     kc_ref, vc_ref):
    def one(t_ref, w1a, w1b, pa, pb, w2, o_ref):
        r = t_ref[...]
        n = r.shape[0]
        first = _dot((r + pa[...]).astype(BF16), w1a[...])
        second = _dot((r + pb[...]).astype(BF16), w1b[...])
        rowi = lax.broadcasted_iota(jnp.int32, second.shape, 0)
        nxt = jnp.where(rowi < n - 1, pltpu.roll(second, n - 1, 0), 0.0)
        hid = jax.nn.gelu(first + nxt)
        o_ref[...] = _dot(hid.astype(BF16), w2[...]).astype(o_ref.dtype)

    one(k_ref, w1a_k, w1b_k, pa_k, pb_k, w2_k, kc_ref)
    one(v_ref, w1a_v, w1b_v, pa_v, pb_v, w2_v, vc_ref)


def _compress_weights(w1, w2, pos):
    half = CMP_LEN // 2
    eye = jnp.eye(NSA_KV_HEADS, dtype=F32)

    def first_layer(w):
        m = w[:, None, :, None, :] * eye[None, :, None, :, None]
        return m.reshape(half * NSA_KV_HEADS * HEAD_DIM, NSA_KV_HEADS * CMP_HIDDEN).astype(BF16)

    def pos_row(p):
        return jnp.broadcast_to(p[:, None, :], (half, NSA_KV_HEADS, HEAD_DIM)).reshape(1, -1).astype(F32)

    w2_bd = (eye[:, None, :, None] * w2[None, :, None, :]).reshape(
        NSA_KV_HEADS * CMP_HIDDEN, NSA_KV_HEADS * HEAD_DIM).astype(BF16)
    return (first_layer(w1[:half]), first_layer(w1[half:]), pos_row(pos[:half]), pos_row(pos[half:]), w2_bd)


def _compress(kc_raw, vc_raw, wk, wv):
    bsz, n_rows, feat = kc_raw.shape
    consts = list(wk) + list(wv)
    blk = pl.BlockSpec((None, n_rows, feat), lambda b: (b, 0, 0))
    out = pl.BlockSpec((None, n_rows, LANES), lambda b: (b, 0, 0))
    return pl.pallas_call(
        _compress_kernel,
        grid=(bsz,),
        in_specs=[blk, blk] + [_const_spec(c.shape) for c in consts],
        out_specs=[out, out],
        out_shape=[jax.ShapeDtypeStruct((bsz, n_rows, LANES), BF16)] * 2,
        compiler_params=_cparams(("parallel",)),
        name="nsa_compress",
    )(kc_raw, vc_raw, *consts)


def _nsa_kernel(q_ref, kc_ref, vc_ref, ks_ref, vs_ref, kw_ref, vw_ref, gt_ref, ovt_ref, o_ref,
                m_scr, l_scr, acc_scr, o0_scr, *, seq):
    tq = q_ref.shape[0]
    rows = NSA_GROUP * tq
    n_cmp_rows = kc_ref.shape[0]
    n_sel = seq // SEL_LEN
    t0 = pl.program_id(1) * tq
    i_tile = pl.program_id(1)

    q = q_ref[...]
    gt = gt_ref[...]
    lane = lax.broadcasted_iota(jnp.int32, (1, LANES), 1)
    t_col = t0 + lax.broadcasted_iota(jnp.int32, (tq, 1), 0)
    t_row = t0 + lax.broadcasted_iota(jnp.int32, (1, tq), 1)

    def tile_rows(a):
        return jnp.concatenate([a] * NSA_GROUP, axis=0)

    def init_state():
        m_scr[...] = jnp.full_like(m_scr, NEG)
        l_scr[...] = jnp.zeros_like(l_scr)
        acc_scr[...] = jnp.zeros_like(acc_scr)

    def online_step(q4, kt, vt, bias):
        tk = kt.shape[0]
        s = _dot_nt(q4, kt) + tile_rows(bias)
        m_prev = m_scr[...]
        m_next = jnp.maximum(m_prev, jnp.max(s, axis=1, keepdims=True))
        alpha = jnp.exp(m_prev - m_next)
        p = jnp.exp(s - jnp.concatenate([m_next] * (tk // LANES), axis=1))
        l_scr[...] = alpha * l_scr[...] + jnp.sum(p, axis=1, keepdims=True)
        acc_scr[...] = alpha * acc_scr[...] + _dot(p.astype(BF16), vt)
        m_scr[...] = m_next

    for h in range(NSA_KV_HEADS):
        in_head = (lane >> _LOG_HEAD) == h
        q4 = jnp.concatenate(
            [jnp.where(in_head, q[:, g * LANES:(g + 1) * LANES], jnp.zeros((), BF16)) for g in range(NSA_GROUP)],
            axis=0)
        gates = [jnp.concatenate([gt[:, (h * NSA_GROUP + g) * 3 + br:(h * NSA_GROUP + g) * 3 + br + 1]
                                  for g in range(NSA_GROUP)], axis=0) for br in range(3)]

        s = _dot_nt(q4, kc_ref[...])
        n_col = lax.broadcasted_iota(jnp.int32, (1, n_cmp_rows), 1)
        vis = (n_col * CMP_STRIDE + (CMP_LEN - 1)) <= t_col
        s = s + tile_rows(jnp.where(vis, 0.0, NEG))
        vis4 = tile_rows(jnp.where(vis, 1.0, 0.0))
        m = jnp.max(s, axis=1, keepdims=True)
        e = jnp.exp(s - m) * vis4
        p = e / jnp.maximum(jnp.sum(e, axis=1, keepdims=True), 1e-30)
        o_cmp = _dot(p.astype(BF16), vc_ref[...])

        p_sum = p[0:tq]
        for g in range(1, NSA_GROUP):
            p_sum = p_sum + p[g * tq:(g + 1) * tq]
        ov_t = ovt_ref[...]
        p_sel_t = sum(_dot_nt(ov_t, part) for part in _split3(p_sum))
        blk = lax.broadcasted_iota(jnp.int32, (n_sel, 1), 0)
        t_blk = t_row >> _LOG_SEL
        valid = blk <= t_blk
        forced = (blk == 0) | (blk == t_blk) | (blk == t_blk - 1)
        rank_t = jnp.where(valid, p_sel_t + FORCE_BONUS * jnp.where(forced, 1.0, 0.0), -FORCE_BONUS)
        cnt = jnp.zeros((n_sel, tq), F32)
        for ii in range(n_sel):
            ri = rank_t[ii:ii + 1, :]
            ahead = jnp.where(blk > ii, jnp.where(ri >= rank_t, 1.0, 0.0), jnp.where(ri > rank_t, 1.0, 0.0))
            cnt = cnt + ahead
        sel_t = jnp.where(cnt < float(min(SEL_TOPK, n_sel)), 1.0, 0.0)
        if n_sel < LANES:
            sel_t = jnp.concatenate([sel_t, jnp.zeros((LANES - n_sel, tq), F32)], axis=0)
        sel = sel_t.T.astype(BF16)

        init_state()
        blk_col = lax.broadcasted_iota(jnp.int32, (LANES, 1), 0)

        def sel_body(j, _):
            k0 = pl.multiple_of(j * NSA_TK_SEL, NSA_TK_SEL)
            kpos = k0 + lax.broadcasted_iota(jnp.int32, (1, NSA_TK_SEL), 1)
            expand = jnp.where((kpos >> _LOG_SEL) == blk_col, 1.0, 0.0).astype(BF16)
            chosen = _dot(sel, expand)
            bias = jnp.where((chosen > 0.5) & (kpos <= t_col), 0.0, NEG)
            online_step(q4, ks_ref[pl.ds(k0, NSA_TK_SEL), :], vs_ref[pl.ds(k0, NSA_TK_SEL), :], bias)
            return 0

        n_key_tiles = (t0 + tq + NSA_TK_SEL - 1) // NSA_TK_SEL
        lax.fori_loop(0, n_key_tiles, sel_body, 0)
        o_sel = acc_scr[...] / l_scr[...]

        init_state()

        def win_body(j, _):
            k0 = pl.multiple_of(j * tq, tq)
            kpos = k0 + lax.broadcasted_iota(jnp.int32, (1, tq), 1)
            dist = t_col - kpos
            bias = jnp.where((dist >= 0) & (dist < WINDOW), 0.0, NEG)
            online_step(q4, kw_ref[pl.ds(k0, tq), :], vw_ref[pl.ds(k0, tq), :], bias)
            return 0

        lax.fori_loop(jnp.maximum(i_tile - WINDOW // tq, 0), i_tile + 1, win_body, 0)
        o_win = acc_scr[...] / l_scr[...]

        o_h = gates[0] * o_cmp + gates[1] * o_sel + gates[2] * o_win
        if h == 0:
            o0_scr[...] = o_h
        else:
            o0 = o0_scr[...]
            for g in range(NSA_GROUP):
                o_ref[:, g * LANES:(g + 1) * LANES] = jnp.where(
                    lane < HEAD_DIM, o0[g * tq:(g + 1) * tq], o_h[g * tq:(g + 1) * tq]).astype(o_ref.dtype)


def _nsa(q3, kc, vc, ks, vs, kw, vw, gt, ov_t):
    bsz, seq, qw = q3.shape
    tq = min(NSA_TQ, seq)
    n_rows = kc.shape[1]
    full = lambda w: pl.BlockSpec((None, seq, w), lambda b, i: (b, 0, 0))
    tile = lambda w: pl.BlockSpec((None, tq, w), lambda b, i: (b, i, 0))
    cmp_spec = pl.BlockSpec((None, n_rows, LANES), lambda b, i: (b, 0, 0))
    rows = NSA_GROUP * tq
    return pl.pallas_call(
        functools.partial(_nsa_kernel, seq=seq),
        grid=(bsz, seq // tq),
        in_specs=[tile(qw), cmp_spec, cmp_spec, full(LANES), full(LANES), full(LANES), full(LANES),
                  tile(LANES), _const_spec(ov_t.shape)],
        out_specs=tile(qw),
        out_shape=jax.ShapeDtypeStruct((bsz, seq, qw), BF16),
        scratch_shapes=[pltpu.VMEM((rows, LANES), F32), pltpu.VMEM((rows, LANES), F32),
                        pltpu.VMEM((rows, LANES), F32), pltpu.VMEM((rows, LANES), F32)],
        compiler_params=_cparams(("parallel", "arbitrary")),
        name="nsa_attention",
    )(q3, kc, vc, ks, vs, kw, vw, gt, ov_t)


def _sb_kernel(q_ref, k_ref, v_ref, o_ref, r_scr, acc_scr):
    tq = q_ref.shape[0]
    i_tile = pl.program_id(1)
    t0 = i_tile * tq
    q = q_ref[...]
    lane = lax.broadcasted_iota(jnp.int32, (1, LANES), 1)
    t_col = t0 + lax.broadcasted_iota(jnp.int32, (tq, 1), 0)
    jr = lax.broadcasted_iota(jnp.int32, (tq, tq), 0)
    jc = lax.broadcasted_iota(jnp.int32, (tq, tq), 1)
    upper = jnp.where(jr > jc, 1.0, 0.0).astype(BF16)
    qm = []
    for h in range(SB_HEADS):
        in_head = (lane >> _LOG_HEAD) == (h % 2)
        pair = q[:, (h // 2) * LANES:(h // 2 + 1) * LANES]
        qm.append(jnp.where(in_head, pair, jnp.zeros((), BF16)))
    r_scr[...] = jnp.zeros_like(r_scr)
    acc_scr[...] = jnp.zeros_like(acc_scr)

    def cond(c):
        j, r_max = c
        return (j >= 0) & (r_max > SB_EXIT)

    def body(c):
        j, _ = c
        k0 = pl.multiple_of(j * tq, tq)
        kpos = k0 + lax.broadcasted_iota(jnp.int32, (1, tq), 1)
        causal = kpos < t_col
        r_max = jnp.full((), NEG, F32)
        for h in range(SB_HEADS):
            pr = slice((h // 2) * LANES, (h // 2 + 1) * LANES)
            kt = k_ref[pl.ds(k0, tq), pr]
            vt = v_ref[pl.ds(k0, tq), pr]
            z = _dot_nt(qm[h], kt)
            ls = jnp.minimum(z, 0.0) - jnp.log1p(jnp.exp(-jnp.abs(z)))
            rest = jnp.where(causal, ls - z, 0.0)
            hi, lo = rest.astype(BF16), None
            lo = (rest - hi.astype(F32)).astype(BF16)
            after = _dot(hi, upper) + _dot(lo, upper) + r_scr[h]
            w = jnp.where(causal, jnp.exp(ls + after), 0.0)
            acc_scr[h] = acc_scr[h] + _dot(w.astype(BF16), vt)
            r_new = r_scr[h] + jnp.sum(rest, axis=1, keepdims=True)
            r_scr[h] = r_new
            r_max = jnp.maximum(r_max, jnp.max(r_new))
        return j - 1, r_max

    lax.while_loop(cond, body, (i_tile, jnp.zeros((), F32)))
    for hp in range(SB_HEADS // 2):
        o_ref[:, hp * LANES:(hp + 1) * LANES] = jnp.where(
            lane < HEAD_DIM, acc_scr[2 * hp], acc_scr[2 * hp + 1]).astype(o_ref.dtype)


def _sb(q3, k3, v3):
    bsz, seq, w = q3.shape
    tq = min(SB_TQ, seq)
    full = pl.BlockSpec((None, seq, w), lambda b, i: (b, 0, 0))
    tile = pl.BlockSpec((None, tq, w), lambda b, i: (b, i, 0))
    return pl.pallas_call(
        _sb_kernel,
        grid=(bsz, seq // tq),
        in_specs=[tile, full, full],
        out_specs=tile,
        out_shape=jax.ShapeDtypeStruct((bsz, seq, w), BF16),
        scratch_shapes=[pltpu.VMEM((SB_HEADS, tq, LANES), F32), pltpu.VMEM((SB_HEADS, tq, LANES), F32)],
        compiler_params=_cparams(("parallel", "arbitrary")),
        name="sb_attention",
    )(q3, k3, v3)


def _merge_kernel(x_ref, g_ref, a_ref, b_ref, c_ref, wg_ref, wb_ref, wo_ref, o_ref):
    x = x_ref[...]
    d = x.shape[1]
    h = _rms(x, g_ref[...]).astype(BF16)
    mixed = None
    for n, br in enumerate((a_ref, b_ref, c_ref)):
        gate = jax.nn.sigmoid(_dot(h, wg_ref[:, n * d:(n + 1) * d]))
        y = gate * _dot(br[...], wb_ref[n])
        mixed = y if mixed is None else mixed + y
    o_ref[...] = x + _dot(mixed.astype(BF16), wo_ref[...])


def _merge(x2, g, o_a, o_b, o_c, w_gate, w_branch, w_out):
    t, d = x2.shape
    tm = min(TM_PROJ, t)
    row = lambda i: (i, 0)
    bw = o_a.shape[1]
    return pl.pallas_call(
        _merge_kernel,
        grid=(t // tm,),
        in_specs=[pl.BlockSpec((tm, d), row), _const_spec((1, d)),
                  pl.BlockSpec((tm, bw), row), pl.BlockSpec((tm, bw), row), pl.BlockSpec((tm, bw), row),
                  _const_spec(w_gate.shape), _const_spec(w_branch.shape), _const_spec(w_out.shape)],
        out_specs=pl.BlockSpec((tm, d), row),
        out_shape=jax.ShapeDtypeStruct((t, d), F32),
        compiler_params=_cparams(("parallel",)),
        name="gated_merge",
    )(x2, g, o_a, o_b, o_c, w_gate, w_branch, w_out)


def _ffn_kernel(x_ref, halo_ref, g_ref, wup_ref, cw_ref, cb_ref, wdn_ref, o_ref, *, tiles_per_seq):
    x = x_ref[...]
    tm = x.shape[0]
    d_ff = wdn_ref.shape[0]
    at_start = (pl.program_id(0) % tiles_per_seq) == 0
    g = g_ref[...]
    h_halo = jnp.where(at_start, 0.0, _rms(halo_ref[...], g))
    h = jnp.concatenate([h_halo, _rms(x, g)], axis=0).astype(BF16)
    acc = jnp.zeros(x.shape, F32)
    for f in range(d_ff // FF_TILE):
        halves = []
        for base in (0, d_ff):
            cols = slice(base + f * FF_TILE, base + (f + 1) * FF_TILE)
            u = _dot(h, wup_ref[:, cols])
            w = cw_ref[:, cols]
            conv = (w[2:3] * u[SUBLANES:]
                    + w[1:2] * pltpu.roll(u, 1, 0)[SUBLANES:]
                    + w[0:1] * pltpu.roll(u, 2, 0)[SUBLANES:]
                    + cb_ref[:, cols])
            halves.append(conv)
        act = (jax.nn.silu(halves[0]) * halves[1]).astype(BF16)
        acc = acc + _dot(act, wdn_ref[f * FF_TILE:(f + 1) * FF_TILE, :])
    o_ref[...] = x + acc


def _ffn(x2, g, w_up, conv_w, conv_b, w_down, seq):
    t, d = x2.shape
    tm = min(TM_FFN, seq)
    tiles_per_seq = seq // tm
    row = lambda i: (i, 0)
    halo = lambda i: (jnp.maximum(i * (tm // SUBLANES) - 1, 0), 0)
    return pl.pallas_call(
        functools.partial(_ffn_kernel, tiles_per_seq=tiles_per_seq),
        grid=(t // tm,),
        in_specs=[pl.BlockSpec((tm, d), row), pl.BlockSpec((SUBLANES, d), halo), _const_spec((1, d)),
                  _const_spec(w_up.shape), _const_spec(conv_w.shape), _const_spec(conv_b.shape),
                  _const_spec(w_down.shape)],
        out_specs=pl.BlockSpec((tm, d), row),
        out_shape=jax.ShapeDtypeStruct((t, d), F32),
        compiler_params=_cparams(("parallel",)),
        name="conv_ffn",
    )(x2, x2, g, w_up, conv_w, conv_b, w_down)


def _norm_kernel(x_ref, g_ref, o_ref):
    o_ref[...] = _rms(x_ref[...], g_ref[...])


def _final_norm(x2, g):
    t, d = x2.shape
    tm = min(TM_PROJ, t)
    return pl.pallas_call(
        _norm_kernel,
        grid=(t // tm,),
        in_specs=[pl.BlockSpec((tm, d), lambda i: (i, 0)), _const_spec((1, d))],
        out_specs=pl.BlockSpec((tm, d), lambda i: (i, 0)),
        out_shape=jax.ShapeDtypeStruct((t, d), F32),
        compiler_params=_cparams(("parallel",)),
        name="final_norm",
    )(x2, g)


def _nsa_slot_order():
    return [NSA_GROUP * (s % 2) + s // 2 for s in range(NSA_HEADS)]


def _pack_w_in(w):
    d = w.shape[0]
    o_q, o_kv, o_g, o_sb, o_gm = 512, 1024, 1792, 1816, 3352
    u = w[:, 0:o_q]
    q = w[:, o_q:o_kv].reshape(d, NSA_HEADS, HEAD_DIM)[:, jnp.array(_nsa_slot_order())].reshape(d, -1)
    kv = w[:, o_kv:o_g].reshape(d, 3, 2, NSA_KV_HEADS * HEAD_DIM)
    k = kv[:, :, 0].reshape(d, -1)
    v = kv[:, :, 1].reshape(d, -1)
    gates = jnp.pad(w[:, o_g:o_sb], ((0, 0), (0, LANES - (o_sb - o_g))))
    sb = w[:, o_sb:o_gm]
    packed = jnp.concatenate([u, q, k, v, gates, sb], axis=1).astype(BF16)
    return packed, w[:, o_gm:].astype(BF16)


def _rope_tables(seq):
    inv_freq = 1.0 / (ROPE_THETA ** (jnp.arange(0, HEAD_DIM, 2, dtype=F32) / HEAD_DIM))
    ang = jnp.arange(seq, dtype=F32)[:, None] * inv_freq[None, :]
    cos, sin = jnp.cos(ang), jnp.sin(ang)
    cos_t = jnp.tile(cos, (1, LANES // (HEAD_DIM // 2)))
    sin_t = jnp.tile(jnp.concatenate([-sin, sin], axis=1), (1, LANES // HEAD_DIM))
    return cos_t, sin_t


def _overlap_t(seq):
    n_rows = seq // CMP_STRIDE
    n_sel = seq // SEL_LEN
    cmp_start = jnp.arange(n_rows) * CMP_STRIDE
    sel_start = jnp.arange(n_sel) * SEL_LEN
    ov = ((cmp_start[None, :] < sel_start[:, None] + SEL_LEN)
          & (cmp_start[None, :] + CMP_LEN > sel_start[:, None])
          & (jnp.arange(n_rows)[None, :] < (seq - CMP_LEN) // CMP_STRIDE + 1))
    return ov.astype(BF16)


def kernel(x, norm_mix, w_in, ssm_lam_re, ssm_lam_im, ssm_log_step, ssm_b_re, ssm_b_im, ssm_c_re, ssm_c_im, ssm_d, ssm_w_glu, cmp_w1_k, cmp_w2_k, cmp_pos_k, cmp_w1_v, cmp_w2_v, cmp_pos_v, w_branch, w_out, norm_ffn, ffn_w_up, ffn_conv_w, ffn_conv_b, ffn_w_down, norm_final):
    bsz, seq, d = x.shape
    depth = w_in.shape[0]
    t = bsz * seq
    assert seq % 512 == 0 and seq // SEL_LEN <= LANES
    cos_t, sin_t = _rope_tables(seq)
    ov_t = _overlap_t(seq)
    slot_rows = jnp.array(_nsa_slot_order())
    x2 = x.reshape(t, d).astype(F32)
    for l in range(depth):
        w_packed, w_gate = _pack_w_in(w_in[l])
        (u, q, kc_raw, ks, kw, vc_raw, vs, vw, gt, sq, sk, sv) = _proj_in(
            x2, norm_mix[l].reshape(1, d), w_packed, cos_t, sin_t, seq)
        b3 = lambda a: a.reshape(bsz, seq, a.shape[-1])
        tables = _s5_tables(ssm_lam_re[l], ssm_lam_im[l], ssm_log_step[l], ssm_b_re[l], ssm_b_im[l],
                            ssm_c_re[l], ssm_c_im[l], min(S5_CHUNK, seq))
        o_ssm = _s5(b3(u), tables, ssm_d[l].reshape(1, -1).astype(F32), ssm_w_glu[l].astype(BF16))
        rows16 = lambda a: a.reshape(bsz, seq // CMP_STRIDE, CMP_STRIDE * a.shape[-1])
        kc, vc = _compress(rows16(kc_raw), rows16(vc_raw),
                           _compress_weights(cmp_w1_k[l], cmp_w2_k[l], cmp_pos_k[l]),
                           _compress_weights(cmp_w1_v[l], cmp_w2_v[l], cmp_pos_v[l]))
        o_nsa = _nsa(b3(q), kc, vc, b3(ks), b3(vs), b3(kw), b3(vw), b3(gt), ov_t)
        o_sb = _sb(b3(sq), b3(sk), b3(sv))
        wb = w_branch[l]
        wb_nsa = wb[1].reshape(NSA_HEADS, HEAD_DIM, d)[slot_rows].reshape(BRANCH_WIDTH, d)
        wb_all = jnp.stack([wb[0], wb_nsa, wb[2]]).astype(BF16)
        x2 = _merge(x2, norm_mix[l].reshape(1, d), o_ssm.reshape(t, -1), o_nsa.reshape(t, -1),
                    o_sb.reshape(t, -1), w_gate, wb_all, w_out[l].astype(BF16))
        x2 = _ffn(x2, norm_ffn[l].reshape(1, d), ffn_w_up[l].astype(BF16), ffn_conv_w[l].astype(F32),
                  ffn_conv_b[l].reshape(1, -1).astype(F32), ffn_w_down[l].astype(BF16), seq)
    return _final_norm(x2, norm_final.reshape(1, d)).reshape(bsz, seq, d).astype(x.dtype)
```

```python
import functools
import math

import jax
import jax.numpy as jnp
from jax import lax
from jax.experimental import pallas as pl
from jax.experimental.pallas import tpu as pltpu

F32 = jnp.float32
BF16 = jnp.bfloat16

HEAD_DIM = 64
RMS_EPS = 1e-6
ROPE_THETA = 10000.0
SSM_GROUP = 16
SSM_STATE = 64
NSA_HEADS = 8
NSA_KV_HEADS = 2
NSA_GROUP = NSA_HEADS // NSA_KV_HEADS
CMP_LEN = 32
CMP_STRIDE = 16
CMP_HIDDEN = 128
SEL_LEN = 64
SEL_TOPK = 16
WINDOW = 512
FORCE_BONUS = 1e6
SB_HEADS = 8
CONV_WIDTH = 3
BRANCH_WIDTH = 512

LANES = 128
SUBLANES = 8
VMEM_LIMIT = 56 * 1024 * 1024

TM_PROJ = 512
TM_FFN = 512
FF_TILE = 256
S5_CHUNK = 128
S5_LANES = 256
NSA_TQ = 128
NSA_TK_SEL = 512
NSA_TK_WIN = 256
SB_TQ = 128

_LOG_HEAD = HEAD_DIM.bit_length() - 1
_LOG_SEL = SEL_LEN.bit_length() - 1
NEG = -1e30
SB_EXIT = -104.0


def _cparams(sem):
    return pltpu.CompilerParams(dimension_semantics=sem, vmem_limit_bytes=VMEM_LIMIT)


def _const_spec(shape):
    n = len(shape)
    return pl.BlockSpec(shape, lambda *_: (0,) * n, pipeline_mode=pl.Buffered(1))


def _rms(x, g):
    return x * lax.rsqrt(jnp.mean(x * x, axis=-1, keepdims=True) + RMS_EPS) * g


def _dot(a, b):
    return jnp.dot(a, b, preferred_element_type=F32)


def _dot_nt(a, b):
    return lax.dot_general(a, b, (((1,), (1,)), ((), ())), preferred_element_type=F32)


def _split3(x):
    h = x.astype(BF16)
    r = x - h.astype(F32)
    m = r.astype(BF16)
    lo = (r - m.astype(F32)).astype(BF16)
    return h, m, lo


_C_U = 0
_C_ROPE = 512
_C_V = 1408
_C_G = 1536
_C_SB = 1664
_C_END = 2688
_R_VS, _R_VW, _R_SV, _R_END = 0, 128, 256, 768


def _proj_kernel(x_ref, g_ref, w_ref, wvt_ref, cos_ref, sin_ref,
                 u_ref, q_ref, kc_ref, ks_ref, kw_ref, vc_ref, vst_ref, vwt_ref, gt_ref,
                 sq_ref, sk_ref, svt_ref):
    h = _rms(x_ref[...], g_ref[...]).astype(BF16)

    def mm(a, b):
        return _dot(h, w_ref[:, a:b])

    vt = _dot_nt(wvt_ref[...], h)
    vst_ref[...] = vt[_R_VS:_R_VW].astype(BF16)
    vwt_ref[...] = vt[_R_VW:_R_SV].astype(BF16)
    svt_ref[...] = vt[_R_SV:_R_END].astype(BF16)

    u_ref[...] = mm(_C_U, _C_ROPE)

    r = mm(_C_ROPE, _C_V)
    cos = cos_ref[...]
    sin = sin_ref[...]
    lane = lax.broadcasted_iota(jnp.int32, cos.shape, 1)
    first = (lane & (HEAD_DIM - 1)) < (HEAD_DIM // 2)
    roped = []
    for c in range((_C_V - _C_ROPE) // LANES):
        rc = r[:, c * LANES:(c + 1) * LANES]
        partner = jnp.where(first, pltpu.roll(rc, LANES - HEAD_DIM // 2, 1),
                            pltpu.roll(rc, HEAD_DIM // 2, 1))
        roped.append(rc * cos + partner * sin)
    scale = HEAD_DIM ** -0.5
    for c in range(4):
        q_ref[:, c * LANES:(c + 1) * LANES] = (roped[c] * scale).astype(BF16)
    kc_ref[...] = roped[4]
    ks_ref[...] = roped[5].astype(BF16)
    kw_ref[...] = roped[6].astype(BF16)

    vc_ref[...] = mm(_C_V, _C_G)
    gt_ref[...] = jax.nn.sigmoid(mm(_C_G, _C_SB))

    sb = mm(_C_SB, _C_END)
    sq_ref[...] = (sb[:, 0:512] * scale).astype(BF16)
    sk_ref[...] = sb[:, 512:1024].astype(BF16)


def _proj_in(x2, g, w_packed, w_vt, cos_t, sin_t, bsz, seq):
    t, d = x2.shape
    tm = min(TM_PROJ, seq)
    n_pos = seq // tm
    row = lambda i: (i, 0)
    pos = lambda i: (i % n_pos, 0)
    col = lambda i: (i // n_pos, 0, i % n_pos)
    outs = [(512, F32, False), (512, BF16, False), (128, F32, False), (128, BF16, False), (128, BF16, False),
            (128, F32, False), (128, BF16, True), (128, BF16, True), (128, F32, False),
            (512, BF16, False), (512, BF16, False), (512, BF16, True)]
    return pl.pallas_call(
        _proj_kernel,
        grid=(t // tm,),
        in_specs=[pl.BlockSpec((tm, d), row),
                  _const_spec((1, d)),
                  _const_spec(w_packed.shape),
                  _const_spec(w_vt.shape),
                  pl.BlockSpec((tm, LANES), pos),
                  pl.BlockSpec((tm, LANES), pos)],
        out_specs=[pl.BlockSpec((None, w, tm), col) if tr else pl.BlockSpec((tm, w), row) for w, _, tr in outs],
        out_shape=[jax.ShapeDtypeStruct((bsz, w, seq) if tr else (t, w), dt) for w, dt, tr in outs],
        compiler_params=_cparams(("parallel",)),
        name="proj_in",
    )(x2, g, w_packed, w_vt, cos_t, sin_t)


def _s5_kernel(u_ref, bcat_ref, ccat_ref, pw_re_ref, pw_im_ref, cy_re_ref, cy_im_ref,
               d_ref, wglu_ref, o_ref, car_re, car_im):
    @pl.when(pl.program_id(1) == 0)
    def _():
        car_re[...] = jnp.zeros_like(car_re)
        car_im[...] = jnp.zeros_like(car_im)

    u = u_ref[...]
    ub = u.astype(BF16)
    chunk = u.shape[0]
    n_steps = pw_re_ref.shape[1]
    n_tiles = bcat_ref.shape[0]
    n_sub = chunk // SUBLANES
    y_blocks = [None] * (n_tiles // 2)
    for j in range(n_tiles):
        blk = j // 2
        x = _dot(ub[:, blk * LANES:(blk + 1) * LANES], bcat_ref[j])
        hr = x[:, :S5_LANES].reshape(n_sub, SUBLANES, S5_LANES)
        hi = x[:, S5_LANES:].reshape(n_sub, SUBLANES, S5_LANES)
        for k in range(n_steps):
            sr = pltpu.roll(hr, 1 << k, 1)
            si = pltpu.roll(hi, 1 << k, 1)
            cr = pw_re_ref[j, k]
            ci = pw_im_ref[j, k]
            hr, hi = hr + cr * sr - ci * si, hi + cr * si + ci * sr
        pr = cy_re_ref[j]
        pi = cy_im_ref[j]
        c_re = car_re[j:j + 1, :]
        c_im = car_im[j:j + 1, :]
        tiles_re, tiles_im = [], []
        for v in range(n_sub):
            t_re = hr[v] + pr * c_re - pi * c_im
            t_im = hi[v] + pr * c_im + pi * c_re
            c_re = t_re[SUBLANES - 1:SUBLANES, :]
            c_im = t_im[SUBLANES - 1:SUBLANES, :]
            tiles_re.append(t_re)
            tiles_im.append(t_im)
        car_re[j:j + 1, :] = c_re
        car_im[j:j + 1, :] = c_im
        hcat = jnp.concatenate([jnp.concatenate(tiles_re, axis=0), jnp.concatenate(tiles_im, axis=0)],
                               axis=1).astype(BF16)
        yj = _dot(hcat, ccat_ref[j])
        y_blocks[blk] = yj if y_blocks[blk] is None else y_blocks[blk] + yj
    y = jnp.concatenate(y_blocks, axis=1) + d_ref[...] * u
    z = jax.nn.gelu(y)
    o_ref[...] = (z * jax.nn.sigmoid(_dot(z.astype(BF16), wglu_ref[...]))).astype(o_ref.dtype)


def _s5_tables(lam_re, lam_im, log_step, b_re, b_im, c_re, c_im):
    g_n, p_n = lam_re.shape
    i_n = b_re.shape[-1]
    width = g_n * i_n
    lanes_total = g_n * p_n
    n_tiles = lanes_total // S5_LANES
    step = jnp.exp(log_step.astype(F32))[:, None]
    lr = lam_re.astype(F32)
    li = lam_im.astype(F32)
    mag = jnp.exp(lr * step)
    a_re = mag * jnp.cos(li * step)
    a_im = mag * jnp.sin(li * step)
    den = lr * lr + li * li
    nr = a_re - 1.0
    f_re = (nr * lr + a_im * li) / den
    f_im = (a_im * lr - nr * li) / den
    br = b_re.astype(F32)
    bi = b_im.astype(F32)
    bb_re = f_re[..., None] * br - f_im[..., None] * bi
    bb_im = f_re[..., None] * bi + f_im[..., None] * br
    eye = jnp.eye(g_n, dtype=F32)

    def in_map(bb):
        return (eye[:, None, :, None] * bb.transpose(0, 2, 1)[:, :, None, :]).reshape(width, lanes_total)

    def out_map(c):
        return (eye[:, None, :, None] * c.transpose(0, 2, 1)[:, :, None, :]).reshape(lanes_total, width)

    bd_re, bd_im = in_map(bb_re), in_map(bb_im)
    cd_re, cd_im = out_map(c_re.astype(F32)), out_map(c_im.astype(F32))
    bcat, ccat = [], []
    for j in range(n_tiles):
        rows = slice((j // 2) * LANES, (j // 2 + 1) * LANES)
        cols = slice(j * S5_LANES, (j + 1) * S5_LANES)
        bcat.append(jnp.concatenate([bd_re[rows, cols], bd_im[rows, cols]], axis=1))
        ccat.append(jnp.concatenate([cd_re[cols, rows], -cd_im[cols, rows]], axis=0))
    bcat = jnp.stack(bcat).astype(BF16)
    ccat = jnp.stack(ccat).astype(BF16)

    log_mag = (lr * step).reshape(1, lanes_total)
    ang = (li * step).reshape(1, lanes_total)

    def powers(n):
        m = jnp.exp(n * log_mag)
        return m * jnp.cos(n * ang), m * jnp.sin(n * ang)

    n_steps = SUBLANES.bit_length() - 1
    shifts = 2 ** jnp.arange(n_steps)
    pw_re, pw_im = powers(shifts.astype(F32)[:, None])
    cy_re, cy_im = powers(jnp.arange(1, SUBLANES + 1, dtype=F32)[:, None])

    def tiles(a):
        return a.reshape(a.shape[0], n_tiles, S5_LANES).transpose(1, 0, 2)

    live = (jnp.arange(SUBLANES)[None, :] >= shifts[:, None]).astype(F32)[None, :, :, None]
    pw_re = tiles(pw_re)[:, :, None, :] * live
    pw_im = tiles(pw_im)[:, :, None, :] * live
    return bcat, ccat, pw_re, pw_im, tiles(cy_re), tiles(cy_im)


def _s5(u3, tables, d_row, wglu):
    bsz, seq, width = u3.shape
    bcat, ccat, pw_re, pw_im, cy_re, cy_im = tables
    chunk = min(S5_CHUNK, seq)
    n_tiles = bcat.shape[0]
    return pl.pallas_call(
        _s5_kernel,
        grid=(bsz, seq // chunk),
        in_specs=[pl.BlockSpec((None, chunk, width), lambda b, c: (b, c, 0)),
                  _const_spec(bcat.shape), _const_spec(ccat.shape),
                  _const_spec(pw_re.shape), _const_spec(pw_im.shape),
                  _const_spec(cy_re.shape), _const_spec(cy_im.shape),
                  _const_spec(d_row.shape), _const_spec(wglu.shape)],
        out_specs=pl.BlockSpec((None, chunk, width), lambda b, c: (b, c, 0)),
        out_shape=jax.ShapeDtypeStruct((bsz, seq, width), BF16),
        scratch_shapes=[pltpu.VMEM((n_tiles, S5_LANES), F32), pltpu.VMEM((n_tiles, S5_LANES), F32)],
        compiler_params=_cparams(("parallel", "arbitrary")),
        name="s5_scan",
    )(u3, bcat, ccat, pw_re, pw_im, cy_re, cy_im, d_row, wglu)


def _compress_kernel(k_ref, v_ref, w1a_k, w1b_k, pa_k, pb_k, w2_k, w1a_v, w1b_v, pa_v, pb_v, w2_v,
                     kc_ref, vc_ref):
    def hidden(t_ref, w1a, w1b, pa, pb):
        r = t_ref[...]
        n = r.shape[0]
        first = _dot((r + pa[...]).astype(BF16), w1a[...])
        second = _dot((r + pb[...]).astype(BF16), w1b[...])
        rowi = lax.broadcasted_iota(jnp.int32, second.shape, 0)
        nxt = jnp.where(rowi < n - 1, pltpu.roll(second, n - 1, 0), 0.0)
        return jax.nn.gelu(first + nxt).astype(BF16)

    kc_ref[...] = _dot(hidden(k_ref, w1a_k, w1b_k, pa_k, pb_k), w2_k[...]).astype(kc_ref.dtype)
    vc_ref[...] = _dot_nt(w2_v[...], hidden(v_ref, w1a_v, w1b_v, pa_v, pb_v)).astype(vc_ref.dtype)


def _compress_weights(w1, w2, pos):
    half = CMP_LEN // 2
    eye = jnp.eye(NSA_KV_HEADS, dtype=F32)

    def first_layer(w):
        m = w[:, None, :, None, :] * eye[None, :, None, :, None]
        return m.reshape(half * NSA_KV_HEADS * HEAD_DIM, NSA_KV_HEADS * CMP_HIDDEN).astype(BF16)

    def pos_row(p):
        return jnp.broadcast_to(p[:, None, :], (half, NSA_KV_HEADS, HEAD_DIM)).reshape(1, -1).astype(F32)

    w2_bd = (eye[:, None, :, None] * w2[None, :, None, :]).reshape(
        NSA_KV_HEADS * CMP_HIDDEN, NSA_KV_HEADS * HEAD_DIM).astype(BF16)
    return (first_layer(w1[:half]), first_layer(w1[half:]), pos_row(pos[:half]), pos_row(pos[half:]), w2_bd)


def _compress(kc_raw, vc_raw, wk, wv):
    bsz, n_rows, feat = kc_raw.shape
    consts = list(wk) + list(wv[:4]) + [wv[4].T]
    blk = pl.BlockSpec((None, n_rows, feat), lambda b: (b, 0, 0))
    return pl.pallas_call(
        _compress_kernel,
        grid=(bsz,),
        in_specs=[blk, blk] + [_const_spec(c.shape) for c in consts],
        out_specs=[pl.BlockSpec((None, n_rows, LANES), lambda b: (b, 0, 0)),
                   pl.BlockSpec((None, LANES, n_rows), lambda b: (b, 0, 0))],
        out_shape=[jax.ShapeDtypeStruct((bsz, n_rows, LANES), BF16),
                   jax.ShapeDtypeStruct((bsz, LANES, n_rows), BF16)],
        compiler_params=_cparams(("parallel",)),
        name="nsa_compress",
    )(kc_raw, vc_raw, *consts)


def _nsa_kernel(q_ref, kc_ref, vct_ref, ks_ref, vst_ref, kw_ref, vwt_ref, gt_ref, ovt_ref, o_ref,
                m_scr, l_scr, acc_scr, out_scr, sel_scr, q4_scr, *, seq):
    tq = q_ref.shape[0]
    n_cmp_rows = kc_ref.shape[0]
    n_sel = seq // SEL_LEN
    i_tile = pl.program_id(1)
    t0 = i_tile * tq
    heads = range(NSA_KV_HEADS)

    q = q_ref[...]
    gt_t = gt_ref[...].T
    lane = lax.broadcasted_iota(jnp.int32, (1, LANES), 1)
    feat = lax.broadcasted_iota(jnp.int32, (LANES, 1), 0)
    t_row = t0 + lax.broadcasted_iota(jnp.int32, (1, tq), 1)

    def tile_lanes(a):
        return jnp.concatenate([a] * NSA_GROUP, axis=1)

    def init_state():
        m_scr[...] = jnp.full_like(m_scr, NEG)
        l_scr[...] = jnp.zeros_like(l_scr)
        acc_scr[...] = jnp.zeros_like(acc_scr)

    def online_step(h, kt, vt_t, bias):
        s = _dot_nt(kt, q4_scr[h]) + tile_lanes(bias)
        m_prev = m_scr[h]
        m_next = jnp.maximum(m_prev, jnp.max(s, axis=0, keepdims=True))
        alpha = jnp.exp(m_prev - m_next)
        p = jnp.exp(s - m_next)
        l_scr[h] = alpha * l_scr[h] + jnp.sum(p, axis=0, keepdims=True)
        acc_scr[h] = alpha * acc_scr[h] + _dot(vt_t, p.astype(BF16))
        m_scr[h] = m_next

    def finish(h):
        return acc_scr[h] * (1.0 / l_scr[h])

    def gate_row(h, br):
        return jnp.concatenate([gt_t[(h * NSA_GROUP + g) * 3 + br:(h * NSA_GROUP + g) * 3 + br + 1, :]
                                for g in range(NSA_GROUP)], axis=1)

    for h in heads:
        in_head = (lane >> _LOG_HEAD) == h
        q4 = jnp.concatenate(
            [jnp.where(in_head, q[:, g * LANES:(g + 1) * LANES], jnp.zeros((), BF16)) for g in range(NSA_GROUP)],
            axis=0)
        q4_scr[h] = q4

        s = _dot_nt(kc_ref[...], q4)
        n_col = lax.broadcasted_iota(jnp.int32, (n_cmp_rows, 1), 0)
        vis = (n_col * CMP_STRIDE + (CMP_LEN - 1)) <= t_row
        s = s + tile_lanes(jnp.where(vis, 0.0, NEG))
        vis4 = tile_lanes(jnp.where(vis, 1.0, 0.0))
        m = jnp.max(s, axis=0, keepdims=True)
        e = jnp.exp(s - m) * vis4
        p = e * (1.0 / jnp.maximum(jnp.sum(e, axis=0, keepdims=True), 1e-30))
        out_scr[h] = gate_row(h, 0) * _dot(vct_ref[...], p.astype(BF16))

        p_sum = p[:, 0:tq]
        for g in range(1, NSA_GROUP):
            p_sum = p_sum + p[:, g * tq:(g + 1) * tq]
        ov_t = ovt_ref[...]
        p_sel_t = sum(_dot(ov_t, part) for part in _split3(p_sum))
        blk = lax.broadcasted_iota(jnp.int32, (n_sel, 1), 0)
        t_blk = t_row >> _LOG_SEL
        valid = blk <= t_blk
        forced = (blk == 0) | (blk == t_blk) | (blk == t_blk - 1)
        rank_t = jnp.where(valid, p_sel_t + FORCE_BONUS * jnp.where(forced, 1.0, 0.0), -FORCE_BONUS)
        n_tiles = n_sel // SUBLANES
        rank_tiles = [rank_t[r * SUBLANES:(r + 1) * SUBLANES] for r in range(n_tiles)]
        cnt_tiles = [jnp.zeros((SUBLANES, tq), F32) for _ in range(n_tiles)]
        blk8 = lax.broadcasted_iota(jnp.int32, (SUBLANES, 1), 0)
        for ii in range(n_sel):
            ri = rank_t[ii:ii + 1, :]
            for r in range(n_tiles):
                if ii < r * SUBLANES:
                    ahead = ri >= rank_tiles[r]
                elif ii >= (r + 1) * SUBLANES:
                    ahead = ri > rank_tiles[r]
                else:
                    tie_ok = jnp.where(blk8 + r * SUBLANES > ii, 1.0, 0.0)
                    ahead = (ri > rank_tiles[r]) | ((ri == rank_tiles[r]) & (tie_ok > 0.5))
                cnt_tiles[r] = cnt_tiles[r] + jnp.where(ahead, 1.0, 0.0)
        top = float(min(SEL_TOPK, n_sel))
        for r in range(n_tiles):
            sel_bias = jnp.where(cnt_tiles[r] < top, 0.0, NEG)
            for b in range(SUBLANES):
                sel_scr[h, r * SUBLANES + b] = jnp.broadcast_to(sel_bias[b:b + 1, :], (SUBLANES, tq))

    init_state()
    tk = NSA_TK_SEL
    kpos_sel = lax.broadcasted_iota(jnp.int32, (tk, 1), 0)
    blocks_per_tile = tk // SEL_LEN

    def sel_body(j, _):
        k0 = pl.multiple_of(j * tk, tk)
        causal = (k0 + kpos_sel) <= t_row
        kt = ks_ref[pl.ds(k0, tk), :]
        vt_t = vst_ref[:, pl.ds(k0, tk)]
        for h in heads:
            chosen = jnp.concatenate(
                [sel_scr[h, j * blocks_per_tile + b] for b in range(blocks_per_tile)
                 for _ in range(SEL_LEN // SUBLANES)], axis=0)
            online_step(h, kt, vt_t, jnp.where(causal, chosen, NEG))
        return 0

    lax.fori_loop(0, (t0 + tq + tk - 1) // tk, sel_body, 0)
    for h in heads:
        out_scr[h] = out_scr[h] + gate_row(h, 1) * finish(h)

    init_state()
    tkw = NSA_TK_WIN
    kpos_win = lax.broadcasted_iota(jnp.int32, (tkw, 1), 0)

    def win_body(j, _):
        k0 = pl.multiple_of(j * tkw, tkw)
        dist = t_row - (k0 + kpos_win)
        bias = jnp.where((dist >= 0) & (dist < WINDOW), 0.0, NEG)
        kt = kw_ref[pl.ds(k0, tkw), :]
        vt_t = vwt_ref[:, pl.ds(k0, tkw)]
        for h in heads:
            online_step(h, kt, vt_t, bias)
        return 0

    lax.fori_loop(jnp.maximum(t0 - (WINDOW - 1), 0) // tkw, (t0 + tq + tkw - 1) // tkw, win_body, 0)
    o_h = [out_scr[h] + gate_row(h, 2) * finish(h) for h in heads]
    both = jnp.where(feat < HEAD_DIM, o_h[0], o_h[1])
    for g in range(NSA_GROUP):
        o_ref[:, g * LANES:(g + 1) * LANES] = both[:, g * tq:(g + 1) * tq].T.astype(o_ref.dtype)


def _nsa(q3, kc, vc_t, ks, vs_t, kw, vw_t, gt, ov_t):
    bsz, seq, qw = q3.shape
    tq = NSA_TQ
    n_rows = kc.shape[1]
    n_sel = seq // SEL_LEN
    full = pl.BlockSpec((None, seq, LANES), lambda b, i: (b, 0, 0))
    full_t = pl.BlockSpec((None, LANES, seq), lambda b, i: (b, 0, 0))
    tile = lambda w: pl.BlockSpec((None, tq, w), lambda b, i: (b, i, 0))
    cols = NSA_GROUP * tq
    return pl.pallas_call(
        functools.partial(_nsa_kernel, seq=seq),
        grid=(bsz, seq // tq),
        in_specs=[tile(qw),
                  pl.BlockSpec((None, n_rows, LANES), lambda b, i: (b, 0, 0)),
                  pl.BlockSpec((None, LANES, n_rows), lambda b, i: (b, 0, 0)),
                  full, full_t, full, full_t, tile(LANES), _const_spec(ov_t.shape)],
        out_specs=tile(qw),
        out_shape=jax.ShapeDtypeStruct((bsz, seq, qw), BF16),
        scratch_shapes=[pltpu.VMEM((NSA_KV_HEADS, 1, cols), F32), pltpu.VMEM((NSA_KV_HEADS, 1, cols), F32),
                        pltpu.VMEM((NSA_KV_HEADS, LANES, cols), F32), pltpu.VMEM((NSA_KV_HEADS, LANES, cols), F32),
                        pltpu.VMEM((NSA_KV_HEADS, n_sel, SUBLANES, tq), F32),
                        pltpu.VMEM((NSA_KV_HEADS, cols, LANES), BF16)],
        compiler_params=_cparams(("parallel", "arbitrary")),
        name="nsa_attention",
    )(q3, kc, vc_t, ks, vs_t, kw, vw_t, gt, ov_t)


def _sb_kernel(q_ref, k_ref, vt_ref, o_ref, r_scr, acc_scr):
    tq = q_ref.shape[0]
    tk = tq
    i_tile = pl.program_id(1)
    t0 = i_tile * tq
    q = q_ref[...]
    lane = lax.broadcasted_iota(jnp.int32, (1, LANES), 1)
    feat = lax.broadcasted_iota(jnp.int32, (LANES, 1), 0)
    t_row = t0 + lax.broadcasted_iota(jnp.int32, (1, tq), 1)
    kpos_col = lax.broadcasted_iota(jnp.int32, (tk, 1), 0)
    sr = lax.broadcasted_iota(jnp.int32, (tk, tk), 0)
    sc = lax.broadcasted_iota(jnp.int32, (tk, tk), 1)
    later = jnp.where(sc > sr, 1.0, 0.0).astype(BF16)
    qm = []
    for h in range(SB_HEADS):
        in_head = (lane >> _LOG_HEAD) == (h % 2)
        pair = q[:, (h // 2) * LANES:(h // 2 + 1) * LANES]
        qm.append(jnp.where(in_head, pair, jnp.zeros((), BF16)))
    r_scr[...] = jnp.zeros_like(r_scr)
    acc_scr[...] = jnp.zeros_like(acc_scr)

    def cond(c):
        j, r_max = c
        return (j >= 0) & (r_max > SB_EXIT)

    def body(c):
        j, _ = c
        k0 = pl.multiple_of(j * tk, tk)
        kt = k_ref[pl.ds(k0, tk), :]
        z = jnp.concatenate([_dot_nt(kt[:, (h // 2) * LANES:(h // 2 + 1) * LANES], qm[h])
                             for h in range(SB_HEADS)], axis=1)
        causal = jnp.where((k0 + kpos_col) < t_row, 1.0, 0.0)
        causal = jnp.concatenate([causal] * SB_HEADS, axis=1)
        ls = jnp.minimum(z, 0.0) - jnp.log(1.0 + jnp.exp(-jnp.abs(z)))
        rest = (ls - z) * causal
        hi = rest.astype(BF16)
        lo = (rest - hi.astype(F32)).astype(BF16)
        r_prev = r_scr[...]
        after = _dot(later, hi) + _dot(later, lo) + r_prev
        w = (jnp.exp(ls + after) * causal).astype(BF16)
        for h in range(SB_HEADS):
            cols = slice(h * tq, (h + 1) * tq)
            vt = vt_ref[(h // 2) * LANES:(h // 2 + 1) * LANES, pl.ds(k0, tk)]
            acc_scr[:, cols] = acc_scr[:, cols] + _dot(vt, w[:, cols])
        r_new = r_prev + jnp.sum(rest, axis=0, keepdims=True)
        r_scr[...] = r_new
        return j - 1, jnp.max(r_new)

    lax.while_loop(cond, body, (i_tile, jnp.zeros((), F32)))
    for hp in range(SB_HEADS // 2):
        even = acc_scr[:, (2 * hp) * tq:(2 * hp + 1) * tq]
        odd = acc_scr[:, (2 * hp + 1) * tq:(2 * hp + 2) * tq]
        o_ref[:, hp * LANES:(hp + 1) * LANES] = jnp.where(feat < HEAD_DIM, even, odd).T.astype(o_ref.dtype)


def _sb(q3, k3, v_t):
    bsz, seq, w = q3.shape
    tq = SB_TQ
    full = pl.BlockSpec((None, seq, w), lambda b, i: (b, 0, 0))
    full_t = pl.BlockSpec((None, w, seq), lambda b, i: (b, 0, 0))
    tile = pl.BlockSpec((None, tq, w), lambda b, i: (b, i, 0))
    return pl.pallas_call(
        _sb_kernel,
        grid=(bsz, seq // tq),
        in_specs=[tile, full, full_t],
        out_specs=tile,
        out_shape=jax.ShapeDtypeStruct((bsz, seq, w), BF16),
        scratch_shapes=[pltpu.VMEM((1, SB_HEADS * tq), F32), pltpu.VMEM((LANES, SB_HEADS * tq), F32)],
        compiler_params=_cparams(("parallel", "arbitrary")),
        name="sb_attention",
    )(q3, k3, v_t)


def _merge_kernel(x_ref, g_ref, a_ref, b_ref, c_ref, wg_ref, wb_ref, wo_ref, o_ref):
    x = x_ref[...]
    d = x.shape[1]
    h = _rms(x, g_ref[...]).astype(BF16)
    mixed = None
    for n, br in enumerate((a_ref, b_ref, c_ref)):
        gate = jax.nn.sigmoid(_dot(h, wg_ref[:, n * d:(n + 1) * d]))
        y = gate * _dot(br[...], wb_ref[n])
        mixed = y if mixed is None else mixed + y
    o_ref[...] = x + _dot(mixed.astype(BF16), wo_ref[...])


def _merge(x2, g, o_a, o_b, o_c, w_gate, w_branch, w_out):
    t, d = x2.shape
    tm = min(TM_PROJ, t)
    row = lambda i: (i, 0)
    bw = o_a.shape[1]
    return pl.pallas_call(
        _merge_kernel,
        grid=(t // tm,),
        in_specs=[pl.BlockSpec((tm, d), row), _const_spec((1, d)),
                  pl.BlockSpec((tm, bw), row), pl.BlockSpec((tm, bw), row), pl.BlockSpec((tm, bw), row),
                  _const_spec(w_gate.shape), _const_spec(w_branch.shape), _const_spec(w_out.shape)],
        out_specs=pl.BlockSpec((tm, d), row),
        out_shape=jax.ShapeDtypeStruct((t, d), F32),
        compiler_params=_cparams(("parallel",)),
        name="gated_merge",
    )(x2, g, o_a, o_b, o_c, w_gate, w_branch, w_out)


def _ffn_kernel(x_ref, halo_ref, g_ref, wup_ref, cw_ref, cb_ref, wdn_ref, o_ref, *, tiles_per_seq):
    x = x_ref[...]
    tm = x.shape[0]
    d_ff = wdn_ref.shape[0]
    at_start = (pl.program_id(0) % tiles_per_seq) == 0
    g = g_ref[...]
    h_halo = jnp.where(at_start, 0.0, _rms(halo_ref[...], g))
    h = jnp.concatenate([h_halo, _rms(x, g)], axis=0).astype(BF16)
    acc = jnp.zeros(x.shape, F32)
    for f in range(d_ff // FF_TILE):
        halves = []
        for base in (0, d_ff):
            cols = slice(base + f * FF_TILE, base + (f + 1) * FF_TILE)
            u = _dot(h, wup_ref[:, cols])
            w = cw_ref[:, cols]
            conv = (w[2:3] * u[SUBLANES:]
                    + w[1:2] * pltpu.roll(u, 1, 0)[SUBLANES:]
                    + w[0:1] * pltpu.roll(u, 2, 0)[SUBLANES:]
                    + cb_ref[:, cols])
            halves.append(conv)
        act = (jax.nn.silu(halves[0]) * halves[1]).astype(BF16)
        acc = acc + _dot(act, wdn_ref[f * FF_TILE:(f + 1) * FF_TILE, :])
    o_ref[...] = x + acc


def _ffn(x2, g, w_up, conv_w, conv_b, w_down, seq):
    t, d = x2.shape
    tm = min(TM_FFN, seq)
    tiles_per_seq = seq // tm
    row = lambda i: (i, 0)
    halo = lambda i: (jnp.maximum(i * (tm // SUBLANES) - 1, 0), 0)
    return pl.pallas_call(
        functools.partial(_ffn_kernel, tiles_per_seq=tiles_per_seq),
        grid=(t // tm,),
        in_specs=[pl.BlockSpec((tm, d), row), pl.BlockSpec((SUBLANES, d), halo), _const_spec((1, d)),
                  _const_spec(w_up.shape), _const_spec(conv_w.shape), _const_spec(conv_b.shape),
                  _const_spec(w_down.shape)],
        out_specs=pl.BlockSpec((tm, d), row),
        out_shape=jax.ShapeDtypeStruct((t, d), F32),
        compiler_params=_cparams(("parallel",)),
        name="conv_ffn",
    )(x2, x2, g, w_up, conv_w, conv_b, w_down)


def _norm_kernel(x_ref, g_ref, o_ref):
    o_ref[...] = _rms(x_ref[...], g_ref[...])


def _final_norm(x2, g):
    t, d = x2.shape
    tm = min(TM_PROJ, t)
    return pl.pallas_call(
        _norm_kernel,
        grid=(t // tm,),
        in_specs=[pl.BlockSpec((tm, d), lambda i: (i, 0)), _const_spec((1, d))],
        out_specs=pl.BlockSpec((tm, d), lambda i: (i, 0)),
        out_shape=jax.ShapeDtypeStruct((t, d), F32),
        compiler_params=_cparams(("parallel",)),
        name="final_norm",
    )(x2, g)


def _nsa_slot_order():
    return [NSA_GROUP * (s % 2) + s // 2 for s in range(NSA_HEADS)]


def _pack_w_in(w):
    d = w.shape[0]
    o_q, o_kv, o_g, o_sb, o_gm = 512, 1024, 1792, 1816, 3352
    u = w[:, 0:o_q]
    q = w[:, o_q:o_kv].reshape(d, NSA_HEADS, HEAD_DIM)[:, jnp.array(_nsa_slot_order())].reshape(d, -1)
    kv = w[:, o_kv:o_g].reshape(d, 3, 2, NSA_KV_HEADS * HEAD_DIM)
    k = kv[:, :, 0].reshape(d, -1)
    v = kv[:, :, 1]
    gates = jnp.pad(w[:, o_g:o_sb], ((0, 0), (0, LANES - (o_sb - o_g))))
    sb = w[:, o_sb:o_gm]
    sb_w = SB_HEADS * HEAD_DIM
    packed = jnp.concatenate([u, q, k, v[:, 0], gates, sb[:, :2 * sb_w]], axis=1).astype(BF16)
    w_vt = jnp.concatenate([v[:, 1], v[:, 2], sb[:, 2 * sb_w:]], axis=1).T.astype(BF16)
    return packed, w_vt, w[:, o_gm:].astype(BF16)


def _rope_tables(seq):
    inv_freq = 1.0 / (ROPE_THETA ** (jnp.arange(0, HEAD_DIM, 2, dtype=F32) / HEAD_DIM))
    ang = jnp.arange(seq, dtype=F32)[:, None] * inv_freq[None, :]
    cos, sin = jnp.cos(ang), jnp.sin(ang)
    cos_t = jnp.tile(cos, (1, LANES // (HEAD_DIM // 2)))
    sin_t = jnp.tile(jnp.concatenate([-sin, sin], axis=1), (1, LANES // HEAD_DIM))
    return cos_t, sin_t


def _overlap_t(seq):
    n_rows = seq // CMP_STRIDE
    n_sel = seq // SEL_LEN
    cmp_start = jnp.arange(n_rows) * CMP_STRIDE
    sel_start = jnp.arange(n_sel) * SEL_LEN
    ov = ((cmp_start[None, :] < sel_start[:, None] + SEL_LEN)
          & (cmp_start[None, :] + CMP_LEN > sel_start[:, None])
          & (jnp.arange(n_rows)[None, :] < (seq - CMP_LEN) // CMP_STRIDE + 1))
    return ov.astype(BF16)


def kernel(x, norm_mix, w_in, ssm_lam_re, ssm_lam_im, ssm_log_step, ssm_b_re, ssm_b_im, ssm_c_re, ssm_c_im, ssm_d, ssm_w_glu, cmp_w1_k, cmp_w2_k, cmp_pos_k, cmp_w1_v, cmp_w2_v, cmp_pos_v, w_branch, w_out, norm_ffn, ffn_w_up, ffn_conv_w, ffn_conv_b, ffn_w_down, norm_final):
    bsz, seq, d = x.shape
    depth = w_in.shape[0]
    t = bsz * seq
    assert seq % 512 == 0 and seq // SEL_LEN <= LANES
    cos_t, sin_t = _rope_tables(seq)
    ov_t = _overlap_t(seq)
    slot_rows = jnp.array(_nsa_slot_order())
    x2 = x.reshape(t, d).astype(F32)
    for l in range(depth):
        w_packed, w_vt, w_gate = _pack_w_in(w_in[l])
        (u, q, kc_raw, ks, kw, vc_raw, vs_t, vw_t, gt, sq, sk, sv_t) = _proj_in(
            x2, norm_mix[l].reshape(1, d), w_packed, w_vt, cos_t, sin_t, bsz, seq)
        b3 = lambda a: a.reshape(bsz, seq, a.shape[-1])
        tables = _s5_tables(ssm_lam_re[l], ssm_lam_im[l], ssm_log_step[l], ssm_b_re[l], ssm_b_im[l],
                            ssm_c_re[l], ssm_c_im[l])
        o_ssm = _s5(b3(u), tables, ssm_d[l].reshape(1, -1).astype(F32), ssm_w_glu[l].astype(BF16))
        rows16 = lambda a: a.reshape(bsz, seq // CMP_STRIDE, CMP_STRIDE * a.shape[-1])
        kc, vc_t = _compress(rows16(kc_raw), rows16(vc_raw),
                             _compress_weights(cmp_w1_k[l], cmp_w2_k[l], cmp_pos_k[l]),
                             _compress_weights(cmp_w1_v[l], cmp_w2_v[l], cmp_pos_v[l]))
        o_nsa = _nsa(b3(q), kc, vc_t, b3(ks), vs_t, b3(kw), vw_t, b3(gt), ov_t)
        o_sb = _sb(b3(sq), b3(sk), sv_t)
        wb = w_branch[l]
        wb_nsa = wb[1].reshape(NSA_HEADS, HEAD_DIM, d)[slot_rows].reshape(BRANCH_WIDTH, d)
        wb_all = jnp.stack([wb[0], wb_nsa, wb[2]]).astype(BF16)
        x2 = _merge(x2, norm_mix[l].reshape(1, d), o_ssm.reshape(t, -1), o_nsa.reshape(t, -1),
                    o_sb.reshape(t, -1), w_gate, wb_all, w_out[l].astype(BF16))
        x2 = _ffn(x2, norm_ffn[l].reshape(1, d), ffn_w_up[l].astype(BF16), ffn_conv_w[l].astype(F32),
                  ffn_conv_b[l].reshape(1, -1).astype(F32), ffn_w_down[l].astype(BF16), seq)
    return _final_norm(x2, norm_final.reshape(1, d)).reshape(bsz, seq, d).astype(x.dtype)
```

```python
import functools
import math

import jax
import jax.numpy as jnp
from jax import lax
from jax.experimental import pallas as pl
from jax.experimental.pallas import tpu as pltpu

F32 = jnp.float32
BF16 = jnp.bfloat16

HEAD_DIM = 64
RMS_EPS = 1e-6
ROPE_THETA = 10000.0
SSM_GROUP = 16
SSM_STATE = 64
NSA_HEADS = 8
NSA_KV_HEADS = 2
NSA_GROUP = NSA_HEADS // NSA_KV_HEADS
CMP_LEN = 32
CMP_STRIDE = 16
CMP_HIDDEN = 128
SEL_LEN = 64
SEL_TOPK = 16
WINDOW = 512
FORCE_BONUS = 1e6
SB_HEADS = 8
CONV_WIDTH = 3
BRANCH_WIDTH = 512

LANES = 128
SUBLANES = 8
VMEM_LIMIT = 56 * 1024 * 1024

TM_PROJ = 512
TM_FFN = 512
FF_TILE = 256
S5_CHUNK = 128
S5_LANES = 256
NSA_TQ = 256
NSA_TK_SEL = 512
NSA_TK_WIN = 256
SB_TQ = 128

_LOG_HEAD = HEAD_DIM.bit_length() - 1
_LOG_SEL = SEL_LEN.bit_length() - 1
NEG = -1e30
SB_EXIT = -104.0


def _cparams(sem):
    return pltpu.CompilerParams(dimension_semantics=sem, vmem_limit_bytes=VMEM_LIMIT)


def _const_spec(shape):
    n = len(shape)
    return pl.BlockSpec(shape, lambda *_: (0,) * n, pipeline_mode=pl.Buffered(1))


def _rms(x, g):
    return x * lax.rsqrt(jnp.mean(x * x, axis=-1, keepdims=True) + RMS_EPS) * g


def _dot(a, b):
    return jnp.dot(a, b, preferred_element_type=F32)


def _dot_nt(a, b):
    return lax.dot_general(a, b, (((1,), (1,)), ((), ())), preferred_element_type=F32)


def _split3(x):
    h = x.astype(BF16)
    r = x - h.astype(F32)
    m = r.astype(BF16)
    lo = (r - m.astype(F32)).astype(BF16)
    return h, m, lo


_C_U = 0
_C_ROPE = 512
_C_V = 1408
_C_G = 1536
_C_SB = 1664
_C_END = 2688
_R_VS, _R_VW, _R_SV, _R_END = 0, 128, 256, 768


def _proj_kernel(x_ref, g_ref, w_ref, wvt_ref, cos_ref, sin_ref,
                 u_ref, q_ref, kc_ref, ksa_ref, kw_ref, vc_ref, vsa_ref, vwa_ref, gt_ref,
                 sq_ref, sk_ref, svt_ref, *, n_pos):
    tm = x_ref.shape[0]
    h = _rms(x_ref[...], g_ref[...]).astype(BF16)

    def mm(a, b):
        return _dot(h, w_ref[:, a:b])

    vt = _dot_nt(wvt_ref[...], h)
    ones_rows = jnp.where(lax.broadcasted_iota(jnp.int32, (HEAD_DIM, tm), 0) == 0, 1.0, 0.0)
    for hh in range(NSA_KV_HEADS):
        vsa_ref[hh] = jnp.concatenate(
            [vt[_R_VS + hh * HEAD_DIM:_R_VS + (hh + 1) * HEAD_DIM], ones_rows], axis=0).astype(BF16)
        vwa_ref[hh] = jnp.concatenate(
            [vt[_R_VW + hh * HEAD_DIM:_R_VW + (hh + 1) * HEAD_DIM], ones_rows], axis=0).astype(BF16)
    svt_ref[...] = vt[_R_SV:_R_END].astype(BF16)

    u_ref[...] = mm(_C_U, _C_ROPE)

    r = mm(_C_ROPE, _C_V)
    cos = cos_ref[...]
    sin = sin_ref[...]
    lane = lax.broadcasted_iota(jnp.int32, cos.shape, 1)
    first = (lane & (HEAD_DIM - 1)) < (HEAD_DIM // 2)
    roped = []
    for c in range((_C_V - _C_ROPE) // LANES):
        rc = r[:, c * LANES:(c + 1) * LANES]
        partner = jnp.where(first, pltpu.roll(rc, LANES - HEAD_DIM // 2, 1),
                            pltpu.roll(rc, HEAD_DIM // 2, 1))
        roped.append(rc * cos + partner * sin)
    scale = HEAD_DIM ** -0.5
    for c in range(4):
        q_ref[:, c * LANES:(c + 1) * LANES] = (roped[c] * scale).astype(BF16)
    kc_ref[...] = roped[4]
    kw_ref[...] = roped[6].astype(BF16)
    pos = (pl.program_id(0) % n_pos) * tm + lax.broadcasted_iota(jnp.int32, (tm, 1), 0)
    blk = pos >> _LOG_SEL
    lane_k = lax.broadcasted_iota(jnp.int32, (1, LANES), 1)
    for hh in range(NSA_KV_HEADS):
        own = (lane_k >> _LOG_HEAD) == hh
        one_hot = jnp.where(blk == (lane_k & (HEAD_DIM - 1)), 1.0, 0.0)
        ksa_ref[hh] = jnp.where(own, roped[5], one_hot).astype(BF16)

    vc_ref[...] = mm(_C_V, _C_G)
    gt_ref[...] = jax.nn.sigmoid(mm(_C_G, _C_SB))

    sb = mm(_C_SB, _C_END)
    sq_ref[...] = (sb[:, 0:512] * scale).astype(BF16)
    sk_ref[...] = sb[:, 512:1024].astype(BF16)


def _proj_in(x2, g, w_packed, w_vt, cos_t, sin_t, bsz, seq):
    t, d = x2.shape
    tm = min(TM_PROJ, seq)
    n_pos = seq // tm
    row = lambda i: (i, 0)
    pos = lambda i: (i % n_pos, 0)
    nh = NSA_KV_HEADS

    def tok(w, dt):
        return pl.BlockSpec((tm, w), row), jax.ShapeDtypeStruct((t, w), dt)

    def feat_major(w, dt):
        return (pl.BlockSpec((None, w, tm), lambda i: (i // n_pos, 0, i % n_pos)),
                jax.ShapeDtypeStruct((bsz, w, seq), dt))

    def head_tok(dt):
        return (pl.BlockSpec((None, nh, tm, LANES), lambda i: (i // n_pos, 0, i % n_pos, 0)),
                jax.ShapeDtypeStruct((bsz, nh, seq, LANES), dt))

    def head_feat(dt):
        return (pl.BlockSpec((None, nh, LANES, tm), lambda i: (i // n_pos, 0, 0, i % n_pos)),
                jax.ShapeDtypeStruct((bsz, nh, LANES, seq), dt))

    outs = [tok(512, F32), tok(512, BF16), tok(128, F32), head_tok(BF16), tok(128, BF16),
            tok(128, F32), head_feat(BF16), head_feat(BF16), tok(128, F32),
            tok(512, BF16), tok(512, BF16), feat_major(512, BF16)]
    return pl.pallas_call(
        functools.partial(_proj_kernel, n_pos=n_pos),
        grid=(t // tm,),
        in_specs=[pl.BlockSpec((tm, d), row),
                  _const_spec((1, d)),
                  _const_spec(w_packed.shape),
                  _const_spec(w_vt.shape),
                  pl.BlockSpec((tm, LANES), pos),
                  pl.BlockSpec((tm, LANES), pos)],
        out_specs=[spec for spec, _ in outs],
        out_shape=[shape for _, shape in outs],
        compiler_params=_cparams(("parallel",)),
        name="proj_in",
    )(x2, g, w_packed, w_vt, cos_t, sin_t)


def _s5_kernel(u_ref, bcat_ref, ccat_ref, pw_re_ref, pw_im_ref, cy_re_ref, cy_im_ref,
               d_ref, wglu_ref, o_ref, car_re, car_im):
    @pl.when(pl.program_id(1) == 0)
    def _():
        car_re[...] = jnp.zeros_like(car_re)
        car_im[...] = jnp.zeros_like(car_im)

    u = u_ref[...]
    ub = u.astype(BF16)
    chunk = u.shape[0]
    n_steps = pw_re_ref.shape[1]
    n_tiles = bcat_ref.shape[0]
    n_sub = chunk // SUBLANES
    y_blocks = [None] * (n_tiles // 2)
    for j in range(n_tiles):
        blk = j // 2
        x = _dot(ub[:, blk * LANES:(blk + 1) * LANES], bcat_ref[j])
        hr = x[:, :S5_LANES].reshape(n_sub, SUBLANES, S5_LANES)
        hi = x[:, S5_LANES:].reshape(n_sub, SUBLANES, S5_LANES)
        for k in range(n_steps):
            sr = pltpu.roll(hr, 1 << k, 1)
            si = pltpu.roll(hi, 1 << k, 1)
            cr = pw_re_ref[j, k]
            ci = pw_im_ref[j, k]
            hr, hi = hr + cr * sr - ci * si, hi + cr * si + ci * sr
        pr = cy_re_ref[j]
        pi = cy_im_ref[j]
        c_re = car_re[j:j + 1, :]
        c_im = car_im[j:j + 1, :]
        tiles_re, tiles_im = [], []
        for v in range(n_sub):
            t_re = hr[v] + pr * c_re - pi * c_im
            t_im = hi[v] + pr * c_im + pi * c_re
            c_re = t_re[SUBLANES - 1:SUBLANES, :]
            c_im = t_im[SUBLANES - 1:SUBLANES, :]
            tiles_re.append(t_re)
            tiles_im.append(t_im)
        car_re[j:j + 1, :] = c_re
        car_im[j:j + 1, :] = c_im
        hcat = jnp.concatenate([jnp.concatenate(tiles_re, axis=0), jnp.concatenate(tiles_im, axis=0)],
                               axis=1).astype(BF16)
        yj = _dot(hcat, ccat_ref[j])
        y_blocks[blk] = yj if y_blocks[blk] is None else y_blocks[blk] + yj
    y = jnp.concatenate(y_blocks, axis=1) + d_ref[...] * u
    z = jax.nn.gelu(y)
    o_ref[...] = (z * jax.nn.sigmoid(_dot(z.astype(BF16), wglu_ref[...]))).astype(o_ref.dtype)


def _s5_tables(lam_re, lam_im, log_step, b_re, b_im, c_re, c_im):
    g_n, p_n = lam_re.shape
    i_n = b_re.shape[-1]
    width = g_n * i_n
    lanes_total = g_n * p_n
    n_tiles = lanes_total // S5_LANES
    step = jnp.exp(log_step.astype(F32))[:, None]
    lr = lam_re.astype(F32)
    li = lam_im.astype(F32)
    mag = jnp.exp(lr * step)
    a_re = mag * jnp.cos(li * step)
    a_im = mag * jnp.sin(li * step)
    den = lr * lr + li * li
    nr = a_re - 1.0
    f_re = (nr * lr + a_im * li) / den
    f_im = (a_im * lr - nr * li) / den
    br = b_re.astype(F32)
    bi = b_im.astype(F32)
    bb_re = f_re[..., None] * br - f_im[..., None] * bi
    bb_im = f_re[..., None] * bi + f_im[..., None] * br
    eye = jnp.eye(g_n, dtype=F32)

    def in_map(bb):
        return (eye[:, None, :, None] * bb.transpose(0, 2, 1)[:, :, None, :]).reshape(width, lanes_total)

    def out_map(c):
        return (eye[:, None, :, None] * c.transpose(0, 2, 1)[:, :, None, :]).reshape(lanes_total, width)

    bd_re, bd_im = in_map(bb_re), in_map(bb_im)
    cd_re, cd_im = out_map(c_re.astype(F32)), out_map(c_im.astype(F32))
    bcat, ccat = [], []
    for j in range(n_tiles):
        rows = slice((j // 2) * LANES, (j // 2 + 1) * LANES)
        cols = slice(j * S5_LANES, (j + 1) * S5_LANES)
        bcat.append(jnp.concatenate([bd_re[rows, cols], bd_im[rows, cols]], axis=1))
        ccat.append(jnp.concatenate([cd_re[cols, rows], -cd_im[cols, rows]], axis=0))
    bcat = jnp.stack(bcat).astype(BF16)
    ccat = jnp.stack(ccat).astype(BF16)

    log_mag = (lr * step).reshape(1, lanes_total)
    ang = (li * step).reshape(1, lanes_total)

    def powers(n):
        m = jnp.exp(n * log_mag)
        return m * jnp.cos(n * ang), m * jnp.sin(n * ang)

    n_steps = SUBLANES.bit_length() - 1
    shifts = 2 ** jnp.arange(n_steps)
    pw_re, pw_im = powers(shifts.astype(F32)[:, None])
    cy_re, cy_im = powers(jnp.arange(1, SUBLANES + 1, dtype=F32)[:, None])

    def tiles(a):
        return a.reshape(a.shape[0], n_tiles, S5_LANES).transpose(1, 0, 2)

    live = (jnp.arange(SUBLANES)[None, :] >= shifts[:, None]).astype(F32)[None, :, :, None]
    pw_re = tiles(pw_re)[:, :, None, :] * live
    pw_im = tiles(pw_im)[:, :, None, :] * live
    return bcat, ccat, pw_re, pw_im, tiles(cy_re), tiles(cy_im)


def _s5(u3, tables, d_row, wglu):
    bsz, seq, width = u3.shape
    bcat, ccat, pw_re, pw_im, cy_re, cy_im = tables
    chunk = min(S5_CHUNK, seq)
    n_tiles = bcat.shape[0]
    return pl.pallas_call(
        _s5_kernel,
        grid=(bsz, seq // chunk),
        in_specs=[pl.BlockSpec((None, chunk, width), lambda b, c: (b, c, 0)),
                  _const_spec(bcat.shape), _const_spec(ccat.shape),
                  _const_spec(pw_re.shape), _const_spec(pw_im.shape),
                  _const_spec(cy_re.shape), _const_spec(cy_im.shape),
                  _const_spec(d_row.shape), _const_spec(wglu.shape)],
        out_specs=pl.BlockSpec((None, chunk, width), lambda b, c: (b, c, 0)),
        out_shape=jax.ShapeDtypeStruct((bsz, seq, width), BF16),
        scratch_shapes=[pltpu.VMEM((n_tiles, S5_LANES), F32), pltpu.VMEM((n_tiles, S5_LANES), F32)],
        compiler_params=_cparams(("parallel", "arbitrary")),
        name="s5_scan",
    )(u3, bcat, ccat, pw_re, pw_im, cy_re, cy_im, d_row, wglu)


def _compress_kernel(k_ref, v_ref, w1a_k, w1b_k, pa_k, pb_k, w2_k, w1a_v, w1b_v, pa_v, pb_v, w2_v,
                     kc_ref, vc_ref):
    def hidden(t_ref, w1a, w1b, pa, pb):
        r = t_ref[...]
        n = r.shape[0]
        first = _dot((r + pa[...]).astype(BF16), w1a[...])
        second = _dot((r + pb[...]).astype(BF16), w1b[...])
        rowi = lax.broadcasted_iota(jnp.int32, second.shape, 0)
        nxt = jnp.where(rowi < n - 1, pltpu.roll(second, n - 1, 0), 0.0)
        return jax.nn.gelu(first + nxt).astype(BF16)

    kc_ref[...] = _dot(hidden(k_ref, w1a_k, w1b_k, pa_k, pb_k), w2_k[...]).astype(kc_ref.dtype)
    vc_ref[...] = _dot_nt(w2_v[...], hidden(v_ref, w1a_v, w1b_v, pa_v, pb_v)).astype(vc_ref.dtype)


def _compress_weights(w1, w2, pos):
    half = CMP_LEN // 2
    eye = jnp.eye(NSA_KV_HEADS, dtype=F32)

    def first_layer(w):
        m = w[:, None, :, None, :] * eye[None, :, None, :, None]
        return m.reshape(half * NSA_KV_HEADS * HEAD_DIM, NSA_KV_HEADS * CMP_HIDDEN).astype(BF16)

    def pos_row(p):
        return jnp.broadcast_to(p[:, None, :], (half, NSA_KV_HEADS, HEAD_DIM)).reshape(1, -1).astype(F32)

    w2_bd = (eye[:, None, :, None] * w2[None, :, None, :]).reshape(
        NSA_KV_HEADS * CMP_HIDDEN, NSA_KV_HEADS * HEAD_DIM).astype(BF16)
    return (first_layer(w1[:half]), first_layer(w1[half:]), pos_row(pos[:half]), pos_row(pos[half:]), w2_bd)


def _compress(kc_raw, vc_raw, wk, wv):
    bsz, n_rows, feat = kc_raw.shape
    consts = list(wk) + list(wv[:4]) + [wv[4].T]
    blk = pl.BlockSpec((None, n_rows, feat), lambda b: (b, 0, 0))
    return pl.pallas_call(
        _compress_kernel,
        grid=(bsz,),
        in_specs=[blk, blk] + [_const_spec(c.shape) for c in consts],
        out_specs=[pl.BlockSpec((None, n_rows, LANES), lambda b: (b, 0, 0)),
                   pl.BlockSpec((None, LANES, n_rows), lambda b: (b, 0, 0))],
        out_shape=[jax.ShapeDtypeStruct((bsz, n_rows, LANES), BF16),
                   jax.ShapeDtypeStruct((bsz, LANES, n_rows), BF16)],
        compiler_params=_cparams(("parallel",)),
        name="nsa_compress",
    )(kc_raw, vc_raw, *consts)


def _nsa_kernel(q_ref, kc_ref, vct_ref, ksa_ref, vsa_ref, kw_ref, vwa_ref, gt_ref, ovt_ref, o_ref,
                m_scr, acc_scr, out_scr, q4_scr, q4s_scr, *, seq):
    tq = q_ref.shape[0]
    n_cmp_rows = kc_ref.shape[0]
    n_sel = seq // SEL_LEN
    i_tile = pl.program_id(1)
    t0 = i_tile * tq
    heads = range(NSA_KV_HEADS)

    q = q_ref[...]
    gt_t = gt_ref[...].T
    lane = lax.broadcasted_iota(jnp.int32, (1, LANES), 1)
    t_row = t0 + lax.broadcasted_iota(jnp.int32, (1, tq), 1)

    def tile_lanes(a):
        return jnp.concatenate([a] * NSA_GROUP, axis=1)

    def init_state():
        m_scr[...] = jnp.full_like(m_scr, NEG)
        acc_scr[...] = jnp.zeros_like(acc_scr)

    def online_tile(kts, q_scr, vts, bias):
        scores = [_dot_nt(kts[h], q_scr[h]) for h in heads]
        for h in heads:
            s = scores[h]
            if bias is not None:
                s = s + tile_lanes(bias)
            m_prev = m_scr[h]
            m_next = jnp.maximum(m_prev, jnp.max(s, axis=0, keepdims=True))
            alpha = jnp.exp(m_prev - m_next)
            p = jnp.exp(s - m_next).astype(BF16)
            acc_scr[h] = alpha * acc_scr[h] + _dot(vts[h], p)
            m_scr[h] = m_next

    def finish(h):
        acc = acc_scr[h]
        return acc[0:HEAD_DIM] * (1.0 / acc[HEAD_DIM:HEAD_DIM + 1])

    def gate_row(h, br):
        return jnp.concatenate([gt_t[(h * NSA_GROUP + g) * 3 + br:(h * NSA_GROUP + g) * 3 + br + 1, :]
                                for g in range(NSA_GROUP)], axis=1)

    for h in heads:
        in_head = (lane >> _LOG_HEAD) == h
        q4 = jnp.concatenate(
            [jnp.where(in_head, q[:, g * LANES:(g + 1) * LANES], jnp.zeros((), BF16)) for g in range(NSA_GROUP)],
            axis=0)
        q4_scr[h] = q4

        s = _dot_nt(kc_ref[...], q4)
        n_col = lax.broadcasted_iota(jnp.int32, (n_cmp_rows, 1), 0)
        vis = (n_col * CMP_STRIDE + (CMP_LEN - 1)) <= t_row
        s = s + tile_lanes(jnp.where(vis, 0.0, NEG))
        vis4 = tile_lanes(jnp.where(vis, 1.0, 0.0))
        m = jnp.max(s, axis=0, keepdims=True)
        e = jnp.exp(s - m) * vis4
        p = e * (1.0 / jnp.maximum(jnp.sum(e, axis=0, keepdims=True), 1e-30))
        o_cmp = _dot(vct_ref[...], p.astype(BF16))
        out_scr[h] = gate_row(h, 0) * o_cmp[h * HEAD_DIM:(h + 1) * HEAD_DIM]

        p_sum = p[:, 0:tq]
        for g in range(1, NSA_GROUP):
            p_sum = p_sum + p[:, g * tq:(g + 1) * tq]
        ov_t = ovt_ref[...]
        p_sel_t = sum(_dot(ov_t, part) for part in _split3(p_sum))
        blk = lax.broadcasted_iota(jnp.int32, (n_sel, 1), 0)
        t_blk = t_row >> _LOG_SEL
        valid = blk <= t_blk
        forced = (blk == 0) | (blk == t_blk) | (blk == t_blk - 1)
        rank_t = jnp.where(valid, p_sel_t + FORCE_BONUS * jnp.where(forced, 1.0, 0.0), -FORCE_BONUS)
        n_tiles = n_sel // SUBLANES
        rank_tiles = [rank_t[r * SUBLANES:(r + 1) * SUBLANES] for r in range(n_tiles)]
        cnt_tiles = [jnp.zeros((SUBLANES, tq), F32) for _ in range(n_tiles)]
        blk8 = lax.broadcasted_iota(jnp.int32, (SUBLANES, 1), 0)
        for ii in range(n_sel):
            ri = rank_t[ii:ii + 1, :]
            for r in range(n_tiles):
                if ii < r * SUBLANES:
                    ahead = ri >= rank_tiles[r]
                elif ii >= (r + 1) * SUBLANES:
                    ahead = ri > rank_tiles[r]
                else:
                    tie_ok = jnp.where(blk8 + r * SUBLANES > ii, 1.0, 0.0)
                    ahead = (ri > rank_tiles[r]) | ((ri == rank_tiles[r]) & (tie_ok > 0.5))
                cnt_tiles[r] = cnt_tiles[r] + jnp.where(ahead, 1.0, 0.0)
        top = float(min(SEL_TOPK, n_sel))
        sel_bias = jnp.concatenate(
            [jnp.where(c < top, 0.0, NEG) for c in cnt_tiles] + [jnp.zeros((LANES - n_sel, tq), F32)], axis=0)
        mask_lanes = sel_bias.T
        if h == 0:
            mask_lanes = pltpu.roll(mask_lanes, HEAD_DIM, 1)
        mask_lanes = mask_lanes.astype(BF16)
        q4s_scr[h] = jnp.concatenate(
            [jnp.where(in_head, q[:, g * LANES:(g + 1) * LANES], mask_lanes) for g in range(NSA_GROUP)], axis=0)

    init_state()
    tk = NSA_TK_SEL
    n_full = t0 // tk

    def sel_tile(j, bias):
        k0 = pl.multiple_of(j * tk, tk)
        online_tile([ksa_ref[h, pl.ds(k0, tk), :] for h in heads], q4s_scr,
                    [vsa_ref[h, :, pl.ds(k0, tk)] for h in heads], bias)

    def sel_body(j, _):
        sel_tile(j, None)
        return 0

    lax.fori_loop(0, n_full, sel_body, 0)
    kpos_sel = n_full * tk + lax.broadcasted_iota(jnp.int32, (tk, 1), 0)
    sel_tile(n_full, jnp.where(kpos_sel <= t_row, 0.0, NEG))
    for h in heads:
        out_scr[h] = out_scr[h] + gate_row(h, 1) * finish(h)

    init_state()
    tkw = NSA_TK_WIN
    kpos_win = lax.broadcasted_iota(jnp.int32, (tkw, 1), 0)

    def win_body(j, _):
        k0 = pl.multiple_of(j * tkw, tkw)
        dist = t_row - (k0 + kpos_win)
        bias = jnp.where((dist >= 0) & (dist < WINDOW), 0.0, NEG)
        kt = kw_ref[pl.ds(k0, tkw), :]
        online_tile([kt, kt], q4_scr, [vwa_ref[h, :, pl.ds(k0, tkw)] for h in heads], bias)
        return 0

    lax.fori_loop(jnp.maximum(t0 - (WINDOW - 1), 0) // tkw, (t0 + tq + tkw - 1) // tkw, win_body, 0)
    both = jnp.concatenate([out_scr[h] + gate_row(h, 2) * finish(h) for h in heads], axis=0)
    for g in range(NSA_GROUP):
        o_ref[:, g * LANES:(g + 1) * LANES] = both[:, g * tq:(g + 1) * tq].T.astype(o_ref.dtype)


def _nsa(q3, kc, vc_t, ks_aug, vs_aug, kw, vw_aug, gt, ov_t):
    bsz, seq, qw = q3.shape
    tq = NSA_TQ
    n_rows = kc.shape[1]
    nh = NSA_KV_HEADS
    full = pl.BlockSpec((None, seq, LANES), lambda b, i: (b, 0, 0))
    head_tok = pl.BlockSpec((None, nh, seq, LANES), lambda b, i: (b, 0, 0, 0))
    head_feat = pl.BlockSpec((None, nh, LANES, seq), lambda b, i: (b, 0, 0, 0))
    tile = lambda w: pl.BlockSpec((None, tq, w), lambda b, i: (b, i, 0))
    cols = NSA_GROUP * tq
    return pl.pallas_call(
        functools.partial(_nsa_kernel, seq=seq),
        grid=(bsz, seq // tq),
        in_specs=[tile(qw),
                  pl.BlockSpec((None, n_rows, LANES), lambda b, i: (b, 0, 0)),
                  pl.BlockSpec((None, LANES, n_rows), lambda b, i: (b, 0, 0)),
                  head_tok, head_feat, full, head_feat, tile(LANES), _const_spec(ov_t.shape)],
        out_specs=tile(qw),
        out_shape=jax.ShapeDtypeStruct((bsz, seq, qw), BF16),
        scratch_shapes=[pltpu.VMEM((nh, 1, cols), F32),
                        pltpu.VMEM((nh, LANES, cols), F32),
                        pltpu.VMEM((nh, HEAD_DIM, cols), F32),
                        pltpu.VMEM((nh, cols, LANES), BF16),
                        pltpu.VMEM((nh, cols, LANES), BF16)],
        compiler_params=_cparams(("parallel", "arbitrary")),
        name="nsa_attention",
    )(q3, kc, vc_t, ks_aug, vs_aug, kw, vw_aug, gt, ov_t)


def _sb_kernel(q_ref, k_ref, vt_ref, o_ref, r_scr, acc_scr):
    tq = q_ref.shape[0]
    tk = 2 * tq
    i_tile = pl.program_id(1)
    t0 = i_tile * tq
    q = q_ref[...]
    lane = lax.broadcasted_iota(jnp.int32, (1, LANES), 1)
    feat = lax.broadcasted_iota(jnp.int32, (LANES, 1), 0)
    t_row = t0 + lax.broadcasted_iota(jnp.int32, (1, tq), 1)
    kpos_col = lax.broadcasted_iota(jnp.int32, (tk, 1), 0)
    sr = lax.broadcasted_iota(jnp.int32, (tk, tk), 0)
    sc = lax.broadcasted_iota(jnp.int32, (tk, tk), 1)
    later = jnp.where(sc > sr, 1.0, 0.0).astype(BF16)
    qm = []
    for h in range(SB_HEADS):
        in_head = (lane >> _LOG_HEAD) == (h % 2)
        pair = q[:, (h // 2) * LANES:(h // 2 + 1) * LANES]
        qm.append(jnp.where(in_head, pair, jnp.zeros((), BF16)))
    r_scr[...] = jnp.zeros_like(r_scr)
    acc_scr[...] = jnp.zeros_like(acc_scr)

    def cond(c):
        j, r_max = c
        return (j >= 0) & (r_max > SB_EXIT)

    def body(c):
        j, _ = c
        k0 = pl.multiple_of(jnp.maximum(j - 1, 0) * tq, tq)
        kt = k_ref[pl.ds(k0, tk), :]
        z = jnp.concatenate([_dot_nt(kt[:, (h // 2) * LANES:(h // 2 + 1) * LANES], qm[h])
                             for h in range(SB_HEADS)], axis=1)
        kpos = k0 + kpos_col
        causal = jnp.where((kpos < t_row) & (kpos < (j + 1) * tq), 1.0, 0.0)
        causal = jnp.concatenate([causal] * SB_HEADS, axis=1)
        ls = jnp.minimum(z, 0.0) - jnp.log(1.0 + jnp.exp(-jnp.abs(z)))
        rest = (ls - z) * causal
        hi = rest.astype(BF16)
        lo = (rest - hi.astype(F32)).astype(BF16)
        r_prev = r_scr[...]
        after = _dot(later, hi) + _dot(later, lo) + r_prev
        w = (jnp.exp(ls + after) * causal).astype(BF16)
        for h in range(SB_HEADS):
            cols = slice(h * tq, (h + 1) * tq)
            vt = vt_ref[(h // 2) * LANES:(h // 2 + 1) * LANES, pl.ds(k0, tk)]
            acc_scr[:, cols] = acc_scr[:, cols] + _dot(vt, w[:, cols])
        r_new = r_prev + jnp.sum(rest, axis=0, keepdims=True)
        r_scr[...] = r_new
        return j - 2, jnp.max(r_new)

    lax.while_loop(cond, body, (i_tile, jnp.zeros((), F32)))
    for hp in range(SB_HEADS // 2):
        even = acc_scr[:, (2 * hp) * tq:(2 * hp + 1) * tq]
        odd = acc_scr[:, (2 * hp + 1) * tq:(2 * hp + 2) * tq]
        o_ref[:, hp * LANES:(hp + 1) * LANES] = jnp.where(feat < HEAD_DIM, even, odd).T.astype(o_ref.dtype)


def _sb(q3, k3, v_t):
    bsz, seq, w = q3.shape
    tq = SB_TQ
    full = pl.BlockSpec((None, seq, w), lambda b, i: (b, 0, 0))
    full_t = pl.BlockSpec((None, w, seq), lambda b, i: (b, 0, 0))
    tile = pl.BlockSpec((None, tq, w), lambda b, i: (b, i, 0))
    return pl.pallas_call(
        _sb_kernel,
        grid=(bsz, seq // tq),
        in_specs=[tile, full, full_t],
        out_specs=tile,
        out_shape=jax.ShapeDtypeStruct((bsz, seq, w), BF16),
        scratch_shapes=[pltpu.VMEM((1, SB_HEADS * tq), F32), pltpu.VMEM((LANES, SB_HEADS * tq), F32)],
        compiler_params=_cparams(("parallel", "arbitrary")),
        name="sb_attention",
    )(q3, k3, v_t)


def _merge_kernel(x_ref, g_ref, a_ref, b_ref, c_ref, wg_ref, wb_ref, wo_ref, o_ref):
    x = x_ref[...]
    d = x.shape[1]
    h = _rms(x, g_ref[...]).astype(BF16)
    mixed = None
    for n, br in enumerate((a_ref, b_ref, c_ref)):
        gate = jax.nn.sigmoid(_dot(h, wg_ref[:, n * d:(n + 1) * d]))
        y = gate * _dot(br[...], wb_ref[n])
        mixed = y if mixed is None else mixed + y
    o_ref[...] = x + _dot(mixed.astype(BF16), wo_ref[...])


def _merge(x2, g, o_a, o_b, o_c, w_gate, w_branch, w_out):
    t, d = x2.shape
    tm = min(TM_PROJ, t)
    row = lambda i: (i, 0)
    bw = o_a.shape[1]
    return pl.pallas_call(
        _merge_kernel,
        grid=(t // tm,),
        in_specs=[pl.BlockSpec((tm, d), row), _const_spec((1, d)),
                  pl.BlockSpec((tm, bw), row), pl.BlockSpec((tm, bw), row), pl.BlockSpec((tm, bw), row),
                  _const_spec(w_gate.shape), _const_spec(w_branch.shape), _const_spec(w_out.shape)],
        out_specs=pl.BlockSpec((tm, d), row),
        out_shape=jax.ShapeDtypeStruct((t, d), F32),
        compiler_params=_cparams(("parallel",)),
        name="gated_merge",
    )(x2, g, o_a, o_b, o_c, w_gate, w_branch, w_out)


def _ffn_kernel(x_ref, halo_ref, g_ref, wup_ref, cw_ref, cb_ref, wdn_ref, o_ref, act_scr, *, tiles_per_seq):
    x = x_ref[...]
    tm = x.shape[0]
    d_ff = wdn_ref.shape[0]
    at_start = (pl.program_id(0) % tiles_per_seq) == 0
    g = g_ref[...]
    h_halo = jnp.where(at_start, 0.0, _rms(halo_ref[...], g))
    h = jnp.concatenate([h_halo, _rms(x, g)], axis=0).astype(BF16)
    for f in range(d_ff // FF_TILE):
        halves = []
        for base in (0, d_ff):
            cols = slice(base + f * FF_TILE, base + (f + 1) * FF_TILE)
            u = _dot(h, wup_ref[:, cols])
            w = cw_ref[:, cols]
            conv = (w[2:3] * u[SUBLANES:]
                    + w[1:2] * pltpu.roll(u, 1, 0)[SUBLANES:]
                    + w[0:1] * pltpu.roll(u, 2, 0)[SUBLANES:]
                    + cb_ref[:, cols])
            halves.append(conv)
        act_scr[:, f * FF_TILE:(f + 1) * FF_TILE] = (jax.nn.silu(halves[0]) * halves[1]).astype(BF16)
    o_ref[...] = x + _dot(act_scr[...], wdn_ref[...])


def _ffn(x2, g, w_up, conv_w, conv_b, w_down, seq):
    t, d = x2.shape
    tm = min(TM_FFN, seq)
    tiles_per_seq = seq // tm
    row = lambda i: (i, 0)
    halo = lambda i: (jnp.maximum(i * (tm // SUBLANES) - 1, 0), 0)
    return pl.pallas_call(
        functools.partial(_ffn_kernel, tiles_per_seq=tiles_per_seq),
        grid=(t // tm,),
        in_specs=[pl.BlockSpec((tm, d), row), pl.BlockSpec((SUBLANES, d), halo), _const_spec((1, d)),
                  _const_spec(w_up.shape), _const_spec(conv_w.shape), _const_spec(conv_b.shape),
                  _const_spec(w_down.shape)],
        out_specs=pl.BlockSpec((tm, d), row),
        out_shape=jax.ShapeDtypeStruct((t, d), F32),
        scratch_shapes=[pltpu.VMEM((tm, w_down.shape[0]), BF16)],
        compiler_params=_cparams(("parallel",)),
        name="conv_ffn",
    )(x2, x2, g, w_up, conv_w, conv_b, w_down)


def _norm_kernel(x_ref, g_ref, o_ref):
    o_ref[...] = _rms(x_ref[...], g_ref[...])


def _final_norm(x2, g):
    t, d = x2.shape
    tm = min(TM_PROJ, t)
    return pl.pallas_call(
        _norm_kernel,
        grid=(t // tm,),
        in_specs=[pl.BlockSpec((tm, d), lambda i: (i, 0)), _const_spec((1, d))],
        out_specs=pl.BlockSpec((tm, d), lambda i: (i, 0)),
        out_shape=jax.ShapeDtypeStruct((t, d), F32),
        compiler_params=_cparams(("parallel",)),
        name="final_norm",
    )(x2, g)


def _nsa_slot_order():
    return [NSA_GROUP * (s % 2) + s // 2 for s in range(NSA_HEADS)]


def _pack_w_in(w):
    d = w.shape[0]
    o_q, o_kv, o_g, o_sb, o_gm = 512, 1024, 1792, 1816, 3352
    u = w[:, 0:o_q]
    q = w[:, o_q:o_kv].reshape(d, NSA_HEADS, HEAD_DIM)[:, jnp.array(_nsa_slot_order())].reshape(d, -1)
    kv = w[:, o_kv:o_g].reshape(d, 3, 2, NSA_KV_HEADS * HEAD_DIM)
    k = kv[:, :, 0].reshape(d, -1)
    v = kv[:, :, 1]
    gates = jnp.pad(w[:, o_g:o_sb], ((0, 0), (0, LANES - (o_sb - o_g))))
    sb = w[:, o_sb:o_gm]
    sb_w = SB_HEADS * HEAD_DIM
    packed = jnp.concatenate([u, q, k, v[:, 0], gates, sb[:, :2 * sb_w]], axis=1).astype(BF16)
    w_vt = jnp.concatenate([v[:, 1], v[:, 2], sb[:, 2 * sb_w:]], axis=1).T.astype(BF16)
    return packed, w_vt, w[:, o_gm:].astype(BF16)


def _rope_tables(seq):
    inv_freq = 1.0 / (ROPE_THETA ** (jnp.arange(0, HEAD_DIM, 2, dtype=F32) / HEAD_DIM))
    ang = jnp.arange(seq, dtype=F32)[:, None] * inv_freq[None, :]
    cos, sin = jnp.cos(ang), jnp.sin(ang)
    cos_t = jnp.tile(cos, (1, LANES // (HEAD_DIM // 2)))
    sin_t = jnp.tile(jnp.concatenate([-sin, sin], axis=1), (1, LANES // HEAD_DIM))
    return cos_t, sin_t


def _overlap_t(seq):
    n_rows = seq // CMP_STRIDE
    n_sel = seq // SEL_LEN
    cmp_start = jnp.arange(n_rows) * CMP_STRIDE
    sel_start = jnp.arange(n_sel) * SEL_LEN
    ov = ((cmp_start[None, :] < sel_start[:, None] + SEL_LEN)
          & (cmp_start[None, :] + CMP_LEN > sel_start[:, None])
          & (jnp.arange(n_rows)[None, :] < (seq - CMP_LEN) // CMP_STRIDE + 1))
    return ov.astype(BF16)


def kernel(x, norm_mix, w_in, ssm_lam_re, ssm_lam_im, ssm_log_step, ssm_b_re, ssm_b_im, ssm_c_re, ssm_c_im, ssm_d, ssm_w_glu, cmp_w1_k, cmp_w2_k, cmp_pos_k, cmp_w1_v, cmp_w2_v, cmp_pos_v, w_branch, w_out, norm_ffn, ffn_w_up, ffn_conv_w, ffn_conv_b, ffn_w_down, norm_final):
    bsz, seq, d = x.shape
    depth = w_in.shape[0]
    t = bsz * seq
    assert seq % 512 == 0 and seq // SEL_LEN <= HEAD_DIM
    cos_t, sin_t = _rope_tables(seq)
    ov_t = _overlap_t(seq)
    slot_rows = jnp.array(_nsa_slot_order())
    x2 = x.reshape(t, d).astype(F32)
    for l in range(depth):
        w_packed, w_vt, w_gate = _pack_w_in(w_in[l])
        (u, q, kc_raw, ks_aug, kw, vc_raw, vs_aug, vw_aug, gt, sq, sk, sv_t) = _proj_in(
            x2, norm_mix[l].reshape(1, d), w_packed, w_vt, cos_t, sin_t, bsz, seq)
        b3 = lambda a: a.reshape(bsz, seq, a.shape[-1])
        tables = _s5_tables(ssm_lam_re[l], ssm_lam_im[l], ssm_log_step[l], ssm_b_re[l], ssm_b_im[l],
                            ssm_c_re[l], ssm_c_im[l])
        o_ssm = _s5(b3(u), tables, ssm_d[l].reshape(1, -1).astype(F32), ssm_w_glu[l].astype(BF16))
        rows16 = lambda a: a.reshape(bsz, seq // CMP_STRIDE, CMP_STRIDE * a.shape[-1])
        kc, vc_t = _compress(rows16(kc_raw), rows16(vc_raw),
                             _compress_weights(cmp_w1_k[l], cmp_w2_k[l], cmp_pos_k[l]),
                             _compress_weights(cmp_w1_v[l], cmp_w2_v[l], cmp_pos_v[l]))
        o_nsa = _nsa(b3(q), kc, vc_t, ks_aug, vs_aug, b3(kw), vw_aug, b3(gt), ov_t)
        o_sb = _sb(b3(sq), b3(sk), sv_t)
        wb = w_branch[l]
        wb_nsa = wb[1].reshape(NSA_HEADS, HEAD_DIM, d)[slot_rows].reshape(BRANCH_WIDTH, d)
        wb_all = jnp.stack([wb[0], wb_nsa, wb[2]]).astype(BF16)
        x2 = _merge(x2, norm_mix[l].reshape(1, d), o_ssm.reshape(t, -1), o_nsa.reshape(t, -1),
                    o_sb.reshape(t, -1), w_gate, wb_all, w_out[l].astype(BF16))
        x2 = _ffn(x2, norm_ffn[l].reshape(1, d), ffn_w_up[l].astype(BF16), ffn_conv_w[l].astype(F32),
                  ffn_conv_b[l].reshape(1, -1).astype(F32), ffn_w_down[l].astype(BF16), seq)
    return _final_norm(x2, norm_final.reshape(1, d)).reshape(bsz, seq, d).astype(x.dtype)
```

```python
import functools
import math

import jax
import jax.numpy as jnp
from jax import lax
from jax.experimental import pallas as pl
from jax.experimental.pallas import tpu as pltpu

F32 = jnp.float32
BF16 = jnp.bfloat16

HEAD_DIM = 64
RMS_EPS = 1e-6
ROPE_THETA = 10000.0
SSM_GROUP = 16
SSM_STATE = 64
NSA_HEADS = 8
NSA_KV_HEADS = 2
NSA_GROUP = NSA_HEADS // NSA_KV_HEADS
CMP_LEN = 32
CMP_STRIDE = 16
CMP_HIDDEN = 128
SEL_LEN = 64
SEL_TOPK = 16
WINDOW = 512
FORCE_BONUS = 1e6
SB_HEADS = 8
CONV_WIDTH = 3
BRANCH_WIDTH = 512

LANES = 128
SUBLANES = 8
VMEM_LIMIT = 56 * 1024 * 1024

TM_PROJ = 512
TM_FFN = 512
FF_TILE = 256
S5_CHUNK = 128
S5_LANES = 256
NSA_TQ = 256
NSA_TK_SEL = 512
NSA_TK_WIN = 256
SB_TQ = 128
SB_TILES = 3

_LOG_HEAD = HEAD_DIM.bit_length() - 1
_LOG_SEL = SEL_LEN.bit_length() - 1
NEG = -1e30
SB_EXIT = -104.0


def _cparams(sem):
    return pltpu.CompilerParams(dimension_semantics=sem, vmem_limit_bytes=VMEM_LIMIT)


def _const_spec(shape):
    n = len(shape)
    return pl.BlockSpec(shape, lambda *_: (0,) * n, pipeline_mode=pl.Buffered(1))


def _layer_spec(stacked, layer):
    n = stacked.ndim - 1
    return pl.BlockSpec((None,) + stacked.shape[1:], lambda *_: (layer,) + (0,) * n,
                        pipeline_mode=pl.Buffered(1))


def _rms(x, g):
    return x * lax.rsqrt(jnp.mean(x * x, axis=-1, keepdims=True) + RMS_EPS) * g


def _dot(a, b):
    return jnp.dot(a, b, preferred_element_type=F32)


def _dot_nt(a, b):
    return lax.dot_general(a, b, (((1,), (1,)), ((), ())), preferred_element_type=F32)


def _split3(x):
    h = x.astype(BF16)
    r = x - h.astype(F32)
    m = r.astype(BF16)
    lo = (r - m.astype(F32)).astype(BF16)
    return h, m, lo


_C_U = 0
_C_ROPE = 512
_C_V = 1408
_C_G = 1536
_C_SB = 1664
_C_END = 2688
_R_VS, _R_VW, _R_SV, _R_END = 0, 128, 256, 768


def _proj_kernel(x_ref, g_ref, w_ref, wvt_ref, cos_ref, sin_ref,
                 u_ref, q_ref, kc_ref, ksa_ref, kw_ref, vc_ref, vsa_ref, vwa_ref, gt_ref,
                 sq_ref, sk_ref, svt_ref, rows_scr, *, n_pos):
    tm = x_ref.shape[0]
    h = _rms(x_ref[...], g_ref[...]).astype(BF16)

    def store_rows16(val, out_ref):
        rows_scr[...] = val
        for b in range(CMP_STRIDE):
            out_ref[:, b * LANES:(b + 1) * LANES] = rows_scr[pl.ds(b, tm // CMP_STRIDE, stride=CMP_STRIDE), :]

    def mm(a, b):
        return _dot(h, w_ref[:, a:b])

    vt = _dot_nt(wvt_ref[...], h)
    ones_rows = jnp.where(lax.broadcasted_iota(jnp.int32, (HEAD_DIM, tm), 0) == 0, 1.0, 0.0)
    for hh in range(NSA_KV_HEADS):
        vsa_ref[hh] = jnp.concatenate(
            [vt[_R_VS + hh * HEAD_DIM:_R_VS + (hh + 1) * HEAD_DIM], ones_rows], axis=0).astype(BF16)
        vwa_ref[hh] = jnp.concatenate(
            [vt[_R_VW + hh * HEAD_DIM:_R_VW + (hh + 1) * HEAD_DIM], ones_rows], axis=0).astype(BF16)
    svt_ref[...] = vt[_R_SV:_R_END].astype(BF16)

    u_ref[...] = mm(_C_U, _C_ROPE)

    r = mm(_C_ROPE, _C_V)
    cos = cos_ref[...]
    sin = sin_ref[...]
    lane = lax.broadcasted_iota(jnp.int32, cos.shape, 1)
    first = (lane & (HEAD_DIM - 1)) < (HEAD_DIM // 2)
    roped = []
    for c in range((_C_V - _C_ROPE) // LANES):
        rc = r[:, c * LANES:(c + 1) * LANES]
        partner = jnp.where(first, pltpu.roll(rc, LANES - HEAD_DIM // 2, 1),
                            pltpu.roll(rc, HEAD_DIM // 2, 1))
        roped.append(rc * cos + partner * sin)
    scale = HEAD_DIM ** -0.5
    for c in range(4):
        q_ref[:, c * LANES:(c + 1) * LANES] = (roped[c] * scale).astype(BF16)
    store_rows16(roped[4], kc_ref)
    kw_ref[...] = roped[6].astype(BF16)
    pos = (pl.program_id(0) % n_pos) * tm + lax.broadcasted_iota(jnp.int32, (tm, 1), 0)
    blk = pos >> _LOG_SEL
    lane_k = lax.broadcasted_iota(jnp.int32, (1, LANES), 1)
    for hh in range(NSA_KV_HEADS):
        own = (lane_k >> _LOG_HEAD) == hh
        one_hot = jnp.where(blk == (lane_k & (HEAD_DIM - 1)), 1.0, 0.0)
        ksa_ref[hh] = jnp.where(own, roped[5], one_hot).astype(BF16)

    store_rows16(mm(_C_V, _C_G), vc_ref)
    gt_ref[...] = jax.nn.sigmoid(mm(_C_G, _C_SB))

    sb = mm(_C_SB, _C_END)
    sq_ref[...] = (sb[:, 0:512] * scale).astype(BF16)
    sk_ref[...] = sb[:, 512:1024].astype(BF16)


def _proj_in(x2, layer, g, w_packed, w_vt, cos_t, sin_t, bsz, seq):
    t, d = x2.shape
    tm = min(TM_PROJ, seq)
    n_pos = seq // tm
    row = lambda i: (i, 0)
    pos = lambda i: (i % n_pos, 0)
    nh = NSA_KV_HEADS

    def tok(w, dt):
        return pl.BlockSpec((tm, w), row), jax.ShapeDtypeStruct((t, w), dt)

    def feat_major(w, dt):
        return (pl.BlockSpec((None, w, tm), lambda i: (i // n_pos, 0, i % n_pos)),
                jax.ShapeDtypeStruct((bsz, w, seq), dt))

    def head_tok(dt):
        return (pl.BlockSpec((None, nh, tm, LANES), lambda i: (i // n_pos, 0, i % n_pos, 0)),
                jax.ShapeDtypeStruct((bsz, nh, seq, LANES), dt))

    def head_feat(dt):
        return (pl.BlockSpec((None, nh, LANES, tm), lambda i: (i // n_pos, 0, 0, i % n_pos)),
                jax.ShapeDtypeStruct((bsz, nh, LANES, seq), dt))

    def rows16():
        return (pl.BlockSpec((tm // CMP_STRIDE, CMP_STRIDE * LANES), row),
                jax.ShapeDtypeStruct((t // CMP_STRIDE, CMP_STRIDE * LANES), F32))

    outs = [tok(512, F32), tok(512, BF16), rows16(), head_tok(BF16), tok(128, BF16),
            rows16(), head_feat(BF16), head_feat(BF16), tok(128, F32),
            tok(512, BF16), tok(512, BF16), feat_major(512, BF16)]
    return pl.pallas_call(
        functools.partial(_proj_kernel, n_pos=n_pos),
        grid=(t // tm,),
        in_specs=[pl.BlockSpec((tm, d), row),
                  _layer_spec(g, layer),
                  _layer_spec(w_packed, layer),
                  _layer_spec(w_vt, layer),
                  pl.BlockSpec((tm, LANES), pos),
                  pl.BlockSpec((tm, LANES), pos)],
        out_specs=[spec for spec, _ in outs],
        out_shape=[shape for _, shape in outs],
        scratch_shapes=[pltpu.VMEM((tm, LANES), F32)],
        compiler_params=_cparams(("parallel",)),
        name="proj_in",
    )(x2, g, w_packed, w_vt, cos_t, sin_t)


def _s5_kernel(u_ref, bcat_ref, ccat_ref, pw_re_ref, pw_im_ref, cy_re_ref, cy_im_ref,
               d_ref, wglu_ref, o_ref, car_re, car_im):
    @pl.when(pl.program_id(1) == 0)
    def _():
        car_re[...] = jnp.zeros_like(car_re)
        car_im[...] = jnp.zeros_like(car_im)

    u = u_ref[...]
    ub = u.astype(BF16)
    chunk = u.shape[0]
    n_steps = pw_re_ref.shape[1]
    n_tiles = bcat_ref.shape[0]
    n_sub = chunk // SUBLANES
    y_blocks = [None] * (n_tiles // 2)
    for j in range(n_tiles):
        blk = j // 2
        x = _dot(ub[:, blk * LANES:(blk + 1) * LANES], bcat_ref[j])
        hr = x[:, :S5_LANES].reshape(n_sub, SUBLANES, S5_LANES)
        hi = x[:, S5_LANES:].reshape(n_sub, SUBLANES, S5_LANES)
        for k in range(n_steps):
            sr = pltpu.roll(hr, 1 << k, 1)
            si = pltpu.roll(hi, 1 << k, 1)
            cr = pw_re_ref[j, k]
            ci = pw_im_ref[j, k]
            hr, hi = hr + cr * sr - ci * si, hi + cr * si + ci * sr
        pr = cy_re_ref[j]
        pi = cy_im_ref[j]
        c_re = car_re[j:j + 1, :]
        c_im = car_im[j:j + 1, :]
        tiles_re, tiles_im = [], []
        for v in range(n_sub):
            t_re = hr[v] + pr * c_re - pi * c_im
            t_im = hi[v] + pr * c_im + pi * c_re
            c_re = t_re[SUBLANES - 1:SUBLANES, :]
            c_im = t_im[SUBLANES - 1:SUBLANES, :]
            tiles_re.append(t_re)
            tiles_im.append(t_im)
        car_re[j:j + 1, :] = c_re
        car_im[j:j + 1, :] = c_im
        hcat = jnp.concatenate([jnp.concatenate(tiles_re, axis=0), jnp.concatenate(tiles_im, axis=0)],
                               axis=1).astype(BF16)
        yj = _dot(hcat, ccat_ref[j])
        y_blocks[blk] = yj if y_blocks[blk] is None else y_blocks[blk] + yj
    y = jnp.concatenate(y_blocks, axis=1) + d_ref[...] * u
    z = jax.nn.gelu(y)
    o_ref[...] = (z * jax.nn.sigmoid(_dot(z.astype(BF16), wglu_ref[...]))).astype(o_ref.dtype)


def _s5_tables(lam_re, lam_im, log_step, b_re, b_im, c_re, c_im):
    g_n, p_n = lam_re.shape
    i_n = b_re.shape[-1]
    width = g_n * i_n
    lanes_total = g_n * p_n
    n_tiles = lanes_total // S5_LANES
    step = jnp.exp(log_step.astype(F32))[:, None]
    lr = lam_re.astype(F32)
    li = lam_im.astype(F32)
    mag = jnp.exp(lr * step)
    a_re = mag * jnp.cos(li * step)
    a_im = mag * jnp.sin(li * step)
    den = lr * lr + li * li
    nr = a_re - 1.0
    f_re = (nr * lr + a_im * li) / den
    f_im = (a_im * lr - nr * li) / den
    br = b_re.astype(F32)
    bi = b_im.astype(F32)
    bb_re = f_re[..., None] * br - f_im[..., None] * bi
    bb_im = f_re[..., None] * bi + f_im[..., None] * br
    eye = jnp.eye(g_n, dtype=F32)

    def in_map(bb):
        return (eye[:, None, :, None] * bb.transpose(0, 2, 1)[:, :, None, :]).reshape(width, lanes_total)

    def out_map(c):
        return (eye[:, None, :, None] * c.transpose(0, 2, 1)[:, :, None, :]).reshape(lanes_total, width)

    bd_re, bd_im = in_map(bb_re), in_map(bb_im)
    cd_re, cd_im = out_map(c_re.astype(F32)), out_map(c_im.astype(F32))
    bcat, ccat = [], []
    for j in range(n_tiles):
        rows = slice((j // 2) * LANES, (j // 2 + 1) * LANES)
        cols = slice(j * S5_LANES, (j + 1) * S5_LANES)
        bcat.append(jnp.concatenate([bd_re[rows, cols], bd_im[rows, cols]], axis=1))
        ccat.append(jnp.concatenate([cd_re[cols, rows], -cd_im[cols, rows]], axis=0))
    bcat = jnp.stack(bcat).astype(BF16)
    ccat = jnp.stack(ccat).astype(BF16)

    log_mag = (lr * step).reshape(1, lanes_total)
    ang = (li * step).reshape(1, lanes_total)

    def powers(n):
        m = jnp.exp(n * log_mag)
        return m * jnp.cos(n * ang), m * jnp.sin(n * ang)

    n_steps = SUBLANES.bit_length() - 1
    shifts = 2 ** jnp.arange(n_steps)
    pw_re, pw_im = powers(shifts.astype(F32)[:, None])
    cy_re, cy_im = powers(jnp.arange(1, SUBLANES + 1, dtype=F32)[:, None])

    def tiles(a):
        return a.reshape(a.shape[0], n_tiles, S5_LANES).transpose(1, 0, 2)

    live = (jnp.arange(SUBLANES)[None, :] >= shifts[:, None]).astype(F32)[None, :, :, None]
    pw_re = tiles(pw_re)[:, :, None, :] * live
    pw_im = tiles(pw_im)[:, :, None, :] * live
    return bcat, ccat, pw_re, pw_im, tiles(cy_re), tiles(cy_im)


def _s5(u3, layer, tables, d_row, wglu):
    bsz, seq, width = u3.shape
    bcat, ccat, pw_re, pw_im, cy_re, cy_im = tables
    chunk = min(S5_CHUNK, seq)
    n_tiles = bcat.shape[1]
    return pl.pallas_call(
        _s5_kernel,
        grid=(bsz, seq // chunk),
        in_specs=[pl.BlockSpec((None, chunk, width), lambda b, c: (b, c, 0))]
                 + [_layer_spec(a, layer) for a in (bcat, ccat, pw_re, pw_im, cy_re, cy_im, d_row, wglu)],
        out_specs=pl.BlockSpec((None, chunk, width), lambda b, c: (b, c, 0)),
        out_shape=jax.ShapeDtypeStruct((bsz, seq, width), BF16),
        scratch_shapes=[pltpu.VMEM((n_tiles, S5_LANES), F32), pltpu.VMEM((n_tiles, S5_LANES), F32)],
        compiler_params=_cparams(("parallel", "arbitrary")),
        name="s5_scan",
    )(u3, bcat, ccat, pw_re, pw_im, cy_re, cy_im, d_row, wglu)


def _compress_kernel(k_ref, v_ref, w1a_k, w1b_k, pa_k, pb_k, w2_k, w1a_v, w1b_v, pa_v, pb_v, w2_v,
                     kc_ref, vc_ref):
    def hidden(t_ref, w1a, w1b, pa, pb):
        r = t_ref[...]
        n = r.shape[0]
        first = _dot((r + pa[...]).astype(BF16), w1a[...])
        second = _dot((r + pb[...]).astype(BF16), w1b[...])
        rowi = lax.broadcasted_iota(jnp.int32, second.shape, 0)
        nxt = jnp.where(rowi < n - 1, pltpu.roll(second, n - 1, 0), 0.0)
        return jax.nn.gelu(first + nxt).astype(BF16)

    kc_ref[...] = _dot(hidden(k_ref, w1a_k, w1b_k, pa_k, pb_k), w2_k[...]).astype(kc_ref.dtype)
    vc_ref[...] = _dot_nt(w2_v[...], hidden(v_ref, w1a_v, w1b_v, pa_v, pb_v)).astype(vc_ref.dtype)


def _compress_weights(w1, w2, pos, feature_major_out=False):
    half = CMP_LEN // 2
    eye = jnp.eye(NSA_KV_HEADS, dtype=F32)

    def first_layer(w):
        m = w[:, None, :, None, :] * eye[None, :, None, :, None]
        return m.reshape(half * NSA_KV_HEADS * HEAD_DIM, NSA_KV_HEADS * CMP_HIDDEN).astype(BF16)

    def pos_row(p):
        return jnp.broadcast_to(p[:, None, :], (half, NSA_KV_HEADS, HEAD_DIM)).reshape(1, -1).astype(F32)

    w2_bd = (eye[:, None, :, None] * w2[None, :, None, :]).reshape(
        NSA_KV_HEADS * CMP_HIDDEN, NSA_KV_HEADS * HEAD_DIM).astype(BF16)
    if feature_major_out:
        w2_bd = w2_bd.T
    return (first_layer(w1[:half]), first_layer(w1[half:]), pos_row(pos[:half]), pos_row(pos[half:]), w2_bd)


def _compress(kc_raw, vc_raw, layer, wk, wv):
    bsz, n_rows, feat = kc_raw.shape
    consts = list(wk) + list(wv)
    blk = pl.BlockSpec((None, n_rows, feat), lambda b: (b, 0, 0))
    return pl.pallas_call(
        _compress_kernel,
        grid=(bsz,),
        in_specs=[blk, blk] + [_layer_spec(c, layer) for c in consts],
        out_specs=[pl.BlockSpec((None, n_rows, LANES), lambda b: (b, 0, 0)),
                   pl.BlockSpec((None, LANES, n_rows), lambda b: (b, 0, 0))],
        out_shape=[jax.ShapeDtypeStruct((bsz, n_rows, LANES), BF16),
                   jax.ShapeDtypeStruct((bsz, LANES, n_rows), BF16)],
        compiler_params=_cparams(("parallel",)),
        name="nsa_compress",
    )(kc_raw, vc_raw, *consts)


def _nsa_kernel(q_ref, kc_ref, vct_ref, ksa_ref, vsa_ref, kw_ref, vwa_ref, gt_ref, ovt_ref, o_ref,
                m_scr, acc_scr, out_scr, q4_scr, q4s_scr, *, seq):
    tq = q_ref.shape[0]
    n_cmp_rows = kc_ref.shape[0]
    n_sel = seq // SEL_LEN
    i_tile = pl.program_id(1)
    t0 = i_tile * tq
    heads = range(NSA_KV_HEADS)

    q = q_ref[...]
    gt_t = gt_ref[...].T
    lane = lax.broadcasted_iota(jnp.int32, (1, LANES), 1)
    t_row = t0 + lax.broadcasted_iota(jnp.int32, (1, tq), 1)

    def tile_lanes(a):
        return jnp.concatenate([a] * NSA_GROUP, axis=1)

    def init_state():
        m_scr[...] = jnp.full_like(m_scr, NEG)
        acc_scr[...] = jnp.zeros_like(acc_scr)

    def online_tile(kts, q_scr, vts, bias):
        scores = [_dot_nt(kts[h], q_scr[h]) for h in heads]
        for h in heads:
            s = scores[h]
            if bias is not None:
                s = s + tile_lanes(bias)
            m_prev = m_scr[h]
            m_next = jnp.maximum(m_prev, jnp.max(s, axis=0, keepdims=True))
            alpha = jnp.exp(m_prev - m_next)
            p = jnp.exp(s - m_next).astype(BF16)
            acc_scr[h] = alpha * acc_scr[h] + _dot(vts[h], p)
            m_scr[h] = m_next

    def finish(h):
        acc = acc_scr[h]
        return acc[0:HEAD_DIM] * (1.0 / acc[HEAD_DIM:HEAD_DIM + 1])

    def gate_row(h, br):
        return jnp.concatenate([gt_t[(h * NSA_GROUP + g) * 3 + br:(h * NSA_GROUP + g) * 3 + br + 1, :]
                                for g in range(NSA_GROUP)], axis=1)

    for h in heads:
        in_head = (lane >> _LOG_HEAD) == h
        q4 = jnp.concatenate(
            [jnp.where(in_head, q[:, g * LANES:(g + 1) * LANES], jnp.zeros((), BF16)) for g in range(NSA_GROUP)],
            axis=0)
        q4_scr[h] = q4

        s = _dot_nt(kc_ref[...], q4)
        n_col = lax.broadcasted_iota(jnp.int32, (n_cmp_rows, 1), 0)
        vis = (n_col * CMP_STRIDE + (CMP_LEN - 1)) <= t_row
        s = s + tile_lanes(jnp.where(vis, 0.0, NEG))
        m = jnp.max(s, axis=0, keepdims=True)
        e = jnp.exp(s - m)
        any_vis = tile_lanes(jnp.where(t_row >= CMP_LEN - 1, 1.0, 0.0))
        p = e * (any_vis / jnp.maximum(jnp.sum(e, axis=0, keepdims=True), 1e-30))
        o_cmp = _dot(vct_ref[...], p.astype(BF16))
        out_scr[h] = gate_row(h, 0) * o_cmp[h * HEAD_DIM:(h + 1) * HEAD_DIM]

        p_sum = p[:, 0:tq]
        for g in range(1, NSA_GROUP):
            p_sum = p_sum + p[:, g * tq:(g + 1) * tq]
        ov_t = ovt_ref[...]
        p_sel_t = sum(_dot(ov_t, part) for part in _split3(p_sum))
        blk = lax.broadcasted_iota(jnp.int32, (n_sel, 1), 0)
        t_blk = t_row >> _LOG_SEL
        valid = blk <= t_blk
        forced = (blk == 0) | (blk == t_blk) | (blk == t_blk - 1)
        rank_t = jnp.where(valid, p_sel_t + FORCE_BONUS * jnp.where(forced, 1.0, 0.0), -FORCE_BONUS)
        n_tiles = n_sel // SUBLANES
        rank_tiles = [rank_t[r * SUBLANES:(r + 1) * SUBLANES] for r in range(n_tiles)]
        cnt_tiles = [jnp.zeros((SUBLANES, tq), F32) for _ in range(n_tiles)]
        blk8 = lax.broadcasted_iota(jnp.int32, (SUBLANES, 1), 0)
        for ii in range(n_sel):
            ri = rank_t[ii:ii + 1, :]
            for r in range(n_tiles):
                if ii < r * SUBLANES:
                    ahead = ri >= rank_tiles[r]
                elif ii >= (r + 1) * SUBLANES:
                    ahead = ri > rank_tiles[r]
                else:
                    tie_ok = jnp.where(blk8 + r * SUBLANES > ii, 1.0, 0.0)
                    ahead = (ri > rank_tiles[r]) | ((ri == rank_tiles[r]) & (tie_ok > 0.5))
                cnt_tiles[r] = cnt_tiles[r] + jnp.where(ahead, 1.0, 0.0)
        top = float(min(SEL_TOPK, n_sel))
        sel_bias = jnp.concatenate(
            [jnp.where(c < top, 0.0, NEG) for c in cnt_tiles] + [jnp.zeros((LANES - n_sel, tq), F32)], axis=0)
        mask_lanes = sel_bias.T
        if h == 0:
            mask_lanes = pltpu.roll(mask_lanes, HEAD_DIM, 1)
        mask_lanes = mask_lanes.astype(BF16)
        q4s_scr[h] = jnp.concatenate(
            [jnp.where(in_head, q[:, g * LANES:(g + 1) * LANES], mask_lanes) for g in range(NSA_GROUP)], axis=0)

    init_state()
    tk = NSA_TK_SEL
    n_full = t0 // tk

    def sel_tile(j, bias):
        k0 = pl.multiple_of(j * tk, tk)
        online_tile([ksa_ref[h, pl.ds(k0, tk), :] for h in heads], q4s_scr,
                    [vsa_ref[h, :, pl.ds(k0, tk)] for h in heads], bias)

    def sel_body(j, _):
        sel_tile(j, None)
        return 0

    lax.fori_loop(0, n_full, sel_body, 0)
    kpos_sel = n_full * tk + lax.broadcasted_iota(jnp.int32, (tk, 1), 0)
    sel_tile(n_full, jnp.where(kpos_sel <= t_row, 0.0, NEG))
    for h in heads:
        out_scr[h] = out_scr[h] + gate_row(h, 1) * finish(h)

    init_state()
    tkw = NSA_TK_WIN
    kpos_win = lax.broadcasted_iota(jnp.int32, (tkw, 1), 0)

    def win_body(j, _):
        k0 = pl.multiple_of(j * tkw, tkw)
        dist = t_row - (k0 + kpos_win)
        bias = jnp.where((dist >= 0) & (dist < WINDOW), 0.0, NEG)
        kt = kw_ref[pl.ds(k0, tkw), :]
        online_tile([kt, kt], q4_scr, [vwa_ref[h, :, pl.ds(k0, tkw)] for h in heads], bias)
        return 0

    lax.fori_loop(jnp.maximum(t0 - (WINDOW - 1), 0) // tkw, (t0 + tq + tkw - 1) // tkw, win_body, 0)
    both = jnp.concatenate([out_scr[h] + gate_row(h, 2) * finish(h) for h in heads], axis=0)
    for g in range(NSA_GROUP):
        o_ref[:, g * LANES:(g + 1) * LANES] = both[:, g * tq:(g + 1) * tq].T.astype(o_ref.dtype)


def _nsa(q3, kc, vc_t, ks_aug, vs_aug, kw, vw_aug, gt, ov_t):
    bsz, seq, qw = q3.shape
    tq = NSA_TQ
    n_rows = kc.shape[1]
    nh = NSA_KV_HEADS
    full = pl.BlockSpec((None, seq, LANES), lambda b, i: (b, 0, 0))
    head_tok = pl.BlockSpec((None, nh, seq, LANES), lambda b, i: (b, 0, 0, 0))
    head_feat = pl.BlockSpec((None, nh, LANES, seq), lambda b, i: (b, 0, 0, 0))
    tile = lambda w: pl.BlockSpec((None, tq, w), lambda b, i: (b, i, 0))
    cols = NSA_GROUP * tq
    return pl.pallas_call(
        functools.partial(_nsa_kernel, seq=seq),
        grid=(bsz, seq // tq),
        in_specs=[tile(qw),
                  pl.BlockSpec((None, n_rows, LANES), lambda b, i: (b, 0, 0)),
                  pl.BlockSpec((None, LANES, n_rows), lambda b, i: (b, 0, 0)),
                  head_tok, head_feat, full, head_feat, tile(LANES), _const_spec(ov_t.shape)],
        out_specs=tile(qw),
        out_shape=jax.ShapeDtypeStruct((bsz, seq, qw), BF16),
        scratch_shapes=[pltpu.VMEM((nh, 1, cols), F32),
                        pltpu.VMEM((nh, LANES, cols), F32),
                        pltpu.VMEM((nh, HEAD_DIM, cols), F32),
                        pltpu.VMEM((nh, cols, LANES), BF16),
                        pltpu.VMEM((nh, cols, LANES), BF16)],
        compiler_params=_cparams(("parallel", "arbitrary")),
        name="nsa_attention",
    )(q3, kc, vc_t, ks_aug, vs_aug, kw, vw_aug, gt, ov_t)


def _sb_kernel(q_ref, k_ref, vt_ref, o_ref, r_scr, acc_scr):
    tq = q_ref.shape[0]
    tk = SB_TILES * tq
    i_tile = pl.program_id(1)
    t0 = i_tile * tq
    q = q_ref[...]
    lane = lax.broadcasted_iota(jnp.int32, (1, LANES), 1)
    feat = lax.broadcasted_iota(jnp.int32, (LANES, 1), 0)
    t_row = t0 + lax.broadcasted_iota(jnp.int32, (1, tq), 1)
    kpos_col = lax.broadcasted_iota(jnp.int32, (tk, 1), 0)
    sr = lax.broadcasted_iota(jnp.int32, (tq, tq), 0)
    sc = lax.broadcasted_iota(jnp.int32, (tq, tq), 1)
    later = jnp.where(sc > sr, 1.0, 0.0).astype(BF16)
    qm = []
    for h in range(SB_HEADS):
        in_head = (lane >> _LOG_HEAD) == (h % 2)
        pair = q[:, (h // 2) * LANES:(h // 2 + 1) * LANES]
        qm.append(jnp.where(in_head, pair, jnp.zeros((), BF16)))
    r_scr[...] = jnp.zeros_like(r_scr)
    acc_scr[...] = jnp.zeros_like(acc_scr)

    def cond(c):
        j, r_max = c
        return (j >= 0) & (r_max > SB_EXIT)

    def body(c):
        j, _ = c
        k0 = pl.multiple_of(jnp.maximum(j + 1 - SB_TILES, 0) * tq, tq)
        kt = k_ref[pl.ds(k0, tk), :]
        z = jnp.concatenate([_dot_nt(kt[:, (h // 2) * LANES:(h // 2 + 1) * LANES], qm[h])
                             for h in range(SB_HEADS)], axis=1)
        kpos = k0 + kpos_col
        causal = jnp.where((kpos < t_row) & (kpos < (j + 1) * tq), 1.0, 0.0)
        causal = jnp.concatenate([causal] * SB_HEADS, axis=1)
        ls = jnp.minimum(z, 0.0) - jnp.log(1.0 + jnp.exp(-jnp.abs(z)))
        rest = (ls - z) * causal
        hi = rest.astype(BF16)
        lo = (rest - hi.astype(F32)).astype(BF16)
        r_prev = r_scr[...]
        totals = [jnp.sum(rest[a * tq:(a + 1) * tq], axis=0, keepdims=True) for a in range(SB_TILES)]
        after_tiles = []
        right = r_prev
        for a in reversed(range(SB_TILES)):
            rows = slice(a * tq, (a + 1) * tq)
            after_tiles.append(_dot(later, hi[rows]) + _dot(later, lo[rows]) + right)
            right = right + totals[a]
        after = jnp.concatenate(after_tiles[::-1], axis=0)
        w = (jnp.exp(ls + after) * causal).astype(BF16)
        for h in range(SB_HEADS):
            cols = slice(h * tq, (h + 1) * tq)
            vt = vt_ref[(h // 2) * LANES:(h // 2 + 1) * LANES, pl.ds(k0, tk)]
            acc_scr[:, cols] = acc_scr[:, cols] + _dot(vt, w[:, cols])
        r_new = right
        r_scr[...] = r_new
        return j - SB_TILES, jnp.max(r_new)

    lax.while_loop(cond, body, (i_tile, jnp.zeros((), F32)))
    for hp in range(SB_HEADS // 2):
        even = acc_scr[:, (2 * hp) * tq:(2 * hp + 1) * tq]
        odd = acc_scr[:, (2 * hp + 1) * tq:(2 * hp + 2) * tq]
        o_ref[:, hp * LANES:(hp + 1) * LANES] = jnp.where(feat < HEAD_DIM, even, odd).T.astype(o_ref.dtype)


def _sb(q3, k3, v_t):
    bsz, seq, w = q3.shape
    tq = SB_TQ
    full = pl.BlockSpec((None, seq, w), lambda b, i: (b, 0, 0))
    full_t = pl.BlockSpec((None, w, seq), lambda b, i: (b, 0, 0))
    tile = pl.BlockSpec((None, tq, w), lambda b, i: (b, i, 0))
    return pl.pallas_call(
        _sb_kernel,
        grid=(bsz, seq // tq),
        in_specs=[tile, full, full_t],
        out_specs=tile,
        out_shape=jax.ShapeDtypeStruct((bsz, seq, w), BF16),
        scratch_shapes=[pltpu.VMEM((1, SB_HEADS * tq), F32), pltpu.VMEM((LANES, SB_HEADS * tq), F32)],
        compiler_params=_cparams(("parallel", "arbitrary")),
        name="sb_attention",
    )(q3, k3, v_t)


def _merge_kernel(x_ref, g_ref, a_ref, b_ref, c_ref, wg_ref, wb_ref, wo_ref, o_ref):
    x = x_ref[...]
    d = x.shape[1]
    h = _rms(x, g_ref[...]).astype(BF16)
    mixed = None
    for n, br in enumerate((a_ref, b_ref, c_ref)):
        gate = jax.nn.sigmoid(_dot(h, wg_ref[:, n * d:(n + 1) * d]))
        y = gate * _dot(br[...], wb_ref[n])
        mixed = y if mixed is None else mixed + y
    o_ref[...] = x + _dot(mixed.astype(BF16), wo_ref[...])


def _merge(x2, layer, g, o_a, o_b, o_c, w_gate, w_branch, w_out):
    t, d = x2.shape
    tm = min(TM_PROJ, t)
    row = lambda i: (i, 0)
    bw = o_a.shape[1]
    return pl.pallas_call(
        _merge_kernel,
        grid=(t // tm,),
        in_specs=[pl.BlockSpec((tm, d), row), _layer_spec(g, layer),
                  pl.BlockSpec((tm, bw), row), pl.BlockSpec((tm, bw), row), pl.BlockSpec((tm, bw), row),
                  _layer_spec(w_gate, layer), _layer_spec(w_branch, layer), _layer_spec(w_out, layer)],
        out_specs=pl.BlockSpec((tm, d), row),
        out_shape=jax.ShapeDtypeStruct((t, d), F32),
        compiler_params=_cparams(("parallel",)),
        name="gated_merge",
    )(x2, g, o_a, o_b, o_c, w_gate, w_branch, w_out)


def _ffn_kernel(x_ref, halo_ref, g_ref, wup_ref, cw_ref, cb_ref, wdn_ref, gout_ref, o_ref, act_scr, *,
                tiles_per_seq, norm_out):
    x = x_ref[...]
    tm = x.shape[0]
    d_ff = wdn_ref.shape[0]
    at_start = (pl.program_id(0) % tiles_per_seq) == 0
    g = g_ref[...]
    h_halo = jnp.where(at_start, 0.0, _rms(halo_ref[...], g))
    h = jnp.concatenate([h_halo, _rms(x, g)], axis=0).astype(BF16)
    for f in range(d_ff // FF_TILE):
        halves = []
        for base in (0, d_ff):
            cols = slice(base + f * FF_TILE, base + (f + 1) * FF_TILE)
            u = _dot(h, wup_ref[:, cols])
            w = cw_ref[:, cols]
            conv = (w[2:3] * u[SUBLANES:]
                    + w[1:2] * pltpu.roll(u, 1, 0)[SUBLANES:]
                    + w[0:1] * pltpu.roll(u, 2, 0)[SUBLANES:]
                    + cb_ref[:, cols])
            halves.append(conv)
        act_scr[:, f * FF_TILE:(f + 1) * FF_TILE] = (jax.nn.silu(halves[0]) * halves[1]).astype(BF16)
    y = x + _dot(act_scr[...], wdn_ref[...])
    o_ref[...] = _rms(y, gout_ref[...]) if norm_out else y


def _ffn(x2, layer, g, w_up, conv_w, conv_b, w_down, g_out, seq, norm_out):
    t, d = x2.shape
    tm = min(TM_FFN, seq)
    tiles_per_seq = seq // tm
    row = lambda i: (i, 0)
    halo = lambda i: (jnp.maximum(i * (tm // SUBLANES) - 1, 0), 0)
    return pl.pallas_call(
        functools.partial(_ffn_kernel, tiles_per_seq=tiles_per_seq, norm_out=norm_out),
        grid=(t // tm,),
        in_specs=[pl.BlockSpec((tm, d), row), pl.BlockSpec((SUBLANES, d), halo), _layer_spec(g, layer),
                  _layer_spec(w_up, layer), _layer_spec(conv_w, layer), _layer_spec(conv_b, layer),
                  _layer_spec(w_down, layer), _const_spec(g_out.shape)],
        out_specs=pl.BlockSpec((tm, d), row),
        out_shape=jax.ShapeDtypeStruct((t, d), F32),
        scratch_shapes=[pltpu.VMEM((tm, w_down.shape[1]), BF16)],
        compiler_params=_cparams(("parallel",)),
        name="conv_ffn",
    )(x2, x2, g, w_up, conv_w, conv_b, w_down, g_out)


def _norm_kernel(x_ref, g_ref, o_ref):
    o_ref[...] = _rms(x_ref[...], g_ref[...])


def _final_norm(x2, g):
    t, d = x2.shape
    tm = min(TM_PROJ, t)
    return pl.pallas_call(
        _norm_kernel,
        grid=(t // tm,),
        in_specs=[pl.BlockSpec((tm, d), lambda i: (i, 0)), _const_spec((1, d))],
        out_specs=pl.BlockSpec((tm, d), lambda i: (i, 0)),
        out_shape=jax.ShapeDtypeStruct((t, d), F32),
        compiler_params=_cparams(("parallel",)),
        name="final_norm",
    )(x2, g)


def _nsa_slot_order():
    return [NSA_GROUP * (s % 2) + s // 2 for s in range(NSA_HEADS)]


def _pack_w_in(w):
    d = w.shape[0]
    o_q, o_kv, o_g, o_sb, o_gm = 512, 1024, 1792, 1816, 3352
    u = w[:, 0:o_q]
    q = w[:, o_q:o_kv].reshape(d, NSA_HEADS, HEAD_DIM)[:, jnp.array(_nsa_slot_order())].reshape(d, -1)
    kv = w[:, o_kv:o_g].reshape(d, 3, 2, NSA_KV_HEADS * HEAD_DIM)
    k = kv[:, :, 0].reshape(d, -1)
    v = kv[:, :, 1]
    gates = jnp.pad(w[:, o_g:o_sb], ((0, 0), (0, LANES - (o_sb - o_g))))
    sb = w[:, o_sb:o_gm]
    sb_w = SB_HEADS * HEAD_DIM
    packed = jnp.concatenate([u, q, k, v[:, 0], gates, sb[:, :2 * sb_w]], axis=1).astype(BF16)
    w_vt = jnp.concatenate([v[:, 1], v[:, 2], sb[:, 2 * sb_w:]], axis=1).T.astype(BF16)
    return packed, w_vt, w[:, o_gm:].astype(BF16)


def _rope_tables(seq):
    inv_freq = 1.0 / (ROPE_THETA ** (jnp.arange(0, HEAD_DIM, 2, dtype=F32) / HEAD_DIM))
    ang = jnp.arange(seq, dtype=F32)[:, None] * inv_freq[None, :]
    cos, sin = jnp.cos(ang), jnp.sin(ang)
    cos_t = jnp.tile(cos, (1, LANES // (HEAD_DIM // 2)))
    sin_t = jnp.tile(jnp.concatenate([-sin, sin], axis=1), (1, LANES // HEAD_DIM))
    return cos_t, sin_t


def _overlap_t(seq):
    n_rows = seq // CMP_STRIDE
    n_sel = seq // SEL_LEN
    cmp_start = jnp.arange(n_rows) * CMP_STRIDE
    sel_start = jnp.arange(n_sel) * SEL_LEN
    ov = ((cmp_start[None, :] < sel_start[:, None] + SEL_LEN)
          & (cmp_start[None, :] + CMP_LEN > sel_start[:, None])
          & (jnp.arange(n_rows)[None, :] < (seq - CMP_LEN) // CMP_STRIDE + 1))
    return ov.astype(BF16)


def kernel(x, norm_mix, w_in, ssm_lam_re, ssm_lam_im, ssm_log_step, ssm_b_re, ssm_b_im, ssm_c_re, ssm_c_im, ssm_d, ssm_w_glu, cmp_w1_k, cmp_w2_k, cmp_pos_k, cmp_w1_v, cmp_w2_v, cmp_pos_v, w_branch, w_out, norm_ffn, ffn_w_up, ffn_conv_w, ffn_conv_b, ffn_w_down, norm_final):
    bsz, seq, d = x.shape
    depth = w_in.shape[0]
    t = bsz * seq
    assert seq % 512 == 0 and seq // SEL_LEN <= HEAD_DIM
    if depth == 0:
        return _final_norm(x.reshape(t, d).astype(F32), norm_final.reshape(1, d)).reshape(x.shape).astype(x.dtype)
    cos_t, sin_t = _rope_tables(seq)
    ov_t = _overlap_t(seq)
    w_packed, w_vt, w_gate = jax.vmap(_pack_w_in)(w_in)
    s5_tabs = jax.vmap(_s5_tables)(ssm_lam_re, ssm_lam_im, ssm_log_step, ssm_b_re, ssm_b_im, ssm_c_re, ssm_c_im)
    s5_d = ssm_d.reshape(depth, 1, -1).astype(F32)
    s5_glu = ssm_w_glu.astype(BF16)
    cmp_k = jax.vmap(_compress_weights)(cmp_w1_k, cmp_w2_k, cmp_pos_k)
    cmp_v = jax.vmap(functools.partial(_compress_weights, feature_major_out=True))(cmp_w1_v, cmp_w2_v, cmp_pos_v)
    slot_rows = jnp.array(_nsa_slot_order())
    wb_nsa = w_branch[:, 1].reshape(depth, NSA_HEADS, HEAD_DIM, d)[:, slot_rows].reshape(depth, BRANCH_WIDTH, d)
    wb_all = jnp.stack([w_branch[:, 0], wb_nsa, w_branch[:, 2]], axis=1).astype(BF16)
    wo_all = w_out.astype(BF16)
    g_mix = norm_mix.reshape(depth, 1, d).astype(F32)
    g_ffn = norm_ffn.reshape(depth, 1, d).astype(F32)
    up_all = ffn_w_up.astype(BF16)
    down_all = ffn_w_down.astype(BF16)
    conv_w = ffn_conv_w.astype(F32)
    conv_b = ffn_conv_b.reshape(depth, 1, -1).astype(F32)
    g_final = norm_final.reshape(1, d).astype(F32)

    x2 = x.reshape(t, d).astype(F32)
    b3 = lambda a: a.reshape(bsz, seq, a.shape[-1])
    rows16 = lambda a: a.reshape(bsz, seq // CMP_STRIDE, a.shape[-1])
    for l in range(depth):
        (u, q, kc_raw, ks_aug, kw, vc_raw, vs_aug, vw_aug, gt, sq, sk, sv_t) = _proj_in(
            x2, l, g_mix, w_packed, w_vt, cos_t, sin_t, bsz, seq)
        o_ssm = _s5(b3(u), l, s5_tabs, s5_d, s5_glu)
        kc, vc_t = _compress(rows16(kc_raw), rows16(vc_raw), l, cmp_k, cmp_v)
        o_nsa = _nsa(b3(q), kc, vc_t, ks_aug, vs_aug, b3(kw), vw_aug, b3(gt), ov_t)
        o_sb = _sb(b3(sq), b3(sk), sv_t)
        x2 = _merge(x2, l, g_mix, o_ssm.reshape(t, -1), o_nsa.reshape(t, -1), o_sb.reshape(t, -1),
                    w_gate, wb_all, wo_all)
        x2 = _ffn(x2, l, g_ffn, up_all, conv_w, conv_b, down_all, g_final, seq, norm_out=(l == depth - 1))
    return x2.reshape(bsz, seq, d).astype(x.dtype)
```

```python
import functools
import math

import jax
import jax.numpy as jnp
from jax import lax
from jax.experimental import pallas as pl
from jax.experimental.pallas import tpu as pltpu

F32 = jnp.float32
BF16 = jnp.bfloat16

HEAD_DIM = 64
RMS_EPS = 1e-6
ROPE_THETA = 10000.0
SSM_GROUP = 16
SSM_STATE = 64
NSA_HEADS = 8
NSA_KV_HEADS = 2
NSA_GROUP = NSA_HEADS // NSA_KV_HEADS
CMP_LEN = 32
CMP_STRIDE = 16
CMP_HIDDEN = 128
SEL_LEN = 64
SEL_TOPK = 16
WINDOW = 512
FORCE_BONUS = 1e6
SB_HEADS = 8
CONV_WIDTH = 3
BRANCH_WIDTH = 512

LANES = 128
SUBLANES = 8
VMEM_LIMIT = 56 * 1024 * 1024

TM_PROJ = 512
TM_FFN = 512
FF_TILE = 256
S5_CHUNK = 128
S5_LANES = 256
S5_MXU_SHIFTS = 4
NSA_TQ = 256
NSA_TK_SEL = 512
SB_TQ = 128
SB_TILES = 3

_LOG_HEAD = HEAD_DIM.bit_length() - 1
_LOG_SEL = SEL_LEN.bit_length() - 1
NEG = -1e30
SB_EXIT = -104.0


def _cparams(sem):
    return pltpu.CompilerParams(dimension_semantics=sem, vmem_limit_bytes=VMEM_LIMIT)


def _const_spec(shape):
    n = len(shape)
    return pl.BlockSpec(shape, lambda *_: (0,) * n, pipeline_mode=pl.Buffered(1))


def _layer_spec(stacked, layer):
    n = stacked.ndim - 1
    return pl.BlockSpec((None,) + stacked.shape[1:], lambda *_: (layer,) + (0,) * n,
                        pipeline_mode=pl.Buffered(1))


def _rms(x, g):
    return x * lax.rsqrt(jnp.mean(x * x, axis=-1, keepdims=True) + RMS_EPS) * g


def _dot(a, b):
    return jnp.dot(a, b, preferred_element_type=F32)


def _dot_nt(a, b):
    return lax.dot_general(a, b, (((1,), (1,)), ((), ())), preferred_element_type=F32)


def _split3(x):
    h = x.astype(BF16)
    r = x - h.astype(F32)
    m = r.astype(BF16)
    lo = (r - m.astype(F32)).astype(BF16)
    return h, m, lo


_C_U = 0
_C_ROPE = 512
_C_V = 1408
_C_G = 1536
_C_SB = 1664
_C_END = 2688
_R_VS, _R_VW, _R_SV, _R_END = 0, 128, 256, 768


def _proj_kernel(x_ref, g_ref, w_ref, wvt_ref, cos_ref, sin_ref,
                 u_ref, q_ref, kc_ref, ksa_ref, kw_ref, vc_ref, vsa_ref, vwa_ref, gt_ref,
                 sq_ref, sk_ref, svt_ref, rows_scr, *, n_pos):
    tm = x_ref.shape[0]
    h = _rms(x_ref[...], g_ref[...]).astype(BF16)

    def store_rows16(val, out_ref):
        rows_scr[...] = val
        for b in range(CMP_STRIDE):
            out_ref[:, b * LANES:(b + 1) * LANES] = rows_scr[pl.ds(b, tm // CMP_STRIDE, stride=CMP_STRIDE), :]

    def mm(a, b):
        return _dot(h, w_ref[:, a:b])

    vt = _dot_nt(wvt_ref[...], h)
    ones_rows = jnp.where(lax.broadcasted_iota(jnp.int32, (HEAD_DIM, tm), 0) == 0, 1.0, 0.0)
    for hh in range(NSA_KV_HEADS):
        vsa_ref[hh] = jnp.concatenate(
            [vt[_R_VS + hh * HEAD_DIM:_R_VS + (hh + 1) * HEAD_DIM], ones_rows], axis=0).astype(BF16)
        vwa_ref[hh] = jnp.concatenate(
            [vt[_R_VW + hh * HEAD_DIM:_R_VW + (hh + 1) * HEAD_DIM], ones_rows], axis=0).astype(BF16)
    svt_ref[...] = vt[_R_SV:_R_END].astype(BF16)

    u_ref[...] = mm(_C_U, _C_ROPE)

    r = mm(_C_ROPE, _C_V)
    cos = cos_ref[...]
    sin = sin_ref[...]
    lane = lax.broadcasted_iota(jnp.int32, cos.shape, 1)
    first = (lane & (HEAD_DIM - 1)) < (HEAD_DIM // 2)
    roped = []
    for c in range((_C_V - _C_ROPE) // LANES):
        rc = r[:, c * LANES:(c + 1) * LANES]
        partner = jnp.where(first, pltpu.roll(rc, LANES - HEAD_DIM // 2, 1),
                            pltpu.roll(rc, HEAD_DIM // 2, 1))
        roped.append(rc * cos + partner * sin)
    scale = HEAD_DIM ** -0.5
    for c in range(4):
        q_ref[:, c * LANES:(c + 1) * LANES] = (roped[c] * scale).astype(BF16)
    store_rows16(roped[4], kc_ref)
    kw_ref[...] = roped[6].astype(BF16)
    pos = (pl.program_id(0) % n_pos) * tm + lax.broadcasted_iota(jnp.int32, (tm, 1), 0)
    blk = pos >> _LOG_SEL
    lane_k = lax.broadcasted_iota(jnp.int32, (1, LANES), 1)
    for hh in range(NSA_KV_HEADS):
        own = (lane_k >> _LOG_HEAD) == hh
        one_hot = jnp.where(blk == (lane_k & (HEAD_DIM - 1)), 1.0, 0.0)
        ksa_ref[hh] = jnp.where(own, roped[5], one_hot).astype(BF16)

    store_rows16(mm(_C_V, _C_G), vc_ref)
    gt_ref[...] = jax.nn.sigmoid(mm(_C_G, _C_SB))

    sb = mm(_C_SB, _C_END)
    sq_ref[...] = (sb[:, 0:512] * scale).astype(BF16)
    sk_ref[...] = sb[:, 512:1024].astype(BF16)


def _proj_in(x2, layer, g, w_packed, w_vt, cos_t, sin_t, bsz, seq):
    t, d = x2.shape
    tm = min(TM_PROJ, seq)
    n_pos = seq // tm
    row = lambda i: (i, 0)
    pos = lambda i: (i % n_pos, 0)
    nh = NSA_KV_HEADS

    def tok(w, dt):
        return pl.BlockSpec((tm, w), row), jax.ShapeDtypeStruct((t, w), dt)

    def feat_major(w, dt):
        return (pl.BlockSpec((None, w, tm), lambda i: (i // n_pos, 0, i % n_pos)),
                jax.ShapeDtypeStruct((bsz, w, seq), dt))

    def head_tok(dt):
        return (pl.BlockSpec((None, nh, tm, LANES), lambda i: (i // n_pos, 0, i % n_pos, 0)),
                jax.ShapeDtypeStruct((bsz, nh, seq, LANES), dt))

    def head_feat(dt):
        return (pl.BlockSpec((None, nh, LANES, tm), lambda i: (i // n_pos, 0, 0, i % n_pos)),
                jax.ShapeDtypeStruct((bsz, nh, LANES, seq), dt))

    def rows16():
        return (pl.BlockSpec((tm // CMP_STRIDE, CMP_STRIDE * LANES), row),
                jax.ShapeDtypeStruct((t // CMP_STRIDE, CMP_STRIDE * LANES), F32))

    outs = [tok(512, F32), tok(512, BF16), rows16(), head_tok(BF16), tok(128, BF16),
            rows16(), head_feat(BF16), head_feat(BF16), tok(128, F32),
            tok(512, BF16), tok(512, BF16), feat_major(512, BF16)]
    return pl.pallas_call(
        functools.partial(_proj_kernel, n_pos=n_pos),
        grid=(t // tm,),
        in_specs=[pl.BlockSpec((tm, d), row),
                  _layer_spec(g, layer),
                  _layer_spec(w_packed, layer),
                  _layer_spec(w_vt, layer),
                  pl.BlockSpec((tm, LANES), pos),
                  pl.BlockSpec((tm, LANES), pos)],
        out_specs=[spec for spec, _ in outs],
        out_shape=[shape for _, shape in outs],
        scratch_shapes=[pltpu.VMEM((tm, LANES), F32)],
        compiler_params=_cparams(("parallel",)),
        name="proj_in",
    )(x2, g, w_packed, w_vt, cos_t, sin_t)


def _s5_kernel(u_ref, b8_ref, ccat_ref, pw_re_ref, pw_im_ref, cy_re_ref, cy_im_ref, d_ref, wglu_ref, o_ref,
               car_re, car_im):
    @pl.when(pl.program_id(1) == 0)
    def _():
        car_re[...] = jnp.zeros_like(car_re)
        car_im[...] = jnp.zeros_like(car_im)

    u = u_ref[...]
    chunk, width = u.shape
    n_tiles = b8_ref.shape[0]
    n_sub = chunk // SUBLANES
    u3 = u.reshape(n_sub, SUBLANES, width)
    row8 = lax.broadcasted_iota(jnp.int32, (1, SUBLANES, 1), 1)
    shifted = [u.astype(BF16)]
    for s in range(1, S5_MXU_SHIFTS):
        shifted.append(jnp.where(row8 >= s, pltpu.roll(u3, s, 1), 0.0).reshape(chunk, width).astype(BF16))
    ucat = [jnp.concatenate([sh[:, b * LANES:(b + 1) * LANES] for sh in shifted], axis=1)
            for b in range(width // LANES)]
    y_blocks = [None] * (n_tiles // 2)
    for j in range(n_tiles):
        blk = j // 2
        x = _dot(ucat[blk], b8_ref[j])
        hr = x[:, :S5_LANES].reshape(n_sub, SUBLANES, S5_LANES)
        hi = x[:, S5_LANES:].reshape(n_sub, SUBLANES, S5_LANES)
        sr = pltpu.roll(hr, S5_MXU_SHIFTS, 1)
        si = pltpu.roll(hi, S5_MXU_SHIFTS, 1)
        cr = pw_re_ref[j]
        ci = pw_im_ref[j]
        hr, hi = hr + cr * sr - ci * si, hi + cr * si + ci * sr
        pr = cy_re_ref[j]
        pi = cy_im_ref[j]
        c_re = car_re[j:j + 1, :]
        c_im = car_im[j:j + 1, :]
        tiles_re, tiles_im = [], []
        for v in range(n_sub):
            t_re = hr[v] + pr * c_re - pi * c_im
            t_im = hi[v] + pr * c_im + pi * c_re
            c_re = t_re[SUBLANES - 1:SUBLANES, :]
            c_im = t_im[SUBLANES - 1:SUBLANES, :]
            tiles_re.append(t_re)
            tiles_im.append(t_im)
        car_re[j:j + 1, :] = c_re
        car_im[j:j + 1, :] = c_im
        hcat = jnp.concatenate([jnp.concatenate(tiles_re, axis=0), jnp.concatenate(tiles_im, axis=0)],
                               axis=1).astype(BF16)
        yj = _dot(hcat, ccat_ref[j])
        y_blocks[blk] = yj if y_blocks[blk] is None else y_blocks[blk] + yj
    y = jnp.concatenate(y_blocks, axis=1) + d_ref[...] * u
    z = jax.nn.gelu(y)
    o_ref[...] = (z * jax.nn.sigmoid(_dot(z.astype(BF16), wglu_ref[...]))).astype(o_ref.dtype)


def _s5_tables(lam_re, lam_im, log_step, b_re, b_im, c_re, c_im):
    g_n, p_n = lam_re.shape
    i_n = b_re.shape[-1]
    width = g_n * i_n
    lanes_total = g_n * p_n
    n_tiles = lanes_total // S5_LANES
    step = jnp.exp(log_step.astype(F32))[:, None]
    lr = lam_re.astype(F32)
    li = lam_im.astype(F32)
    mag = jnp.exp(lr * step)
    a_re = mag * jnp.cos(li * step)
    a_im = mag * jnp.sin(li * step)
    den = lr * lr + li * li
    nr = a_re - 1.0
    f_re = (nr * lr + a_im * li) / den
    f_im = (a_im * lr - nr * li) / den
    br = b_re.astype(F32)
    bi = b_im.astype(F32)
    bb_re = f_re[..., None] * br - f_im[..., None] * bi
    bb_im = f_re[..., None] * bi + f_im[..., None] * br
    log_mag = (lr * step).reshape(1, lanes_total)
    ang = (li * step).reshape(1, lanes_total)

    def powers(n):
        m = jnp.exp(n * log_mag)
        return m * jnp.cos(n * ang), m * jnp.sin(n * ang)

    gpt = S5_LANES // p_n
    gpb = LANES // i_n
    tile_id = jnp.arange(n_tiles)
    in_tile = (jnp.arange(gpb)[None, :, None]
               == (tile_id[:, None, None] % (gpb // gpt)) * gpt + jnp.arange(gpt)[None, None, :]).astype(F32)

    n_sh = S5_MXU_SHIFTS
    as_re, as_im = powers(jnp.arange(n_sh, dtype=F32)[:, None])
    as_re = as_re.reshape(n_sh, g_n, p_n, 1)
    as_im = as_im.reshape(n_sh, g_n, p_n, 1)
    w_re = as_re * bb_re[None] - as_im * bb_im[None]
    w_im = as_re * bb_im[None] + as_im * bb_re[None]

    def in_tiles(w):
        w = w.reshape(n_sh, n_tiles, gpt, p_n, i_n).transpose(1, 0, 4, 2, 3)
        out = in_tile[:, None, :, None, :, None] * w[:, :, None, :, :, :]
        return out.reshape(n_tiles, n_sh * LANES, S5_LANES)

    b8 = jnp.concatenate([in_tiles(w_re), in_tiles(w_im)], axis=2).astype(BF16)

    def out_tiles(c):
        c = c.astype(F32).reshape(n_tiles, gpt, i_n, p_n).transpose(0, 1, 3, 2)
        out = in_tile.transpose(0, 2, 1)[:, :, None, :, None] * c[:, :, :, None, :]
        return out.reshape(n_tiles, S5_LANES, LANES)

    ccat = jnp.concatenate([out_tiles(c_re), -out_tiles(c_im)], axis=1).astype(BF16)

    cy_re, cy_im = powers(jnp.arange(1, SUBLANES + 1, dtype=F32)[:, None])

    def tiles(a):
        return a.reshape(a.shape[0], n_tiles, S5_LANES).transpose(1, 0, 2)

    live = (jnp.arange(SUBLANES) >= n_sh).astype(F32)[None, :, None]
    pw_re, pw_im = powers(jnp.full((1, 1), float(n_sh), F32))
    return b8, ccat, tiles(pw_re) * live, tiles(pw_im) * live, tiles(cy_re), tiles(cy_im)


def _s5(u3, layer, tables, d_row, wglu):
    bsz, seq, width = u3.shape
    consts = tuple(tables) + (d_row, wglu)
    chunk = min(S5_CHUNK, seq)
    n_tiles = tables[0].shape[1]
    return pl.pallas_call(
        _s5_kernel,
        grid=(bsz, seq // chunk),
        in_specs=[pl.BlockSpec((None, chunk, width), lambda b, c: (b, c, 0))]
                 + [_layer_spec(a, layer) for a in consts],
        out_specs=pl.BlockSpec((None, chunk, width), lambda b, c: (b, c, 0)),
        out_shape=jax.ShapeDtypeStruct((bsz, seq, width), BF16),
        scratch_shapes=[pltpu.VMEM((n_tiles, S5_LANES), F32), pltpu.VMEM((n_tiles, S5_LANES), F32)],
        compiler_params=_cparams(("parallel", "arbitrary")),
        name="s5_scan",
    )(u3, *consts)


def _compress_kernel(k_ref, v_ref, w1a_k, w1b_k, pa_k, pb_k, w2_k, w1a_v, w1b_v, pa_v, pb_v, w2_v,
                     kc_ref, vc_ref):
    def hidden(t_ref, w1a, w1b, pa, pb):
        r = t_ref[...]
        n = r.shape[0]
        first = _dot((r + pa[...]).astype(BF16), w1a[...])
        second = _dot((r + pb[...]).astype(BF16), w1b[...])
        rowi = lax.broadcasted_iota(jnp.int32, second.shape, 0)
        nxt = jnp.where(rowi < n - 1, pltpu.roll(second, n - 1, 0), 0.0)
        return jax.nn.gelu(first + nxt).astype(BF16)

    kc_ref[...] = _dot(hidden(k_ref, w1a_k, w1b_k, pa_k, pb_k), w2_k[...]).astype(kc_ref.dtype)
    vc_ref[...] = _dot_nt(w2_v[...], hidden(v_ref, w1a_v, w1b_v, pa_v, pb_v)).astype(vc_ref.dtype)


def _compress_weights(w1, w2, pos, feature_major_out=False):
    half = CMP_LEN // 2
    eye = jnp.eye(NSA_KV_HEADS, dtype=F32)

    def first_layer(w):
        m = w[:, None, :, None, :] * eye[None, :, None, :, None]
        return m.reshape(half * NSA_KV_HEADS * HEAD_DIM, NSA_KV_HEADS * CMP_HIDDEN).astype(BF16)

    def pos_row(p):
        return jnp.broadcast_to(p[:, None, :], (half, NSA_KV_HEADS, HEAD_DIM)).reshape(1, -1).astype(F32)

    w2_bd = (eye[:, None, :, None] * w2[None, :, None, :]).reshape(
        NSA_KV_HEADS * CMP_HIDDEN, NSA_KV_HEADS * HEAD_DIM).astype(BF16)
    if feature_major_out:
        w2_bd = w2_bd.T
    return (first_layer(w1[:half]), first_layer(w1[half:]), pos_row(pos[:half]), pos_row(pos[half:]), w2_bd)


def _compress(kc_raw, vc_raw, layer, wk, wv):
    bsz, n_rows, feat = kc_raw.shape
    consts = list(wk) + list(wv)
    blk = pl.BlockSpec((None, n_rows, feat), lambda b: (b, 0, 0))
    return pl.pallas_call(
        _compress_kernel,
        grid=(bsz,),
        in_specs=[blk, blk] + [_layer_spec(c, layer) for c in consts],
        out_specs=[pl.BlockSpec((None, n_rows, LANES), lambda b: (b, 0, 0)),
                   pl.BlockSpec((None, LANES, n_rows), lambda b: (b, 0, 0))],
        out_shape=[jax.ShapeDtypeStruct((bsz, n_rows, LANES), BF16),
                   jax.ShapeDtypeStruct((bsz, LANES, n_rows), BF16)],
        compiler_params=_cparams(("parallel",)),
        name="nsa_compress",
    )(kc_raw, vc_raw, *consts)


def _nsa_kernel(q_ref, kc_ref, vct_ref, ksa_ref, vsa_ref, kw_ref, vwa_ref, gt_ref, ovt_ref, o_ref,
                m_scr, acc_scr, out_scr, q4_scr, q4s_scr, *, seq):
    tq = q_ref.shape[0]
    n_cmp_rows = kc_ref.shape[0]
    n_sel = seq // SEL_LEN
    i_tile = pl.program_id(1)
    t0 = i_tile * tq
    heads = range(NSA_KV_HEADS)

    q = q_ref[...]
    gt_t = gt_ref[...].T
    lane = lax.broadcasted_iota(jnp.int32, (1, LANES), 1)
    t_row = t0 + lax.broadcasted_iota(jnp.int32, (1, tq), 1)

    def tile_lanes(a):
        return jnp.concatenate([a] * NSA_GROUP, axis=1)

    def init_state():
        m_scr[...] = jnp.full_like(m_scr, NEG)
        acc_scr[...] = jnp.zeros_like(acc_scr)

    def online_tile(kts, q_scr, vts, bias):
        scores = [_dot_nt(kts[h], q_scr[h]) for h in heads]
        for h in heads:
            s = scores[h]
            if bias is not None:
                s = s + tile_lanes(bias)
            m_prev = m_scr[h]
            m_next = jnp.maximum(m_prev, jnp.max(s, axis=0, keepdims=True))
            alpha = jnp.exp(m_prev - m_next)
            p = jnp.exp(s - m_next).astype(BF16)
            acc_scr[h] = alpha * acc_scr[h] + _dot(vts[h], p)
            m_scr[h] = m_next

    def finish(h):
        acc = acc_scr[h]
        return acc[0:HEAD_DIM] * (1.0 / acc[HEAD_DIM:HEAD_DIM + 1])

    def gate_row(h, br):
        return jnp.concatenate([gt_t[(h * NSA_GROUP + g) * 3 + br:(h * NSA_GROUP + g) * 3 + br + 1, :]
                                for g in range(NSA_GROUP)], axis=1)

    for h in heads:
        in_head = (lane >> _LOG_HEAD) == h
        q4 = jnp.concatenate(
            [jnp.where(in_head, q[:, g * LANES:(g + 1) * LANES], jnp.zeros((), BF16)) for g in range(NSA_GROUP)],
            axis=0)
        q4_scr[h] = q4

        s = _dot_nt(kc_ref[...], q4)
        n_col = lax.broadcasted_iota(jnp.int32, (n_cmp_rows, 1), 0)
        vis = (n_col * CMP_STRIDE + (CMP_LEN - 1)) <= t_row
        s = s + tile_lanes(jnp.where(vis, 0.0, NEG))
        m = jnp.max(s, axis=0, keepdims=True)
        e = jnp.exp(s - m)
        any_vis = tile_lanes(jnp.where(t_row >= CMP_LEN - 1, 1.0, 0.0))
        p = e * (any_vis / jnp.maximum(jnp.sum(e, axis=0, keepdims=True), 1e-30))
        o_cmp = _dot(vct_ref[...], p.astype(BF16))
        out_scr[h] = gate_row(h, 0) * o_cmp[h * HEAD_DIM:(h + 1) * HEAD_DIM]

        p_sum = p[:, 0:tq]
        for g in range(1, NSA_GROUP):
            p_sum = p_sum + p[:, g * tq:(g + 1) * tq]
        ov_t = ovt_ref[...]
        p_sel_t = sum(_dot(ov_t, part) for part in _split3(p_sum))
        blk = lax.broadcasted_iota(jnp.int32, (n_sel, 1), 0)
        t_blk = t_row >> _LOG_SEL
        valid = blk <= t_blk
        forced = (blk == 0) | (blk == t_blk) | (blk == t_blk - 1)
        rank_t = jnp.where(valid, p_sel_t + FORCE_BONUS * jnp.where(forced, 1.0, 0.0), -FORCE_BONUS)
        n_tiles = n_sel // SUBLANES
        rank_tiles = [rank_t[r * SUBLANES:(r + 1) * SUBLANES] for r in range(n_tiles)]
        cnt_tiles = [jnp.zeros((SUBLANES, tq), F32) for _ in range(n_tiles)]
        blk8 = lax.broadcasted_iota(jnp.int32, (SUBLANES, 1), 0)
        for ii in range(n_sel):
            ri = rank_t[ii:ii + 1, :]
            for r in range(n_tiles):
                if ii < r * SUBLANES:
                    ahead = ri >= rank_tiles[r]
                elif ii >= (r + 1) * SUBLANES:
                    ahead = ri > rank_tiles[r]
                else:
                    tie_ok = jnp.where(blk8 + r * SUBLANES > ii, 1.0, 0.0)
                    ahead = (ri > rank_tiles[r]) | ((ri == rank_tiles[r]) & (tie_ok > 0.5))
                cnt_tiles[r] = cnt_tiles[r] + jnp.where(ahead, 1.0, 0.0)
        top = float(min(SEL_TOPK, n_sel))
        sel_bias = jnp.concatenate(
            [jnp.where(c < top, 0.0, NEG) for c in cnt_tiles] + [jnp.zeros((LANES - n_sel, tq), F32)], axis=0)
        mask_lanes = sel_bias.T
        if h == 0:
            mask_lanes = pltpu.roll(mask_lanes, HEAD_DIM, 1)
        mask_lanes = mask_lanes.astype(BF16)
        q4s_scr[h] = jnp.concatenate(
            [jnp.where(in_head, q[:, g * LANES:(g + 1) * LANES], mask_lanes) for g in range(NSA_GROUP)], axis=0)

    init_state()
    tk = NSA_TK_SEL
    n_full = t0 // tk

    def sel_tile(j, bias):
        k0 = pl.multiple_of(j * tk, tk)
        online_tile([ksa_ref[h, pl.ds(k0, tk), :] for h in heads], q4s_scr,
                    [vsa_ref[h, :, pl.ds(k0, tk)] for h in heads], bias)

    def sel_body(j, _):
        sel_tile(j, None)
        return 0

    lax.fori_loop(0, n_full, sel_body, 0)
    kpos_sel = n_full * tk + lax.broadcasted_iota(jnp.int32, (tk, 1), 0)
    sel_tile(n_full, jnp.where(kpos_sel <= t_row, 0.0, NEG))
    for h in heads:
        out_scr[h] = out_scr[h] + gate_row(h, 1) * finish(h)

    init_state()
    tkw = WINDOW + tq
    k0 = pl.multiple_of(jnp.maximum(t0 - WINDOW, 0), tq)
    dist = t_row - (k0 + lax.broadcasted_iota(jnp.int32, (tkw, 1), 0))
    kt = kw_ref[pl.ds(k0, tkw), :]
    online_tile([kt, kt], q4_scr, [vwa_ref[h, :, pl.ds(k0, tkw)] for h in heads],
                jnp.where((dist >= 0) & (dist < WINDOW), 0.0, NEG))
    both = jnp.concatenate([out_scr[h] + gate_row(h, 2) * finish(h) for h in heads], axis=0)
    for g in range(NSA_GROUP):
        o_ref[:, g * LANES:(g + 1) * LANES] = both[:, g * tq:(g + 1) * tq].T.astype(o_ref.dtype)


def _nsa(q3, kc, vc_t, ks_aug, vs_aug, kw, vw_aug, gt, ov_t):
    bsz, seq, qw = q3.shape
    tq = NSA_TQ
    n_rows = kc.shape[1]
    nh = NSA_KV_HEADS
    full = pl.BlockSpec((None, seq, LANES), lambda b, i: (b, 0, 0))
    head_tok = pl.BlockSpec((None, nh, seq, LANES), lambda b, i: (b, 0, 0, 0))
    head_feat = pl.BlockSpec((None, nh, LANES, seq), lambda b, i: (b, 0, 0, 0))
    tile = lambda w: pl.BlockSpec((None, tq, w), lambda b, i: (b, i, 0))
    cols = NSA_GROUP * tq
    return pl.pallas_call(
        functools.partial(_nsa_kernel, seq=seq),
        grid=(bsz, seq // tq),
        in_specs=[tile(qw),
                  pl.BlockSpec((None, n_rows, LANES), lambda b, i: (b, 0, 0)),
                  pl.BlockSpec((None, LANES, n_rows), lambda b, i: (b, 0, 0)),
                  head_tok, head_feat, full, head_feat, tile(LANES), _const_spec(ov_t.shape)],
        out_specs=tile(qw),
        out_shape=jax.ShapeDtypeStruct((bsz, seq, qw), BF16),
        scratch_shapes=[pltpu.VMEM((nh, 1, cols), F32),
                        pltpu.VMEM((nh, LANES, cols), F32),
                        pltpu.VMEM((nh, HEAD_DIM, cols), F32),
                        pltpu.VMEM((nh, cols, LANES), BF16),
                        pltpu.VMEM((nh, cols, LANES), BF16)],
        compiler_params=_cparams(("parallel", "arbitrary")),
        name="nsa_attention",
    )(q3, kc, vc_t, ks_aug, vs_aug, kw, vw_aug, gt, ov_t)


def _sb_kernel(q_ref, k_ref, vt_ref, o_ref, r_scr, acc_scr):
    tq = q_ref.shape[0]
    tk = SB_TILES * tq
    i_tile = pl.program_id(1)
    t0 = i_tile * tq
    q = q_ref[...]
    lane = lax.broadcasted_iota(jnp.int32, (1, LANES), 1)
    feat = lax.broadcasted_iota(jnp.int32, (LANES, 1), 0)
    t_row = t0 + lax.broadcasted_iota(jnp.int32, (1, tq), 1)
    kpos_col = lax.broadcasted_iota(jnp.int32, (tk, 1), 0)
    sr = lax.broadcasted_iota(jnp.int32, (tq, tq), 0)
    sc = lax.broadcasted_iota(jnp.int32, (tq, tq), 1)
    later = jnp.where(sc > sr, 1.0, 0.0).astype(BF16)
    qm = []
    for h in range(SB_HEADS):
        in_head = (lane >> _LOG_HEAD) == (h % 2)
        pair = q[:, (h // 2) * LANES:(h // 2 + 1) * LANES]
        qm.append(jnp.where(in_head, pair, jnp.zeros((), BF16)))
    r_scr[...] = jnp.zeros_like(r_scr)
    acc_scr[...] = jnp.zeros_like(acc_scr)

    def cond(c):
        j, r_max = c
        return (j >= 0) & (r_max > SB_EXIT)

    def body(c):
        j, _ = c
        k0 = pl.multiple_of(jnp.maximum(j + 1 - SB_TILES, 0) * tq, tq)
        kt = k_ref[pl.ds(k0, tk), :]
        z = jnp.concatenate([_dot_nt(kt[:, (h // 2) * LANES:(h // 2 + 1) * LANES], qm[h])
                             for h in range(SB_HEADS)], axis=1)
        kpos = k0 + kpos_col
        causal = jnp.where((kpos < t_row) & (kpos < (j + 1) * tq), 1.0, 0.0)
        causal = jnp.concatenate([causal] * SB_HEADS, axis=1)
        ls = jnp.minimum(z, 0.0) - jnp.log(1.0 + jnp.exp(-jnp.abs(z)))
        rest = (ls - z) * causal
        hi = rest.astype(BF16)
        lo = (rest - hi.astype(F32)).astype(BF16)
        r_prev = r_scr[...]
        totals = [jnp.sum(rest[a * tq:(a + 1) * tq], axis=0, keepdims=True) for a in range(SB_TILES)]
        after_tiles = []
        right = r_prev
        for a in reversed(range(SB_TILES)):
            rows = slice(a * tq, (a + 1) * tq)
            after_tiles.append(_dot(later, hi[rows]) + _dot(later, lo[rows]) + right)
            right = right + totals[a]
        after = jnp.concatenate(after_tiles[::-1], axis=0)
        w = (jnp.exp(ls + after) * causal).astype(BF16)
        for h in range(SB_HEADS):
            cols = slice(h * tq, (h + 1) * tq)
            vt = vt_ref[(h // 2) * LANES:(h // 2 + 1) * LANES, pl.ds(k0, tk)]
            acc_scr[:, cols] = acc_scr[:, cols] + _dot(vt, w[:, cols])
        r_new = right
        r_scr[...] = r_new
        return j - SB_TILES, jnp.max(r_new)

    lax.while_loop(cond, body, (i_tile, jnp.zeros((), F32)))
    for hp in range(SB_HEADS // 2):
        even = acc_scr[:, (2 * hp) * tq:(2 * hp + 1) * tq]
        odd = acc_scr[:, (2 * hp + 1) * tq:(2 * hp + 2) * tq]
        o_ref[:, hp * LANES:(hp + 1) * LANES] = jnp.where(feat < HEAD_DIM, even, odd).T.astype(o_ref.dtype)


def _sb(q3, k3, v_t):
    bsz, seq, w = q3.shape
    tq = SB_TQ
    full = pl.BlockSpec((None, seq, w), lambda b, i: (b, 0, 0))
    full_t = pl.BlockSpec((None, w, seq), lambda b, i: (b, 0, 0))
    tile = pl.BlockSpec((None, tq, w), lambda b, i: (b, i, 0))
    return pl.pallas_call(
        _sb_kernel,
        grid=(bsz, seq // tq),
        in_specs=[tile, full, full_t],
        out_specs=tile,
        out_shape=jax.ShapeDtypeStruct((bsz, seq, w), BF16),
        scratch_shapes=[pltpu.VMEM((1, SB_HEADS * tq), F32), pltpu.VMEM((LANES, SB_HEADS * tq), F32)],
        compiler_params=_cparams(("parallel", "arbitrary")),
        name="sb_attention",
    )(q3, k3, v_t)


def _merge_kernel(x_ref, g_ref, a_ref, b_ref, c_ref, wg_ref, wb_ref, wo_ref, o_ref):
    x = x_ref[...]
    d = x.shape[1]
    h = _rms(x, g_ref[...]).astype(BF16)
    mixed = None
    for n, br in enumerate((a_ref, b_ref, c_ref)):
        gate = jax.nn.sigmoid(_dot(h, wg_ref[:, n * d:(n + 1) * d]))
        y = gate * _dot(br[...], wb_ref[n])
        mixed = y if mixed is None else mixed + y
    o_ref[...] = x + _dot(mixed.astype(BF16), wo_ref[...])


def _merge(x2, layer, g, o_a, o_b, o_c, w_gate, w_branch, w_out):
    t, d = x2.shape
    tm = min(TM_PROJ, t)
    row = lambda i: (i, 0)
    bw = o_a.shape[1]
    return pl.pallas_call(
        _merge_kernel,
        grid=(t // tm,),
        in_specs=[pl.BlockSpec((tm, d), row), _layer_spec(g, layer),
                  pl.BlockSpec((tm, bw), row), pl.BlockSpec((tm, bw), row), pl.BlockSpec((tm, bw), row),
                  _layer_spec(w_gate, layer), _layer_spec(w_branch, layer), _layer_spec(w_out, layer)],
        out_specs=pl.BlockSpec((tm, d), row),
        out_shape=jax.ShapeDtypeStruct((t, d), F32),
        compiler_params=_cparams(("parallel",)),
        name="gated_merge",
    )(x2, g, o_a, o_b, o_c, w_gate, w_branch, w_out)


def _ffn_kernel(x_ref, halo_ref, g_ref, wup_ref, cw_ref, cb_ref, wdn_ref, gout_ref, o_ref, act_scr, *,
                tiles_per_seq, norm_out):
    x = x_ref[...]
    tm = x.shape[0]
    d_ff = wdn_ref.shape[0]
    at_start = (pl.program_id(0) % tiles_per_seq) == 0
    g = g_ref[...]
    h_halo = jnp.where(at_start, 0.0, _rms(halo_ref[...], g))
    h = jnp.concatenate([h_halo, _rms(x, g)], axis=0).astype(BF16)
    for f in range(d_ff // FF_TILE):
        halves = []
        for base in (0, d_ff):
            cols = slice(base + f * FF_TILE, base + (f + 1) * FF_TILE)
            u = _dot(h, wup_ref[:, cols])
            w = cw_ref[:, cols]
            conv = (w[2:3] * u[SUBLANES:]
                    + w[1:2] * pltpu.roll(u, 1, 0)[SUBLANES:]
                    + w[0:1] * pltpu.roll(u, 2, 0)[SUBLANES:]
                    + cb_ref[:, cols])
            halves.append(conv)
        act_scr[:, f * FF_TILE:(f + 1) * FF_TILE] = (jax.nn.silu(halves[0]) * halves[1]).astype(BF16)
    y = x + _dot(act_scr[...], wdn_ref[...])
    o_ref[...] = _rms(y, gout_ref[...]) if norm_out else y


def _ffn(x2, layer, g, w_up, conv_w, conv_b, w_down, g_out, seq, norm_out):
    t, d = x2.shape
    tm = min(TM_FFN, seq)
    tiles_per_seq = seq // tm
    row = lambda i: (i, 0)
    halo = lambda i: (jnp.maximum(i * (tm // SUBLANES) - 1, 0), 0)
    return pl.pallas_call(
        functools.partial(_ffn_kernel, tiles_per_seq=tiles_per_seq, norm_out=norm_out),
        grid=(t // tm,),
        in_specs=[pl.BlockSpec((tm, d), row), pl.BlockSpec((SUBLANES, d), halo), _layer_spec(g, layer),
                  _layer_spec(w_up, layer), _layer_spec(conv_w, layer), _layer_spec(conv_b, layer),
                  _layer_spec(w_down, layer), _const_spec(g_out.shape)],
        out_specs=pl.BlockSpec((tm, d), row),
        out_shape=jax.ShapeDtypeStruct((t, d), F32),
        scratch_shapes=[pltpu.VMEM((tm, w_down.shape[1]), BF16)],
        compiler_params=_cparams(("parallel",)),
        name="conv_ffn",
    )(x2, x2, g, w_up, conv_w, conv_b, w_down, g_out)


def _norm_kernel(x_ref, g_ref, o_ref):
    o_ref[...] = _rms(x_ref[...], g_ref[...])


def _final_norm(x2, g):
    t, d = x2.shape
    tm = min(TM_PROJ, t)
    return pl.pallas_call(
        _norm_kernel,
        grid=(t // tm,),
        in_specs=[pl.BlockSpec((tm, d), lambda i: (i, 0)), _const_spec((1, d))],
        out_specs=pl.BlockSpec((tm, d), lambda i: (i, 0)),
        out_shape=jax.ShapeDtypeStruct((t, d), F32),
        compiler_params=_cparams(("parallel",)),
        name="final_norm",
    )(x2, g)


def _nsa_slot_order():
    return [NSA_GROUP * (s % 2) + s // 2 for s in range(NSA_HEADS)]


def _pack_w_in(w):
    d = w.shape[0]
    o_q, o_kv, o_g, o_sb, o_gm = 512, 1024, 1792, 1816, 3352
    u = w[:, 0:o_q]
    q = w[:, o_q:o_kv].reshape(d, NSA_HEADS, HEAD_DIM)[:, jnp.array(_nsa_slot_order())].reshape(d, -1)
    kv = w[:, o_kv:o_g].reshape(d, 3, 2, NSA_KV_HEADS * HEAD_DIM)
    k = kv[:, :, 0].reshape(d, -1)
    v = kv[:, :, 1]
    gates = jnp.pad(w[:, o_g:o_sb], ((0, 0), (0, LANES - (o_sb - o_g))))
    sb = w[:, o_sb:o_gm]
    sb_w = SB_HEADS * HEAD_DIM
    packed = jnp.concatenate([u, q, k, v[:, 0], gates, sb[:, :2 * sb_w]], axis=1).astype(BF16)
    w_vt = jnp.concatenate([v[:, 1], v[:, 2], sb[:, 2 * sb_w:]], axis=1).T.astype(BF16)
    return packed, w_vt, w[:, o_gm:].astype(BF16)


def _rope_tables(seq):
    inv_freq = 1.0 / (ROPE_THETA ** (jnp.arange(0, HEAD_DIM, 2, dtype=F32) / HEAD_DIM))
    ang = jnp.arange(seq, dtype=F32)[:, None] * inv_freq[None, :]
    cos, sin = jnp.cos(ang), jnp.sin(ang)
    cos_t = jnp.tile(cos, (1, LANES // (HEAD_DIM // 2)))
    sin_t = jnp.tile(jnp.concatenate([-sin, sin], axis=1), (1, LANES // HEAD_DIM))
    return cos_t, sin_t


def _overlap_t(seq):
    n_rows = seq // CMP_STRIDE
    n_sel = seq // SEL_LEN
    cmp_start = jnp.arange(n_rows) * CMP_STRIDE
    sel_start = jnp.arange(n_sel) * SEL_LEN
    ov = ((cmp_start[None, :] < sel_start[:, None] + SEL_LEN)
          & (cmp_start[None, :] + CMP_LEN > sel_start[:, None])
          & (jnp.arange(n_rows)[None, :] < (seq - CMP_LEN) // CMP_STRIDE + 1))
    return ov.astype(BF16)


def kernel(x, norm_mix, w_in, ssm_lam_re, ssm_lam_im, ssm_log_step, ssm_b_re, ssm_b_im, ssm_c_re, ssm_c_im, ssm_d, ssm_w_glu, cmp_w1_k, cmp_w2_k, cmp_pos_k, cmp_w1_v, cmp_w2_v, cmp_pos_v, w_branch, w_out, norm_ffn, ffn_w_up, ffn_conv_w, ffn_conv_b, ffn_w_down, norm_final):
    bsz, seq, d = x.shape
    depth = w_in.shape[0]
    t = bsz * seq
    assert seq % 512 == 0 and seq // SEL_LEN <= HEAD_DIM
    assert seq >= WINDOW + NSA_TQ
    if depth == 0:
        return _final_norm(x.reshape(t, d).astype(F32), norm_final.reshape(1, d)).reshape(x.shape).astype(x.dtype)
    cos_t, sin_t = _rope_tables(seq)
    ov_t = _overlap_t(seq)
    w_packed, w_vt, w_gate = jax.vmap(_pack_w_in)(w_in)
    s5_tabs = jax.vmap(_s5_tables)(ssm_lam_re, ssm_lam_im, ssm_log_step, ssm_b_re, ssm_b_im, ssm_c_re, ssm_c_im)
    s5_d = ssm_d.reshape(depth, 1, -1).astype(F32)
    s5_glu = ssm_w_glu.astype(BF16)
    cmp_k = jax.vmap(_compress_weights)(cmp_w1_k, cmp_w2_k, cmp_pos_k)
    cmp_v = jax.vmap(functools.partial(_compress_weights, feature_major_out=True))(cmp_w1_v, cmp_w2_v, cmp_pos_v)
    slot_rows = jnp.array(_nsa_slot_order())
    wb_nsa = w_branch[:, 1].reshape(depth, NSA_HEADS, HEAD_DIM, d)[:, slot_rows].reshape(depth, BRANCH_WIDTH, d)
    wb_all = jnp.stack([w_branch[:, 0], wb_nsa, w_branch[:, 2]], axis=1).astype(BF16)
    wo_all = w_out.astype(BF16)
    g_mix = norm_mix.reshape(depth, 1, d).astype(F32)
    g_ffn = norm_ffn.reshape(depth, 1, d).astype(F32)
    up_all = ffn_w_up.astype(BF16)
    down_all = ffn_w_down.astype(BF16)
    conv_w = ffn_conv_w.astype(F32)
    conv_b = ffn_conv_b.reshape(depth, 1, -1).astype(F32)
    g_final = norm_final.reshape(1, d).astype(F32)

    x2 = x.reshape(t, d).astype(F32)
    b3 = lambda a: a.reshape(bsz, seq, a.shape[-1])
    rows16 = lambda a: a.reshape(bsz, seq // CMP_STRIDE, a.shape[-1])
    for l in range(depth):
        (u, q, kc_raw, ks_aug, kw, vc_raw, vs_aug, vw_aug, gt, sq, sk, sv_t) = _proj_in(
            x2, l, g_mix, w_packed, w_vt, cos_t, sin_t, bsz, seq)
        o_ssm = _s5(b3(u), l, s5_tabs, s5_d, s5_glu)
        kc, vc_t = _compress(rows16(kc_raw), rows16(vc_raw), l, cmp_k, cmp_v)
        o_nsa = _nsa(b3(q), kc, vc_t, ks_aug, vs_aug, b3(kw), vw_aug, b3(gt), ov_t)
        o_sb = _sb(b3(sq), b3(sk), sv_t)
        x2 = _merge(x2, l, g_mix, o_ssm.reshape(t, -1), o_nsa.reshape(t, -1), o_sb.reshape(t, -1),
                    w_gate, wb_all, wo_all)
        x2 = _ffn(x2, l, g_ffn, up_all, conv_w, conv_b, down_all, g_final, seq, norm_out=(l == depth - 1))
    return x2.reshape(bsz, seq, d).astype(x.dtype)
```

```python
import functools
import math

import jax
import jax.numpy as jnp
from jax import lax
from jax.experimental import pallas as pl
from jax.experimental.pallas import tpu as pltpu

F32 = jnp.float32
BF16 = jnp.bfloat16

HEAD_DIM = 64
RMS_EPS = 1e-6
ROPE_THETA = 10000.0
SSM_GROUP = 16
SSM_STATE = 64
NSA_HEADS = 8
NSA_KV_HEADS = 2
NSA_GROUP = NSA_HEADS // NSA_KV_HEADS
CMP_LEN = 32
CMP_STRIDE = 16
CMP_HIDDEN = 128
SEL_LEN = 64
SEL_TOPK = 16
WINDOW = 512
FORCE_BONUS = 1e6
SB_HEADS = 8
CONV_WIDTH = 3
BRANCH_WIDTH = 512

LANES = 128
SUBLANES = 8
VMEM_LIMIT = 56 * 1024 * 1024

TM_PROJ = 512
TM_FFN = 512
FF_TILE = 256
S5_CHUNK = 128
S5_LANES = 256
S5_MXU_SHIFTS = 4
NSA_TQ = 256
NSA_TK_SEL = 512
SB_TQ = 128
SB_TILES = 3

_LOG_HEAD = HEAD_DIM.bit_length() - 1
_LOG_SEL = SEL_LEN.bit_length() - 1
NEG = -1e30
SB_EXIT = -104.0


def _cparams(sem):
    return pltpu.CompilerParams(dimension_semantics=sem, vmem_limit_bytes=VMEM_LIMIT)


def _const_spec(shape):
    n = len(shape)
    return pl.BlockSpec(shape, lambda *_: (0,) * n, pipeline_mode=pl.Buffered(1))


def _layer_spec(stacked, layer):
    n = stacked.ndim - 1
    return pl.BlockSpec((None,) + stacked.shape[1:], lambda *_: (layer,) + (0,) * n,
                        pipeline_mode=pl.Buffered(1))


def _rms(x, g):
    return x * lax.rsqrt(jnp.mean(x * x, axis=-1, keepdims=True) + RMS_EPS) * g


def _dot(a, b):
    return jnp.dot(a, b, preferred_element_type=F32)


def _dot_nt(a, b):
    return lax.dot_general(a, b, (((1,), (1,)), ((), ())), preferred_element_type=F32)


def _split3(x):
    h = x.astype(BF16)
    r = x - h.astype(F32)
    m = r.astype(BF16)
    lo = (r - m.astype(F32)).astype(BF16)
    return h, m, lo


_C_U = 0
_C_ROPE = 512
_C_V = 1408
_C_G = 1536
_C_SB = 1664
_C_END = 2688
_R_VS, _R_VW, _R_SV, _R_END = 0, 128, 256, 768


def _proj_kernel(x_ref, g_ref, w_ref, wvt_ref, cos_ref, sin_ref,
                 u_ref, q_ref, kc_ref, ksa_ref, kw_ref, vc_ref, vsa_ref, vwa_ref, gt_ref,
                 sq_ref, sk_ref, svt_ref, rows_scr, *, n_pos):
    tm = x_ref.shape[0]
    h = _rms(x_ref[...], g_ref[...]).astype(BF16)

    def store_rows16(val, out_ref):
        rows_scr[...] = val
        for b in range(CMP_STRIDE):
            out_ref[:, b * LANES:(b + 1) * LANES] = rows_scr[pl.ds(b, tm // CMP_STRIDE, stride=CMP_STRIDE), :]

    def mm(a, b):
        return _dot(h, w_ref[:, a:b])

    vt = _dot_nt(wvt_ref[...], h)
    ones_rows = jnp.where(lax.broadcasted_iota(jnp.int32, (HEAD_DIM, tm), 0) == 0, 1.0, 0.0)
    for hh in range(NSA_KV_HEADS):
        vsa_ref[hh] = jnp.concatenate(
            [vt[_R_VS + hh * HEAD_DIM:_R_VS + (hh + 1) * HEAD_DIM], ones_rows], axis=0).astype(BF16)
        vwa_ref[hh] = jnp.concatenate(
            [vt[_R_VW + hh * HEAD_DIM:_R_VW + (hh + 1) * HEAD_DIM], ones_rows], axis=0).astype(BF16)
    svt_ref[...] = vt[_R_SV:_R_END].astype(BF16)

    u_ref[...] = mm(_C_U, _C_ROPE)

    r = mm(_C_ROPE, _C_V)
    cos = cos_ref[...]
    sin = sin_ref[...]
    lane = lax.broadcasted_iota(jnp.int32, cos.shape, 1)
    first = (lane & (HEAD_DIM - 1)) < (HEAD_DIM // 2)
    roped = []
    for c in range((_C_V - _C_ROPE) // LANES):
        rc = r[:, c * LANES:(c + 1) * LANES]
        partner = jnp.where(first, pltpu.roll(rc, LANES - HEAD_DIM // 2, 1),
                            pltpu.roll(rc, HEAD_DIM // 2, 1))
        roped.append(rc * cos + partner * sin)
    scale = HEAD_DIM ** -0.5
    for c in range(4):
        q_ref[:, c * LANES:(c + 1) * LANES] = (roped[c] * scale).astype(BF16)
    store_rows16(roped[4], kc_ref)
    kw_ref[...] = roped[6].astype(BF16)
    pos = (pl.program_id(0) % n_pos) * tm + lax.broadcasted_iota(jnp.int32, (tm, 1), 0)
    blk = pos >> _LOG_SEL
    lane_k = lax.broadcasted_iota(jnp.int32, (1, LANES), 1)
    for hh in range(NSA_KV_HEADS):
        own = (lane_k >> _LOG_HEAD) == hh
        one_hot = jnp.where(blk == (lane_k & (HEAD_DIM - 1)), 1.0, 0.0)
        ksa_ref[hh] = jnp.where(own, roped[5], one_hot).astype(BF16)

    store_rows16(mm(_C_V, _C_G), vc_ref)
    gt_ref[...] = jax.nn.sigmoid(mm(_C_G, _C_SB))

    sb = mm(_C_SB, _C_END)
    sq_ref[...] = (sb[:, 0:512] * scale).astype(BF16)
    sk_ref[...] = sb[:, 512:1024].astype(BF16)


def _proj_in(x2, layer, g, w_packed, w_vt, cos_t, sin_t, bsz, seq):
    t, d = x2.shape
    tm = min(TM_PROJ, seq)
    n_pos = seq // tm
    row = lambda i: (i, 0)
    pos = lambda i: (i % n_pos, 0)
    nh = NSA_KV_HEADS

    def tok(w, dt):
        return pl.BlockSpec((tm, w), row), jax.ShapeDtypeStruct((t, w), dt)

    def feat_major(w, dt):
        return (pl.BlockSpec((None, w, tm), lambda i: (i // n_pos, 0, i % n_pos)),
                jax.ShapeDtypeStruct((bsz, w, seq), dt))

    def head_tok(dt):
        return (pl.BlockSpec((None, nh, tm, LANES), lambda i: (i // n_pos, 0, i % n_pos, 0)),
                jax.ShapeDtypeStruct((bsz, nh, seq, LANES), dt))

    def head_feat(dt):
        return (pl.BlockSpec((None, nh, LANES, tm), lambda i: (i // n_pos, 0, 0, i % n_pos)),
                jax.ShapeDtypeStruct((bsz, nh, LANES, seq), dt))

    def rows16():
        return (pl.BlockSpec((tm // CMP_STRIDE, CMP_STRIDE * LANES), row),
                jax.ShapeDtypeStruct((t // CMP_STRIDE, CMP_STRIDE * LANES), F32))

    outs = [tok(512, F32), tok(512, BF16), rows16(), head_tok(BF16), tok(128, BF16),
            rows16(), head_feat(BF16), head_feat(BF16), tok(128, F32),
            tok(512, BF16), tok(512, BF16), feat_major(512, BF16)]
    return pl.pallas_call(
        functools.partial(_proj_kernel, n_pos=n_pos),
        grid=(t // tm,),
        in_specs=[pl.BlockSpec((tm, d), row),
                  _layer_spec(g, layer),
                  _layer_spec(w_packed, layer),
                  _layer_spec(w_vt, layer),
                  pl.BlockSpec((tm, LANES), pos),
                  pl.BlockSpec((tm, LANES), pos)],
        out_specs=[spec for spec, _ in outs],
        out_shape=[shape for _, shape in outs],
        scratch_shapes=[pltpu.VMEM((tm, LANES), F32)],
        compiler_params=_cparams(("parallel",)),
        name="proj_in",
    )(x2, g, w_packed, w_vt, cos_t, sin_t)


def _s5_kernel(u_ref, b8_ref, ccat_ref, pw_re_ref, pw_im_ref, cy_re_ref, cy_im_ref, d_ref, wglu_ref, o_ref,
               car_re, car_im):
    @pl.when(pl.program_id(1) == 0)
    def _():
        car_re[...] = jnp.zeros_like(car_re)
        car_im[...] = jnp.zeros_like(car_im)

    u = u_ref[...]
    chunk, width = u.shape
    n_tiles = b8_ref.shape[0]
    n_sub = chunk // SUBLANES
    u3 = u.reshape(n_sub, SUBLANES, width)
    row8 = lax.broadcasted_iota(jnp.int32, (1, SUBLANES, 1), 1)
    shifted = [u.astype(BF16)]
    for s in range(1, S5_MXU_SHIFTS):
        shifted.append(jnp.where(row8 >= s, pltpu.roll(u3, s, 1), 0.0).reshape(chunk, width).astype(BF16))
    ucat = [jnp.concatenate([sh[:, b * LANES:(b + 1) * LANES] for sh in shifted], axis=1)
            for b in range(width // LANES)]
    y_blocks = []
    for blk in range(n_tiles // 2):
        pair = (2 * blk, 2 * blk + 1)
        hr, hi = {}, {}
        for j in pair:
            x = _dot(ucat[blk], b8_ref[j])
            xr = x[:, :S5_LANES].reshape(n_sub, SUBLANES, S5_LANES)
            xi = x[:, S5_LANES:].reshape(n_sub, SUBLANES, S5_LANES)
            sr = pltpu.roll(xr, S5_MXU_SHIFTS, 1)
            si = pltpu.roll(xi, S5_MXU_SHIFTS, 1)
            cr = pw_re_ref[j]
            ci = pw_im_ref[j]
            hr[j], hi[j] = xr + cr * sr - ci * si, xi + cr * si + ci * sr
        c_re = {j: car_re[j:j + 1, :] for j in pair}
        c_im = {j: car_im[j:j + 1, :] for j in pair}
        out_re = {j: [] for j in pair}
        out_im = {j: [] for j in pair}
        for v in range(n_sub):
            for j in pair:
                pr = cy_re_ref[j]
                pi = cy_im_ref[j]
                t_re = hr[j][v] + pr * c_re[j] - pi * c_im[j]
                t_im = hi[j][v] + pr * c_im[j] + pi * c_re[j]
                c_re[j] = t_re[SUBLANES - 1:SUBLANES, :]
                c_im[j] = t_im[SUBLANES - 1:SUBLANES, :]
                out_re[j].append(t_re)
                out_im[j].append(t_im)
        y_blk = None
        for j in pair:
            car_re[j:j + 1, :] = c_re[j]
            car_im[j:j + 1, :] = c_im[j]
            hcat = jnp.concatenate([jnp.concatenate(out_re[j], axis=0), jnp.concatenate(out_im[j], axis=0)],
                                   axis=1).astype(BF16)
            yj = _dot(hcat, ccat_ref[j])
            y_blk = yj if y_blk is None else y_blk + yj
        y_blocks.append(y_blk)
    y = jnp.concatenate(y_blocks, axis=1) + d_ref[...] * u
    z = jax.nn.gelu(y)
    o_ref[...] = (z * jax.nn.sigmoid(_dot(z.astype(BF16), wglu_ref[...]))).astype(o_ref.dtype)


def _s5_tables(lam_re, lam_im, log_step, b_re, b_im, c_re, c_im):
    g_n, p_n = lam_re.shape
    i_n = b_re.shape[-1]
    width = g_n * i_n
    lanes_total = g_n * p_n
    n_tiles = lanes_total // S5_LANES
    step = jnp.exp(log_step.astype(F32))[:, None]
    lr = lam_re.astype(F32)
    li = lam_im.astype(F32)
    mag = jnp.exp(lr * step)
    a_re = mag * jnp.cos(li * step)
    a_im = mag * jnp.sin(li * step)
    den = lr * lr + li * li
    nr = a_re - 1.0
    f_re = (nr * lr + a_im * li) / den
    f_im = (a_im * lr - nr * li) / den
    br = b_re.astype(F32)
    bi = b_im.astype(F32)
    bb_re = f_re[..., None] * br - f_im[..., None] * bi
    bb_im = f_re[..., None] * bi + f_im[..., None] * br
    log_mag = (lr * step).reshape(1, lanes_total)
    ang = (li * step).reshape(1, lanes_total)

    def powers(n):
        m = jnp.exp(n * log_mag)
        return m * jnp.cos(n * ang), m * jnp.sin(n * ang)

    gpt = S5_LANES // p_n
    gpb = LANES // i_n
    tile_id = jnp.arange(n_tiles)
    in_tile = (jnp.arange(gpb)[None, :, None]
               == (tile_id[:, None, None] % (gpb // gpt)) * gpt + jnp.arange(gpt)[None, None, :]).astype(F32)

    n_sh = S5_MXU_SHIFTS
    as_re, as_im = powers(jnp.arange(n_sh, dtype=F32)[:, None])
    as_re = as_re.reshape(n_sh, g_n, p_n, 1)
    as_im = as_im.reshape(n_sh, g_n, p_n, 1)
    w_re = as_re * bb_re[None] - as_im * bb_im[None]
    w_im = as_re * bb_im[None] + as_im * bb_re[None]

    def in_tiles(w):
        w = w.reshape(n_sh, n_tiles, gpt, p_n, i_n).transpose(1, 0, 4, 2, 3)
        out = in_tile[:, None, :, None, :, None] * w[:, :, None, :, :, :]
        return out.reshape(n_tiles, n_sh * LANES, S5_LANES)

    b8 = jnp.concatenate([in_tiles(w_re), in_tiles(w_im)], axis=2).astype(BF16)

    def out_tiles(c):
        c = c.astype(F32).reshape(n_tiles, gpt, i_n, p_n).transpose(0, 1, 3, 2)
        out = in_tile.transpose(0, 2, 1)[:, :, None, :, None] * c[:, :, :, None, :]
        return out.reshape(n_tiles, S5_LANES, LANES)

    ccat = jnp.concatenate([out_tiles(c_re), -out_tiles(c_im)], axis=1).astype(BF16)

    cy_re, cy_im = powers(jnp.arange(1, SUBLANES + 1, dtype=F32)[:, None])

    def tiles(a):
        return a.reshape(a.shape[0], n_tiles, S5_LANES).transpose(1, 0, 2)

    live = (jnp.arange(SUBLANES) >= n_sh).astype(F32)[None, :, None]
    pw_re, pw_im = powers(jnp.full((1, 1), float(n_sh), F32))
    return b8, ccat, tiles(pw_re) * live, tiles(pw_im) * live, tiles(cy_re), tiles(cy_im)


def _s5(u3, layer, tables, d_row, wglu):
    bsz, seq, width = u3.shape
    consts = tuple(tables) + (d_row, wglu)
    chunk = min(S5_CHUNK, seq)
    n_tiles = tables[0].shape[1]
    return pl.pallas_call(
        _s5_kernel,
        grid=(bsz, seq // chunk),
        in_specs=[pl.BlockSpec((None, chunk, width), lambda b, c: (b, c, 0))]
                 + [_layer_spec(a, layer) for a in consts],
        out_specs=pl.BlockSpec((None, chunk, width), lambda b, c: (b, c, 0)),
        out_shape=jax.ShapeDtypeStruct((bsz, seq, width), BF16),
        scratch_shapes=[pltpu.VMEM((n_tiles, S5_LANES), F32), pltpu.VMEM((n_tiles, S5_LANES), F32)],
        compiler_params=_cparams(("parallel", "arbitrary")),
        name="s5_scan",
    )(u3, *consts)


def _compress_kernel(k_ref, v_ref, w1a_k, w1b_k, pa_k, pb_k, w2_k, w1a_v, w1b_v, pa_v, pb_v, w2_v,
                     kc_ref, vc_ref):
    def hidden(t_ref, w1a, w1b, pa, pb):
        r = t_ref[...]
        n = r.shape[0]
        first = _dot((r + pa[...]).astype(BF16), w1a[...])
        second = _dot((r + pb[...]).astype(BF16), w1b[...])
        rowi = lax.broadcasted_iota(jnp.int32, second.shape, 0)
        nxt = jnp.where(rowi < n - 1, pltpu.roll(second, n - 1, 0), 0.0)
        return jax.nn.gelu(first + nxt).astype(BF16)

    kc_ref[...] = _dot(hidden(k_ref, w1a_k, w1b_k, pa_k, pb_k), w2_k[...]).astype(kc_ref.dtype)
    vc_ref[...] = _dot_nt(w2_v[...], hidden(v_ref, w1a_v, w1b_v, pa_v, pb_v)).astype(vc_ref.dtype)


def _compress_weights(w1, w2, pos, feature_major_out=False):
    half = CMP_LEN // 2
    eye = jnp.eye(NSA_KV_HEADS, dtype=F32)

    def first_layer(w):
        m = w[:, None, :, None, :] * eye[None, :, None, :, None]
        return m.reshape(half * NSA_KV_HEADS * HEAD_DIM, NSA_KV_HEADS * CMP_HIDDEN).astype(BF16)

    def pos_row(p):
        return jnp.broadcast_to(p[:, None, :], (half, NSA_KV_HEADS, HEAD_DIM)).reshape(1, -1).astype(F32)

    w2_bd = (eye[:, None, :, None] * w2[None, :, None, :]).reshape(
        NSA_KV_HEADS * CMP_HIDDEN, NSA_KV_HEADS * HEAD_DIM).astype(BF16)
    if feature_major_out:
        w2_bd = w2_bd.T
    return (first_layer(w1[:half]), first_layer(w1[half:]), pos_row(pos[:half]), pos_row(pos[half:]), w2_bd)


def _compress(kc_raw, vc_raw, layer, wk, wv):
    bsz, n_rows, feat = kc_raw.shape
    consts = list(wk) + list(wv)
    blk = pl.BlockSpec((None, n_rows, feat), lambda b: (b, 0, 0))
    return pl.pallas_call(
        _compress_kernel,
        grid=(bsz,),
        in_specs=[blk, blk] + [_layer_spec(c, layer) for c in consts],
        out_specs=[pl.BlockSpec((None, n_rows, LANES), lambda b: (b, 0, 0)),
                   pl.BlockSpec((None, LANES, n_rows), lambda b: (b, 0, 0))],
        out_shape=[jax.ShapeDtypeStruct((bsz, n_rows, LANES), BF16),
                   jax.ShapeDtypeStruct((bsz, LANES, n_rows), BF16)],
        compiler_params=_cparams(("parallel",)),
        name="nsa_compress",
    )(kc_raw, vc_raw, *consts)


def _nsa_kernel(q_ref, kc_ref, vct_ref, ksa_ref, vsa_ref, kw_ref, vwa_ref, gt_ref, ovt_ref, o_ref,
                m_scr, acc_scr, out_scr, q4_scr, q4s_scr, cnt_scr, *, seq):
    tq = q_ref.shape[0]
    n_cmp_rows = kc_ref.shape[0]
    n_sel = seq // SEL_LEN
    i_tile = pl.program_id(1)
    t0 = i_tile * tq
    heads = range(NSA_KV_HEADS)

    q = q_ref[...]
    gt_t = gt_ref[...].T
    lane = lax.broadcasted_iota(jnp.int32, (1, LANES), 1)
    t_row = t0 + lax.broadcasted_iota(jnp.int32, (1, tq), 1)

    def tile_lanes(a):
        return jnp.concatenate([a] * NSA_GROUP, axis=1)

    def init_state():
        m_scr[...] = jnp.full_like(m_scr, NEG)
        acc_scr[...] = jnp.zeros_like(acc_scr)

    def online_tile(kts, q_scr, vts, bias):
        scores = [_dot_nt(kts[h], q_scr[h]) for h in heads]
        for h in heads:
            s = scores[h]
            if bias is not None:
                s = s + tile_lanes(bias)
            m_prev = m_scr[h]
            m_next = jnp.maximum(m_prev, jnp.max(s, axis=0, keepdims=True))
            alpha = jnp.exp(m_prev - m_next)
            p = jnp.exp(s - m_next).astype(BF16)
            acc_scr[h] = alpha * acc_scr[h] + _dot(vts[h], p)
            m_scr[h] = m_next

    def finish(h):
        acc = acc_scr[h]
        return acc[0:HEAD_DIM] * (1.0 / acc[HEAD_DIM:HEAD_DIM + 1])

    def gate_row(h, br):
        return jnp.concatenate([gt_t[(h * NSA_GROUP + g) * 3 + br:(h * NSA_GROUP + g) * 3 + br + 1, :]
                                for g in range(NSA_GROUP)], axis=1)

    for h in heads:
        in_head = (lane >> _LOG_HEAD) == h
        q4 = jnp.concatenate(
            [jnp.where(in_head, q[:, g * LANES:(g + 1) * LANES], jnp.zeros((), BF16)) for g in range(NSA_GROUP)],
            axis=0)
        q4_scr[h] = q4

        s = _dot_nt(kc_ref[...], q4)
        n_col = lax.broadcasted_iota(jnp.int32, (n_cmp_rows, 1), 0)
        vis = (n_col * CMP_STRIDE + (CMP_LEN - 1)) <= t_row
        s = s + tile_lanes(jnp.where(vis, 0.0, NEG))
        m = jnp.max(s, axis=0, keepdims=True)
        e = jnp.exp(s - m)
        any_vis = tile_lanes(jnp.where(t_row >= CMP_LEN - 1, 1.0, 0.0))
        p = e * (any_vis / jnp.maximum(jnp.sum(e, axis=0, keepdims=True), 1e-30))
        o_cmp = _dot(vct_ref[...], p.astype(BF16))
        out_scr[h] = gate_row(h, 0) * o_cmp[h * HEAD_DIM:(h + 1) * HEAD_DIM]

        p_sum = p[:, 0:tq]
        for g in range(1, NSA_GROUP):
            p_sum = p_sum + p[:, g * tq:(g + 1) * tq]
        ov_t = ovt_ref[...]
        p_sel_t = sum(_dot(ov_t, part) for part in _split3(p_sum))
        blk = lax.broadcasted_iota(jnp.int32, (n_sel, 1), 0)
        t_blk = t_row >> _LOG_SEL
        valid = blk <= t_blk
        forced = (blk == 0) | (blk == t_blk) | (blk == t_blk - 1)
        rank_t = jnp.where(valid, p_sel_t + FORCE_BONUS * jnp.where(forced, 1.0, 0.0), -FORCE_BONUS)
        n_tiles = n_sel // SUBLANES
        rank_tiles = [rank_t[r * SUBLANES:(r + 1) * SUBLANES] for r in range(n_tiles)]
        blk8 = lax.broadcasted_iota(jnp.int32, (SUBLANES, 1), 0)
        cnt_scr[...] = jnp.zeros_like(cnt_scr)
        last_blk = (t0 + tq - 1) >> _LOG_SEL

        def count_group(grp):
            part = [jnp.zeros((SUBLANES, tq), F32) for _ in range(n_tiles)]
            for ii in range(grp * SUBLANES, (grp + 1) * SUBLANES):
                ri = rank_t[ii:ii + 1, :]
                for r in range(n_tiles):
                    if ii < r * SUBLANES:
                        ahead = ri >= rank_tiles[r]
                    elif ii >= (r + 1) * SUBLANES:
                        ahead = ri > rank_tiles[r]
                    else:
                        tie_ok = jnp.where(blk8 + r * SUBLANES > ii, 1.0, 0.0)
                        ahead = (ri > rank_tiles[r]) | ((ri == rank_tiles[r]) & (tie_ok > 0.5))
                    part[r] = part[r] + jnp.where(ahead, 1.0, 0.0)
            for r in range(n_tiles):
                rows = slice(r * SUBLANES, (r + 1) * SUBLANES)
                cnt_scr[rows, :] = cnt_scr[rows, :] + part[r]

        for grp in range(n_tiles):
            pl.when(grp * SUBLANES <= last_blk)(functools.partial(count_group, grp))
        top = float(min(SEL_TOPK, n_sel))
        sel_bias = jnp.concatenate(
            [jnp.where(cnt_scr[...] < top, 0.0, NEG), jnp.zeros((LANES - n_sel, tq), F32)], axis=0)
        mask_lanes = sel_bias.T
        if h == 0:
            mask_lanes = pltpu.roll(mask_lanes, HEAD_DIM, 1)
        mask_lanes = mask_lanes.astype(BF16)
        q4s_scr[h] = jnp.concatenate(
            [jnp.where(in_head, q[:, g * LANES:(g + 1) * LANES], mask_lanes) for g in range(NSA_GROUP)], axis=0)

    init_state()
    tk = NSA_TK_SEL
    n_full = t0 // tk

    def sel_tile(k0, size, bias):
        online_tile([ksa_ref[h, pl.ds(k0, size), :] for h in heads], q4s_scr,
                    [vsa_ref[h, :, pl.ds(k0, size)] for h in heads], bias)

    def sel_body(j, _):
        sel_tile(pl.multiple_of(j * 2 * tk, 2 * tk), 2 * tk, None)
        return 0

    lax.fori_loop(0, n_full // 2, sel_body, 0)

    @pl.when(n_full % 2 == 1)
    def _():
        sel_tile(pl.multiple_of((n_full - 1) * tk, tk), tk, None)

    k_diag = pl.multiple_of(n_full * tk, tk)
    kpos_sel = k_diag + lax.broadcasted_iota(jnp.int32, (tk, 1), 0)
    sel_tile(k_diag, tk, jnp.where(kpos_sel <= t_row, 0.0, NEG))
    for h in heads:
        out_scr[h] = out_scr[h] + gate_row(h, 1) * finish(h)

    init_state()
    tkw = WINDOW + tq
    k0 = pl.multiple_of(jnp.maximum(t0 - WINDOW, 0), tq)
    dist = t_row - (k0 + lax.broadcasted_iota(jnp.int32, (tkw, 1), 0))
    kt = kw_ref[pl.ds(k0, tkw), :]
    online_tile([kt, kt], q4_scr, [vwa_ref[h, :, pl.ds(k0, tkw)] for h in heads],
                jnp.where((dist >= 0) & (dist < WINDOW), 0.0, NEG))
    both = jnp.concatenate([out_scr[h] + gate_row(h, 2) * finish(h) for h in heads], axis=0)
    for g in range(NSA_GROUP):
        o_ref[:, g * LANES:(g + 1) * LANES] = both[:, g * tq:(g + 1) * tq].T.astype(o_ref.dtype)


def _nsa(q3, kc, vc_t, ks_aug, vs_aug, kw, vw_aug, gt, ov_t):
    bsz, seq, qw = q3.shape
    tq = NSA_TQ
    n_rows = kc.shape[1]
    nh = NSA_KV_HEADS
    full = pl.BlockSpec((None, seq, LANES), lambda b, i: (b, 0, 0))
    head_tok = pl.BlockSpec((None, nh, seq, LANES), lambda b, i: (b, 0, 0, 0))
    head_feat = pl.BlockSpec((None, nh, LANES, seq), lambda b, i: (b, 0, 0, 0))
    tile = lambda w: pl.BlockSpec((None, tq, w), lambda b, i: (b, i, 0))
    cols = NSA_GROUP * tq
    return pl.pallas_call(
        functools.partial(_nsa_kernel, seq=seq),
        grid=(bsz, seq // tq),
        in_specs=[tile(qw),
                  pl.BlockSpec((None, n_rows, LANES), lambda b, i: (b, 0, 0)),
                  pl.BlockSpec((None, LANES, n_rows), lambda b, i: (b, 0, 0)),
                  head_tok, head_feat, full, head_feat, tile(LANES), _const_spec(ov_t.shape)],
        out_specs=tile(qw),
        out_shape=jax.ShapeDtypeStruct((bsz, seq, qw), BF16),
        scratch_shapes=[pltpu.VMEM((nh, 1, cols), F32),
                        pltpu.VMEM((nh, LANES, cols), F32),
                        pltpu.VMEM((nh, HEAD_DIM, cols), F32),
                        pltpu.VMEM((nh, cols, LANES), BF16),
                        pltpu.VMEM((nh, cols, LANES), BF16),
                        pltpu.VMEM((seq // SEL_LEN, tq), F32)],
        compiler_params=_cparams(("parallel", "arbitrary")),
        name="nsa_attention",
    )(q3, kc, vc_t, ks_aug, vs_aug, kw, vw_aug, gt, ov_t)


def _sb_kernel(q_ref, k_ref, vt_ref, o_ref, r_scr, acc_scr):
    tq = q_ref.shape[0]
    tk = SB_TILES * tq
    i_tile = pl.program_id(1)
    t0 = i_tile * tq
    q = q_ref[...]
    lane = lax.broadcasted_iota(jnp.int32, (1, LANES), 1)
    feat = lax.broadcasted_iota(jnp.int32, (LANES, 1), 0)
    t_row = t0 + lax.broadcasted_iota(jnp.int32, (1, tq), 1)
    kpos_col = lax.broadcasted_iota(jnp.int32, (tk, 1), 0)
    sr = lax.broadcasted_iota(jnp.int32, (tq, tq), 0)
    sc = lax.broadcasted_iota(jnp.int32, (tq, tq), 1)
    later = jnp.where(sc > sr, 1.0, 0.0).astype(BF16)
    qm = []
    for h in range(SB_HEADS):
        in_head = (lane >> _LOG_HEAD) == (h % 2)
        pair = q[:, (h // 2) * LANES:(h // 2 + 1) * LANES]
        qm.append(jnp.where(in_head, pair, jnp.zeros((), BF16)))
    r_scr[...] = jnp.zeros_like(r_scr)
    acc_scr[...] = jnp.zeros_like(acc_scr)

    def cond(c):
        j, r_max = c
        return (j >= 0) & (r_max > SB_EXIT)

    def body(c):
        j, _ = c
        k0 = pl.multiple_of(jnp.maximum(j + 1 - SB_TILES, 0) * tq, tq)
        kt = k_ref[pl.ds(k0, tk), :]
        z = jnp.concatenate([_dot_nt(kt[:, (h // 2) * LANES:(h // 2 + 1) * LANES], qm[h])
                             for h in range(SB_HEADS)], axis=1)
        kpos = k0 + kpos_col
        hidden = jnp.where((kpos < t_row) & (kpos < (j + 1) * tq), 0.0, NEG)
        z = z + jnp.concatenate([hidden] * SB_HEADS, axis=1)
        ls = jnp.minimum(z, 0.0) - jnp.log(1.0 + jnp.exp(-jnp.abs(z)))
        rest = ls - z
        hi = rest.astype(BF16)
        lo = (rest - hi.astype(F32)).astype(BF16)
        r_prev = r_scr[...]
        totals = [jnp.sum(rest[a * tq:(a + 1) * tq], axis=0, keepdims=True) for a in range(SB_TILES)]
        after_tiles = []
        right = r_prev
        for a in reversed(range(SB_TILES)):
            rows = slice(a * tq, (a + 1) * tq)
            after_tiles.append(_dot(later, hi[rows]) + _dot(later, lo[rows]) + right)
            right = right + totals[a]
        after = jnp.concatenate(after_tiles[::-1], axis=0)
        w = jnp.exp(ls + after).astype(BF16)
        for h in range(SB_HEADS):
            cols = slice(h * tq, (h + 1) * tq)
            vt = vt_ref[(h // 2) * LANES:(h // 2 + 1) * LANES, pl.ds(k0, tk)]
            acc_scr[:, cols] = acc_scr[:, cols] + _dot(vt, w[:, cols])
        r_new = right
        r_scr[...] = r_new
        return j - SB_TILES, jnp.max(r_new)

    lax.while_loop(cond, body, (i_tile, jnp.zeros((), F32)))
    for hp in range(SB_HEADS // 2):
        even = acc_scr[:, (2 * hp) * tq:(2 * hp + 1) * tq]
        odd = acc_scr[:, (2 * hp + 1) * tq:(2 * hp + 2) * tq]
        o_ref[:, hp * LANES:(hp + 1) * LANES] = jnp.where(feat < HEAD_DIM, even, odd).T.astype(o_ref.dtype)


def _sb(q3, k3, v_t):
    bsz, seq, w = q3.shape
    tq = SB_TQ
    full = pl.BlockSpec((None, seq, w), lambda b, i: (b, 0, 0))
    full_t = pl.BlockSpec((None, w, seq), lambda b, i: (b, 0, 0))
    tile = pl.BlockSpec((None, tq, w), lambda b, i: (b, i, 0))
    return pl.pallas_call(
        _sb_kernel,
        grid=(bsz, seq // tq),
        in_specs=[tile, full, full_t],
        out_specs=tile,
        out_shape=jax.ShapeDtypeStruct((bsz, seq, w), BF16),
        scratch_shapes=[pltpu.VMEM((1, SB_HEADS * tq), F32), pltpu.VMEM((LANES, SB_HEADS * tq), F32)],
        compiler_params=_cparams(("parallel", "arbitrary")),
        name="sb_attention",
    )(q3, k3, v_t)


def _merge_kernel(x_ref, g_ref, a_ref, b_ref, c_ref, wg_ref, wb_ref, wo_ref, o_ref):
    x = x_ref[...]
    d = x.shape[1]
    h = _rms(x, g_ref[...]).astype(BF16)
    mixed = None
    for n, br in enumerate((a_ref, b_ref, c_ref)):
        gate = jax.nn.sigmoid(_dot(h, wg_ref[:, n * d:(n + 1) * d]))
        y = gate * _dot(br[...], wb_ref[n])
        mixed = y if mixed is None else mixed + y
    o_ref[...] = x + _dot(mixed.astype(BF16), wo_ref[...])


def _merge(x2, layer, g, o_a, o_b, o_c, w_gate, w_branch, w_out):
    t, d = x2.shape
    tm = min(TM_PROJ, t)
    row = lambda i: (i, 0)
    bw = o_a.shape[1]
    return pl.pallas_call(
        _merge_kernel,
        grid=(t // tm,),
        in_specs=[pl.BlockSpec((tm, d), row), _layer_spec(g, layer),
                  pl.BlockSpec((tm, bw), row), pl.BlockSpec((tm, bw), row), pl.BlockSpec((tm, bw), row),
                  _layer_spec(w_gate, layer), _layer_spec(w_branch, layer), _layer_spec(w_out, layer)],
        out_specs=pl.BlockSpec((tm, d), row),
        out_shape=jax.ShapeDtypeStruct((t, d), F32),
        compiler_params=_cparams(("parallel",)),
        name="gated_merge",
    )(x2, g, o_a, o_b, o_c, w_gate, w_branch, w_out)


def _ffn_kernel(x_ref, halo_ref, g_ref, wup_ref, cw_ref, cb_ref, wdn_ref, gout_ref, o_ref, act_scr, *,
                tiles_per_seq, norm_out):
    x = x_ref[...]
    tm = x.shape[0]
    d_ff = wdn_ref.shape[0]
    at_start = (pl.program_id(0) % tiles_per_seq) == 0
    g = g_ref[...]
    h_halo = jnp.where(at_start, 0.0, _rms(halo_ref[...], g))
    h = jnp.concatenate([h_halo, _rms(x, g)], axis=0).astype(BF16)
    for f in range(d_ff // FF_TILE):
        halves = []
        for base in (0, d_ff):
            cols = slice(base + f * FF_TILE, base + (f + 1) * FF_TILE)
            u = _dot(h, wup_ref[:, cols])
            w = cw_ref[:, cols]
            conv = (w[2:3] * u[SUBLANES:]
                    + w[1:2] * pltpu.roll(u, 1, 0)[SUBLANES:]
                    + w[0:1] * pltpu.roll(u, 2, 0)[SUBLANES:]
                    + cb_ref[:, cols])
            halves.append(conv)
        act_scr[:, f * FF_TILE:(f + 1) * FF_TILE] = (jax.nn.silu(halves[0]) * halves[1]).astype(BF16)
    y = x + _dot(act_scr[...], wdn_ref[...])
    o_ref[...] = _rms(y, gout_ref[...]) if norm_out else y


def _ffn(x2, layer, g, w_up, conv_w, conv_b, w_down, g_out, seq, norm_out):
    t, d = x2.shape
    tm = min(TM_FFN, seq)
    tiles_per_seq = seq // tm
    row = lambda i: (i, 0)
    halo = lambda i: (jnp.maximum(i * (tm // SUBLANES) - 1, 0), 0)
    return pl.pallas_call(
        functools.partial(_ffn_kernel, tiles_per_seq=tiles_per_seq, norm_out=norm_out),
        grid=(t // tm,),
        in_specs=[pl.BlockSpec((tm, d), row), pl.BlockSpec((SUBLANES, d), halo), _layer_spec(g, layer),
                  _layer_spec(w_up, layer), _layer_spec(conv_w, layer), _layer_spec(conv_b, layer),
                  _layer_spec(w_down, layer), _const_spec(g_out.shape)],
        out_specs=pl.BlockSpec((tm, d), row),
        out_shape=jax.ShapeDtypeStruct((t, d), F32),
        scratch_shapes=[pltpu.VMEM((tm, w_down.shape[1]), BF16)],
        compiler_params=_cparams(("parallel",)),
        name="conv_ffn",
    )(x2, x2, g, w_up, conv_w, conv_b, w_down, g_out)


def _norm_kernel(x_ref, g_ref, o_ref):
    o_ref[...] = _rms(x_ref[...], g_ref[...])


def _final_norm(x2, g):
    t, d = x2.shape
    tm = min(TM_PROJ, t)
    return pl.pallas_call(
        _norm_kernel,
        grid=(t // tm,),
        in_specs=[pl.BlockSpec((tm, d), lambda i: (i, 0)), _const_spec((1, d))],
        out_specs=pl.BlockSpec((tm, d), lambda i: (i, 0)),
        out_shape=jax.ShapeDtypeStruct((t, d), F32),
        compiler_params=_cparams(("parallel",)),
        name="final_norm",
    )(x2, g)


def _nsa_slot_order():
    return [NSA_GROUP * (s % 2) + s // 2 for s in range(NSA_HEADS)]


def _pack_w_in(w):
    d = w.shape[0]
    o_q, o_kv, o_g, o_sb, o_gm = 512, 1024, 1792, 1816, 3352
    kv_w = NSA_KV_HEADS * HEAD_DIM
    sb_w = SB_HEADS * HEAD_DIM
    u = w[:, 0:o_q]
    q = [w[:, o_q + hq * HEAD_DIM:o_q + (hq + 1) * HEAD_DIM] for hq in _nsa_slot_order()]
    k = [w[:, o_kv + br * 2 * kv_w:o_kv + br * 2 * kv_w + kv_w] for br in range(3)]
    v = [w[:, o_kv + br * 2 * kv_w + kv_w:o_kv + (br + 1) * 2 * kv_w] for br in range(3)]
    gates = [w[:, o_g:o_sb], jnp.zeros((d, LANES - (o_sb - o_g)), w.dtype)]
    packed = jnp.concatenate([u] + q + k + [v[0]] + gates + [w[:, o_sb:o_sb + 2 * sb_w]], axis=1).astype(BF16)
    w_vt = jnp.concatenate([v[1], v[2], w[:, o_sb + 2 * sb_w:o_gm]], axis=1).astype(BF16).T
    return packed, w_vt, w[:, o_gm:].astype(BF16)


def _rope_tables(seq):
    inv_freq = 1.0 / (ROPE_THETA ** (jnp.arange(0, HEAD_DIM, 2, dtype=F32) / HEAD_DIM))
    ang = jnp.arange(seq, dtype=F32)[:, None] * inv_freq[None, :]
    cos, sin = jnp.cos(ang), jnp.sin(ang)
    cos_t = jnp.tile(cos, (1, LANES // (HEAD_DIM // 2)))
    sin_t = jnp.tile(jnp.concatenate([-sin, sin], axis=1), (1, LANES // HEAD_DIM))
    return cos_t, sin_t


def _overlap_t(seq):
    n_rows = seq // CMP_STRIDE
    n_sel = seq // SEL_LEN
    cmp_start = jnp.arange(n_rows) * CMP_STRIDE
    sel_start = jnp.arange(n_sel) * SEL_LEN
    ov = ((cmp_start[None, :] < sel_start[:, None] + SEL_LEN)
          & (cmp_start[None, :] + CMP_LEN > sel_start[:, None])
          & (jnp.arange(n_rows)[None, :] < (seq - CMP_LEN) // CMP_STRIDE + 1))
    return ov.astype(BF16)


def kernel(x, norm_mix, w_in, ssm_lam_re, ssm_lam_im, ssm_log_step, ssm_b_re, ssm_b_im, ssm_c_re, ssm_c_im, ssm_d, ssm_w_glu, cmp_w1_k, cmp_w2_k, cmp_pos_k, cmp_w1_v, cmp_w2_v, cmp_pos_v, w_branch, w_out, norm_ffn, ffn_w_up, ffn_conv_w, ffn_conv_b, ffn_w_down, norm_final):
    bsz, seq, d = x.shape
    depth = w_in.shape[0]
    t = bsz * seq
    assert seq % 512 == 0 and seq // SEL_LEN <= HEAD_DIM
    assert seq >= WINDOW + NSA_TQ
    if depth == 0:
        return _final_norm(x.reshape(t, d).astype(F32), norm_final.reshape(1, d)).reshape(x.shape).astype(x.dtype)
    cos_t, sin_t = _rope_tables(seq)
    ov_t = _overlap_t(seq)
    w_packed, w_vt, w_gate = jax.vmap(_pack_w_in)(w_in)
    s5_tabs = jax.vmap(_s5_tables)(ssm_lam_re, ssm_lam_im, ssm_log_step, ssm_b_re, ssm_b_im, ssm_c_re, ssm_c_im)
    s5_d = ssm_d.reshape(depth, 1, -1).astype(F32)
    s5_glu = ssm_w_glu.astype(BF16)
    cmp_k = jax.vmap(_compress_weights)(cmp_w1_k, cmp_w2_k, cmp_pos_k)
    cmp_v = jax.vmap(functools.partial(_compress_weights, feature_major_out=True))(cmp_w1_v, cmp_w2_v, cmp_pos_v)
    wb_nsa = jnp.concatenate([w_branch[:, 1, hq * HEAD_DIM:(hq + 1) * HEAD_DIM] for hq in _nsa_slot_order()], axis=1)
    wb_all = jnp.stack([w_branch[:, 0], wb_nsa, w_branch[:, 2]], axis=1).astype(BF16)
    wo_all = w_out.astype(BF16)
    g_mix = norm_mix.reshape(depth, 1, d).astype(F32)
    g_ffn = norm_ffn.reshape(depth, 1, d).astype(F32)
    up_all = ffn_w_up.astype(BF16)
    down_all = ffn_w_down.astype(BF16)
    conv_w = ffn_conv_w.astype(F32)
    conv_b = ffn_conv_b.reshape(depth, 1, -1).astype(F32)
    g_final = norm_final.reshape(1, d).astype(F32)

    x2 = x.reshape(t, d).astype(F32)
    b3 = lambda a: a.reshape(bsz, seq, a.shape[-1])
    rows16 = lambda a: a.reshape(bsz, seq // CMP_STRIDE, a.shape[-1])
    for l in range(depth):
        (u, q, kc_raw, ks_aug, kw, vc_raw, vs_aug, vw_aug, gt, sq, sk, sv_t) = _proj_in(
            x2, l, g_mix, w_packed, w_vt, cos_t, sin_t, bsz, seq)
        o_ssm = _s5(b3(u), l, s5_tabs, s5_d, s5_glu)
        kc, vc_t = _compress(rows16(kc_raw), rows16(vc_raw), l, cmp_k, cmp_v)
        o_nsa = _nsa(b3(q), kc, vc_t, ks_aug, vs_aug, b3(kw), vw_aug, b3(gt), ov_t)
        o_sb = _sb(b3(sq), b3(sk), sv_t)
        x2 = _merge(x2, l, g_mix, o_ssm.reshape(t, -1), o_nsa.reshape(t, -1), o_sb.reshape(t, -1),
                    w_gate, wb_all, wo_all)
        x2 = _ffn(x2, l, g_ffn, up_all, conv_w, conv_b, down_all, g_final, seq, norm_out=(l == depth - 1))
    return x2.reshape(bsz, seq, d).astype(x.dtype)
```

```python
import functools
import math

import jax
import jax.numpy as jnp
from jax import lax
from jax.experimental import pallas as pl
from jax.experimental.pallas import tpu as pltpu

F32 = jnp.float32
BF16 = jnp.bfloat16

HEAD_DIM = 64
RMS_EPS = 1e-6
ROPE_THETA = 10000.0
SSM_GROUP = 16
SSM_STATE = 64
NSA_HEADS = 8
NSA_KV_HEADS = 2
NSA_GROUP = NSA_HEADS // NSA_KV_HEADS
CMP_LEN = 32
CMP_STRIDE = 16
CMP_HIDDEN = 128
SEL_LEN = 64
SEL_TOPK = 16
WINDOW = 512
FORCE_BONUS = 1e6
SB_HEADS = 8
CONV_WIDTH = 3
BRANCH_WIDTH = 512

LANES = 128
SUBLANES = 8
VMEM_LIMIT = 56 * 1024 * 1024

TM_PROJ = 512
TM_FFN = 512
FF_TILE = 256
S5_CHUNK = 512
S5_LANES = 256
S5_MXU_SHIFTS = 4
NSA_TQ = 256
NSA_TK_SEL = 512
SB_TQ = 256
SB_TILES = 2

_LOG_HEAD = HEAD_DIM.bit_length() - 1
_LOG_SEL = SEL_LEN.bit_length() - 1
NEG = -1e30
SB_EXIT = -104.0


def _cparams(sem):
    return pltpu.CompilerParams(dimension_semantics=sem, vmem_limit_bytes=VMEM_LIMIT)


def _const_spec(shape):
    n = len(shape)
    return pl.BlockSpec(shape, lambda *_: (0,) * n, pipeline_mode=pl.Buffered(1))


def _layer_spec(stacked, layer):
    n = stacked.ndim - 1
    return pl.BlockSpec((None,) + stacked.shape[1:], lambda *_: (layer,) + (0,) * n,
                        pipeline_mode=pl.Buffered(1))


def _rms(x, g):
    return x * lax.rsqrt(jnp.mean(x * x, axis=-1, keepdims=True) + RMS_EPS) * g


def _dot(a, b):
    return jnp.dot(a, b, preferred_element_type=F32)


def _dot_nt(a, b):
    return lax.dot_general(a, b, (((1,), (1,)), ((), ())), preferred_element_type=F32)


def _split3(x):
    h = x.astype(BF16)
    r = x - h.astype(F32)
    m = r.astype(BF16)
    lo = (r - m.astype(F32)).astype(BF16)
    return h, m, lo


_C_U = 0
_C_ROPE = 512
_C_V = 1408
_C_G = 1536
_C_SB = 1664
_C_END = 2688
_R_VS, _R_VW, _R_SV, _R_END = 0, 128, 256, 768


def _proj_kernel(x_ref, g_ref, w_ref, wvt_ref, cos_ref, sin_ref,
                 u_ref, q_ref, kc_ref, ksa_ref, kw_ref, vc_ref, vsa_ref, vwa_ref, gt_ref,
                 sq_ref, sk_ref, svt_ref, rows_scr, *, n_pos):
    tm = x_ref.shape[0]
    h = _rms(x_ref[...], g_ref[...]).astype(BF16)

    def store_rows16(val, out_ref):
        rows_scr[...] = val
        for b in range(CMP_STRIDE):
            out_ref[:, b * LANES:(b + 1) * LANES] = rows_scr[pl.ds(b, tm // CMP_STRIDE, stride=CMP_STRIDE), :]

    def mm(a, b):
        return _dot(h, w_ref[:, a:b])

    vt = _dot_nt(wvt_ref[...], h)
    ones_rows = jnp.where(lax.broadcasted_iota(jnp.int32, (HEAD_DIM, tm), 0) == 0, 1.0, 0.0)
    for hh in range(NSA_KV_HEADS):
        vsa_ref[hh] = jnp.concatenate(
            [vt[_R_VS + hh * HEAD_DIM:_R_VS + (hh + 1) * HEAD_DIM], ones_rows], axis=0).astype(BF16)
        vwa_ref[hh] = jnp.concatenate(
            [vt[_R_VW + hh * HEAD_DIM:_R_VW + (hh + 1) * HEAD_DIM], ones_rows], axis=0).astype(BF16)
    svt_ref[...] = vt[_R_SV:_R_END].astype(BF16)

    u_ref[...] = mm(_C_U, _C_ROPE)

    r = mm(_C_ROPE, _C_V)
    cos = cos_ref[...]
    sin = sin_ref[...]
    lane = lax.broadcasted_iota(jnp.int32, cos.shape, 1)
    first = (lane & (HEAD_DIM - 1)) < (HEAD_DIM // 2)
    roped = []
    for c in range((_C_V - _C_ROPE) // LANES):
        rc = r[:, c * LANES:(c + 1) * LANES]
        partner = jnp.where(first, pltpu.roll(rc, LANES - HEAD_DIM // 2, 1),
                            pltpu.roll(rc, HEAD_DIM // 2, 1))
        roped.append(rc * cos + partner * sin)
    scale = HEAD_DIM ** -0.5
    for c in range(4):
        q_ref[:, c * LANES:(c + 1) * LANES] = (roped[c] * scale).astype(BF16)
    store_rows16(roped[4], kc_ref)
    kw_ref[...] = roped[6].astype(BF16)
    pos = (pl.program_id(0) % n_pos) * tm + lax.broadcasted_iota(jnp.int32, (tm, 1), 0)
    blk = pos >> _LOG_SEL
    lane_k = lax.broadcasted_iota(jnp.int32, (1, LANES), 1)
    for hh in range(NSA_KV_HEADS):
        own = (lane_k >> _LOG_HEAD) == hh
        one_hot = jnp.where(blk == (lane_k & (HEAD_DIM - 1)), 1.0, 0.0)
        ksa_ref[hh] = jnp.where(own, roped[5], one_hot).astype(BF16)

    store_rows16(mm(_C_V, _C_G), vc_ref)
    gt_ref[...] = jax.nn.sigmoid(mm(_C_G, _C_SB))

    sb = mm(_C_SB, _C_END)
    sq_ref[...] = (sb[:, 0:512] * scale).astype(BF16)
    sk_ref[...] = sb[:, 512:1024].astype(BF16)


def _proj_in(x2, layer, g, w_packed, w_vt, cos_t, sin_t, bsz, seq):
    t, d = x2.shape
    tm = min(TM_PROJ, seq)
    n_pos = seq // tm
    row = lambda i: (i, 0)
    pos = lambda i: (i % n_pos, 0)
    nh = NSA_KV_HEADS

    def tok(w, dt):
        return pl.BlockSpec((tm, w), row), jax.ShapeDtypeStruct((t, w), dt)

    def feat_major(w, dt):
        return (pl.BlockSpec((None, w, tm), lambda i: (i // n_pos, 0, i % n_pos)),
                jax.ShapeDtypeStruct((bsz, w, seq), dt))

    def head_tok(dt):
        return (pl.BlockSpec((None, nh, tm, LANES), lambda i: (i // n_pos, 0, i % n_pos, 0)),
                jax.ShapeDtypeStruct((bsz, nh, seq, LANES), dt))

    def head_feat(dt):
        return (pl.BlockSpec((None, nh, LANES, tm), lambda i: (i // n_pos, 0, 0, i % n_pos)),
                jax.ShapeDtypeStruct((bsz, nh, LANES, seq), dt))

    def rows16():
        return (pl.BlockSpec((tm // CMP_STRIDE, CMP_STRIDE * LANES), row),
                jax.ShapeDtypeStruct((t // CMP_STRIDE, CMP_STRIDE * LANES), F32))

    outs = [tok(512, F32), tok(512, BF16), rows16(), head_tok(BF16), tok(128, BF16),
            rows16(), head_feat(BF16), head_feat(BF16), tok(128, F32),
            tok(512, BF16), tok(512, BF16), feat_major(512, BF16)]
    return pl.pallas_call(
        functools.partial(_proj_kernel, n_pos=n_pos),
        grid=(t // tm,),
        in_specs=[pl.BlockSpec((tm, d), row),
                  _layer_spec(g, layer),
                  _layer_spec(w_packed, layer),
                  _layer_spec(w_vt, layer),
                  pl.BlockSpec((tm, LANES), pos),
                  pl.BlockSpec((tm, LANES), pos)],
        out_specs=[spec for spec, _ in outs],
        out_shape=[shape for _, shape in outs],
        scratch_shapes=[pltpu.VMEM((tm, LANES), F32)],
        compiler_params=_cparams(("parallel",)),
        name="proj_in",
    )(x2, g, w_packed, w_vt, cos_t, sin_t)


def _s5_kernel(u_ref, b8_ref, ccat_ref, pw_re_ref, pw_im_ref, cy_re_ref, cy_im_ref, d_ref, wglu_ref, o_ref,
               car_re, car_im):
    @pl.when(pl.program_id(1) == 0)
    def _():
        car_re[...] = jnp.zeros_like(car_re)
        car_im[...] = jnp.zeros_like(car_im)

    u = u_ref[...]
    chunk, width = u.shape
    n_tiles = b8_ref.shape[0]
    n_sub = chunk // SUBLANES
    u3 = u.reshape(n_sub, SUBLANES, width)
    row8 = lax.broadcasted_iota(jnp.int32, (1, SUBLANES, 1), 1)
    shifted = [u.astype(BF16)]
    for s in range(1, S5_MXU_SHIFTS):
        shifted.append(jnp.where(row8 >= s, pltpu.roll(u3, s, 1), 0.0).reshape(chunk, width).astype(BF16))
    ucat = [jnp.concatenate([sh[:, b * LANES:(b + 1) * LANES] for sh in shifted], axis=1)
            for b in range(width // LANES)]
    y_blocks = []
    for blk in range(n_tiles // 2):
        pair = (2 * blk, 2 * blk + 1)
        hr, hi = {}, {}
        for j in pair:
            x = _dot(ucat[blk], b8_ref[j])
            xr = x[:, :S5_LANES].reshape(n_sub, SUBLANES, S5_LANES)
            xi = x[:, S5_LANES:].reshape(n_sub, SUBLANES, S5_LANES)
            sr = pltpu.roll(xr, S5_MXU_SHIFTS, 1)
            si = pltpu.roll(xi, S5_MXU_SHIFTS, 1)
            cr = pw_re_ref[j]
            ci = pw_im_ref[j]
            hr[j], hi[j] = xr + cr * sr - ci * si, xi + cr * si + ci * sr
        c_re = {j: car_re[j:j + 1, :] for j in pair}
        c_im = {j: car_im[j:j + 1, :] for j in pair}
        out_re = {j: [] for j in pair}
        out_im = {j: [] for j in pair}
        for v in range(n_sub):
            for j in pair:
                pr = cy_re_ref[j]
                pi = cy_im_ref[j]
                t_re = hr[j][v] + pr * c_re[j] - pi * c_im[j]
                t_im = hi[j][v] + pr * c_im[j] + pi * c_re[j]
                c_re[j] = t_re[SUBLANES - 1:SUBLANES, :]
                c_im[j] = t_im[SUBLANES - 1:SUBLANES, :]
                out_re[j].append(t_re)
                out_im[j].append(t_im)
        y_blk = None
        for j in pair:
            car_re[j:j + 1, :] = c_re[j]
            car_im[j:j + 1, :] = c_im[j]
            hcat = jnp.concatenate([jnp.concatenate(out_re[j], axis=0), jnp.concatenate(out_im[j], axis=0)],
                                   axis=1).astype(BF16)
            yj = _dot(hcat, ccat_ref[j])
            y_blk = yj if y_blk is None else y_blk + yj
        y_blocks.append(y_blk)
    y = jnp.concatenate(y_blocks, axis=1) + d_ref[...] * u
    z = jax.nn.gelu(y)
    o_ref[...] = (z * jax.nn.sigmoid(_dot(z.astype(BF16), wglu_ref[...]))).astype(o_ref.dtype)


def _s5_tables(lam_re, lam_im, log_step, b_re, b_im, c_re, c_im):
    g_n, p_n = lam_re.shape
    i_n = b_re.shape[-1]
    width = g_n * i_n
    lanes_total = g_n * p_n
    n_tiles = lanes_total // S5_LANES
    step = jnp.exp(log_step.astype(F32))[:, None]
    lr = lam_re.astype(F32)
    li = lam_im.astype(F32)
    mag = jnp.exp(lr * step)
    a_re = mag * jnp.cos(li * step)
    a_im = mag * jnp.sin(li * step)
    den = lr * lr + li * li
    nr = a_re - 1.0
    f_re = (nr * lr + a_im * li) / den
    f_im = (a_im * lr - nr * li) / den
    br = b_re.astype(F32)
    bi = b_im.astype(F32)
    bb_re = f_re[..., None] * br - f_im[..., None] * bi
    bb_im = f_re[..., None] * bi + f_im[..., None] * br
    log_mag = (lr * step).reshape(1, lanes_total)
    ang = (li * step).reshape(1, lanes_total)

    def powers(n):
        m = jnp.exp(n * log_mag)
        return m * jnp.cos(n * ang), m * jnp.sin(n * ang)

    gpt = S5_LANES // p_n
    gpb = LANES // i_n
    tile_id = jnp.arange(n_tiles)
    in_tile = (jnp.arange(gpb)[None, :, None]
               == (tile_id[:, None, None] % (gpb // gpt)) * gpt + jnp.arange(gpt)[None, None, :]).astype(F32)

    n_sh = S5_MXU_SHIFTS
    as_re, as_im = powers(jnp.arange(n_sh, dtype=F32)[:, None])
    as_re = as_re.reshape(n_sh, g_n, p_n, 1)
    as_im = as_im.reshape(n_sh, g_n, p_n, 1)
    w_re = as_re * bb_re[None] - as_im * bb_im[None]
    w_im = as_re * bb_im[None] + as_im * bb_re[None]

    def in_tiles(w):
        w = w.reshape(n_sh, n_tiles, gpt, p_n, i_n).transpose(1, 0, 4, 2, 3)
        out = in_tile[:, None, :, None, :, None] * w[:, :, None, :, :, :]
        return out.reshape(n_tiles, n_sh * LANES, S5_LANES)

    b8 = jnp.concatenate([in_tiles(w_re), in_tiles(w_im)], axis=2).astype(BF16)

    def out_tiles(c):
        c = c.astype(F32).reshape(n_tiles, gpt, i_n, p_n).transpose(0, 1, 3, 2)
        out = in_tile.transpose(0, 2, 1)[:, :, None, :, None] * c[:, :, :, None, :]
        return out.reshape(n_tiles, S5_LANES, LANES)

    ccat = jnp.concatenate([out_tiles(c_re), -out_tiles(c_im)], axis=1).astype(BF16)

    cy_re, cy_im = powers(jnp.arange(1, SUBLANES + 1, dtype=F32)[:, None])

    def tiles(a):
        return a.reshape(a.shape[0], n_tiles, S5_LANES).transpose(1, 0, 2)

    live = (jnp.arange(SUBLANES) >= n_sh).astype(F32)[None, :, None]
    pw_re, pw_im = powers(jnp.full((1, 1), float(n_sh), F32))
    return b8, ccat, tiles(pw_re) * live, tiles(pw_im) * live, tiles(cy_re), tiles(cy_im)


def _s5(u3, layer, tables, d_row, wglu):
    bsz, seq, width = u3.shape
    consts = tuple(tables) + (d_row, wglu)
    chunk = min(S5_CHUNK, seq)
    n_tiles = tables[0].shape[1]
    return pl.pallas_call(
        _s5_kernel,
        grid=(bsz, seq // chunk),
        in_specs=[pl.BlockSpec((None, chunk, width), lambda b, c: (b, c, 0))]
                 + [_layer_spec(a, layer) for a in consts],
        out_specs=pl.BlockSpec((None, chunk, width), lambda b, c: (b, c, 0)),
        out_shape=jax.ShapeDtypeStruct((bsz, seq, width), BF16),
        scratch_shapes=[pltpu.VMEM((n_tiles, S5_LANES), F32), pltpu.VMEM((n_tiles, S5_LANES), F32)],
        compiler_params=_cparams(("parallel", "arbitrary")),
        name="s5_scan",
    )(u3, *consts)


def _compress_kernel(k_ref, v_ref, w1a_k, w1b_k, pa_k, pb_k, w2_k, w1a_v, w1b_v, pa_v, pb_v, w2_v,
                     kc_ref, vc_ref):
    def hidden(t_ref, w1a, w1b, pa, pb):
        r = t_ref[...]
        n = r.shape[0]
        first = _dot((r + pa[...]).astype(BF16), w1a[...])
        second = _dot((r + pb[...]).astype(BF16), w1b[...])
        rowi = lax.broadcasted_iota(jnp.int32, second.shape, 0)
        nxt = jnp.where(rowi < n - 1, pltpu.roll(second, n - 1, 0), 0.0)
        return jax.nn.gelu(first + nxt).astype(BF16)

    kc_ref[...] = _dot(hidden(k_ref, w1a_k, w1b_k, pa_k, pb_k), w2_k[...]).astype(kc_ref.dtype)
    vc_ref[...] = _dot_nt(w2_v[...], hidden(v_ref, w1a_v, w1b_v, pa_v, pb_v)).astype(vc_ref.dtype)


def _compress_weights(w1, w2, pos, feature_major_out=False):
    half = CMP_LEN // 2
    eye = jnp.eye(NSA_KV_HEADS, dtype=F32)

    def first_layer(w):
        m = w[:, None, :, None, :] * eye[None, :, None, :, None]
        return m.reshape(half * NSA_KV_HEADS * HEAD_DIM, NSA_KV_HEADS * CMP_HIDDEN).astype(BF16)

    def pos_row(p):
        return jnp.broadcast_to(p[:, None, :], (half, NSA_KV_HEADS, HEAD_DIM)).reshape(1, -1).astype(F32)

    w2_bd = (eye[:, None, :, None] * w2[None, :, None, :]).reshape(
        NSA_KV_HEADS * CMP_HIDDEN, NSA_KV_HEADS * HEAD_DIM).astype(BF16)
    if feature_major_out:
        w2_bd = w2_bd.T
    return (first_layer(w1[:half]), first_layer(w1[half:]), pos_row(pos[:half]), pos_row(pos[half:]), w2_bd)


def _compress(kc_raw, vc_raw, layer, wk, wv):
    bsz, n_rows, feat = kc_raw.shape
    consts = list(wk) + list(wv)
    blk = pl.BlockSpec((None, n_rows, feat), lambda b: (b, 0, 0))
    return pl.pallas_call(
        _compress_kernel,
        grid=(bsz,),
        in_specs=[blk, blk] + [_layer_spec(c, layer) for c in consts],
        out_specs=[pl.BlockSpec((None, n_rows, LANES), lambda b: (b, 0, 0)),
                   pl.BlockSpec((None, LANES, n_rows), lambda b: (b, 0, 0))],
        out_shape=[jax.ShapeDtypeStruct((bsz, n_rows, LANES), BF16),
                   jax.ShapeDtypeStruct((bsz, LANES, n_rows), BF16)],
        compiler_params=_cparams(("parallel",)),
        name="nsa_compress",
    )(kc_raw, vc_raw, *consts)


def _nsa_kernel(q_ref, kc_ref, vct_ref, ksa_ref, vsa_ref, kw_ref, vwa_ref, gt_ref, ovt_ref, o_ref,
                m_scr, acc_scr, out_scr, q4_scr, q4s_scr, cnt_scr, *, seq):
    tq = q_ref.shape[0]
    n_cmp_rows = kc_ref.shape[0]
    n_sel = seq // SEL_LEN
    i_tile = pl.program_id(1)
    t0 = i_tile * tq
    heads = range(NSA_KV_HEADS)

    q = q_ref[...]
    gt_t = gt_ref[...].T
    lane = lax.broadcasted_iota(jnp.int32, (1, LANES), 1)
    t_row = t0 + lax.broadcasted_iota(jnp.int32, (1, tq), 1)

    def tile_lanes(a):
        return jnp.concatenate([a] * NSA_GROUP, axis=1)

    def init_state():
        m_scr[...] = jnp.full_like(m_scr, NEG)
        acc_scr[...] = jnp.zeros_like(acc_scr)

    def online_tile(kts, q_scr, vts, bias):
        scores = [_dot_nt(kts[h], q_scr[h]) for h in heads]
        for h in heads:
            s = scores[h]
            if bias is not None:
                s = s + tile_lanes(bias)
            m_prev = m_scr[h]
            m_next = jnp.maximum(m_prev, jnp.max(s, axis=0, keepdims=True))
            alpha = jnp.exp(m_prev - m_next)
            p = jnp.exp(s - m_next).astype(BF16)
            acc_scr[h] = alpha * acc_scr[h] + _dot(vts[h], p)
            m_scr[h] = m_next

    def finish(h):
        acc = acc_scr[h]
        return acc[0:HEAD_DIM] * (1.0 / acc[HEAD_DIM:HEAD_DIM + 1])

    def gate_row(h, br):
        return jnp.concatenate([gt_t[(h * NSA_GROUP + g) * 3 + br:(h * NSA_GROUP + g) * 3 + br + 1, :]
                                for g in range(NSA_GROUP)], axis=1)

    for h in heads:
        in_head = (lane >> _LOG_HEAD) == h
        q4 = jnp.concatenate(
            [jnp.where(in_head, q[:, g * LANES:(g + 1) * LANES], jnp.zeros((), BF16)) for g in range(NSA_GROUP)],
            axis=0)
        q4_scr[h] = q4

        s = _dot_nt(kc_ref[...], q4)
        n_col = lax.broadcasted_iota(jnp.int32, (n_cmp_rows, 1), 0)
        vis = (n_col * CMP_STRIDE + (CMP_LEN - 1)) <= t_row
        s = s + tile_lanes(jnp.where(vis, 0.0, NEG))
        m = jnp.max(s, axis=0, keepdims=True)
        e = jnp.exp(s - m)
        any_vis = tile_lanes(jnp.where(t_row >= CMP_LEN - 1, 1.0, 0.0))
        p = e * (any_vis / jnp.maximum(jnp.sum(e, axis=0, keepdims=True), 1e-30))
        o_cmp = _dot(vct_ref[...], p.astype(BF16))
        out_scr[h] = gate_row(h, 0) * o_cmp[h * HEAD_DIM:(h + 1) * HEAD_DIM]

        p_sum = p[:, 0:tq]
        for g in range(1, NSA_GROUP):
            p_sum = p_sum + p[:, g * tq:(g + 1) * tq]
        ov_t = ovt_ref[...]
        p_sel_t = sum(_dot(ov_t, part) for part in _split3(p_sum))
        blk = lax.broadcasted_iota(jnp.int32, (n_sel, 1), 0)
        t_blk = t_row >> _LOG_SEL
        valid = blk <= t_blk
        forced = (blk == 0) | (blk == t_blk) | (blk == t_blk - 1)
        rank_t = jnp.where(valid, p_sel_t + FORCE_BONUS * jnp.where(forced, 1.0, 0.0), -FORCE_BONUS)
        n_tiles = n_sel // SUBLANES
        rank_tiles = [rank_t[r * SUBLANES:(r + 1) * SUBLANES] for r in range(n_tiles)]
        blk8 = lax.broadcasted_iota(jnp.int32, (SUBLANES, 1), 0)
        cnt_scr[...] = jnp.zeros_like(cnt_scr)
        last_blk = (t0 + tq - 1) >> _LOG_SEL

        def count_group(grp):
            part = [jnp.zeros((SUBLANES, tq), F32) for _ in range(n_tiles)]
            for ii in range(grp * SUBLANES, (grp + 1) * SUBLANES):
                ri = rank_t[ii:ii + 1, :]
                for r in range(n_tiles):
                    if ii < r * SUBLANES:
                        ahead = ri >= rank_tiles[r]
                    elif ii >= (r + 1) * SUBLANES:
                        ahead = ri > rank_tiles[r]
                    else:
                        tie_ok = jnp.where(blk8 + r * SUBLANES > ii, 1.0, 0.0)
                        ahead = (ri > rank_tiles[r]) | ((ri == rank_tiles[r]) & (tie_ok > 0.5))
                    part[r] = part[r] + jnp.where(ahead, 1.0, 0.0)
            for r in range(n_tiles):
                rows = slice(r * SUBLANES, (r + 1) * SUBLANES)
                cnt_scr[rows, :] = cnt_scr[rows, :] + part[r]

        for grp in range(n_tiles):
            pl.when(grp * SUBLANES <= last_blk)(functools.partial(count_group, grp))
        top = float(min(SEL_TOPK, n_sel))
        sel_bias = jnp.concatenate(
            [jnp.where(cnt_scr[...] < top, 0.0, NEG), jnp.zeros((LANES - n_sel, tq), F32)], axis=0)
        mask_lanes = sel_bias.T
        if h == 0:
            mask_lanes = pltpu.roll(mask_lanes, HEAD_DIM, 1)
        mask_lanes = mask_lanes.astype(BF16)
        q4s_scr[h] = jnp.concatenate(
            [jnp.where(in_head, q[:, g * LANES:(g + 1) * LANES], mask_lanes) for g in range(NSA_GROUP)], axis=0)

    init_state()
    tk = NSA_TK_SEL
    n_full = t0 // tk

    def sel_tile(k0, size, bias):
        online_tile([ksa_ref[h, pl.ds(k0, size), :] for h in heads], q4s_scr,
                    [vsa_ref[h, :, pl.ds(k0, size)] for h in heads], bias)

    def sel_body(j, _):
        sel_tile(pl.multiple_of(j * 2 * tk, 2 * tk), 2 * tk, None)
        return 0

    lax.fori_loop(0, n_full // 2, sel_body, 0)

    @pl.when(n_full % 2 == 1)
    def _():
        sel_tile(pl.multiple_of((n_full - 1) * tk, tk), tk, None)

    k_diag = pl.multiple_of(n_full * tk, tk)
    kpos_sel = k_diag + lax.broadcasted_iota(jnp.int32, (tk, 1), 0)
    sel_tile(k_diag, tk, jnp.where(kpos_sel <= t_row, 0.0, NEG))
    for h in heads:
        out_scr[h] = out_scr[h] + gate_row(h, 1) * finish(h)

    init_state()
    tkw = WINDOW + tq
    k0 = pl.multiple_of(jnp.maximum(t0 - WINDOW, 0), tq)
    dist = t_row - (k0 + lax.broadcasted_iota(jnp.int32, (tkw, 1), 0))
    kt = kw_ref[pl.ds(k0, tkw), :]
    online_tile([kt, kt], q4_scr, [vwa_ref[h, :, pl.ds(k0, tkw)] for h in heads],
                jnp.where((dist >= 0) & (dist < WINDOW), 0.0, NEG))
    both = jnp.concatenate([out_scr[h] + gate_row(h, 2) * finish(h) for h in heads], axis=0)
    for g in range(NSA_GROUP):
        o_ref[:, g * LANES:(g + 1) * LANES] = both[:, g * tq:(g + 1) * tq].T.astype(o_ref.dtype)


def _nsa(q3, kc, vc_t, ks_aug, vs_aug, kw, vw_aug, gt, ov_t):
    bsz, seq, qw = q3.shape
    tq = NSA_TQ
    n_rows = kc.shape[1]
    nh = NSA_KV_HEADS
    full = pl.BlockSpec((None, seq, LANES), lambda b, i: (b, 0, 0))
    head_tok = pl.BlockSpec((None, nh, seq, LANES), lambda b, i: (b, 0, 0, 0))
    head_feat = pl.BlockSpec((None, nh, LANES, seq), lambda b, i: (b, 0, 0, 0))
    tile = lambda w: pl.BlockSpec((None, tq, w), lambda b, i: (b, i, 0))
    cols = NSA_GROUP * tq
    return pl.pallas_call(
        functools.partial(_nsa_kernel, seq=seq),
        grid=(bsz, seq // tq),
        in_specs=[tile(qw),
                  pl.BlockSpec((None, n_rows, LANES), lambda b, i: (b, 0, 0)),
                  pl.BlockSpec((None, LANES, n_rows), lambda b, i: (b, 0, 0)),
                  head_tok, head_feat, full, head_feat, tile(LANES), _const_spec(ov_t.shape)],
        out_specs=tile(qw),
        out_shape=jax.ShapeDtypeStruct((bsz, seq, qw), BF16),
        scratch_shapes=[pltpu.VMEM((nh, 1, cols), F32),
                        pltpu.VMEM((nh, LANES, cols), F32),
                        pltpu.VMEM((nh, HEAD_DIM, cols), F32),
                        pltpu.VMEM((nh, cols, LANES), BF16),
                        pltpu.VMEM((nh, cols, LANES), BF16),
                        pltpu.VMEM((seq // SEL_LEN, tq), F32)],
        compiler_params=_cparams(("parallel", "arbitrary")),
        name="nsa_attention",
    )(q3, kc, vc_t, ks_aug, vs_aug, kw, vw_aug, gt, ov_t)


def _sb_kernel(q_ref, k_ref, vt_ref, o_ref, r_scr, acc_scr):
    tq = q_ref.shape[0]
    tk = SB_TILES * tq
    i_tile = pl.program_id(1)
    t0 = i_tile * tq
    q = q_ref[...]
    lane = lax.broadcasted_iota(jnp.int32, (1, LANES), 1)
    feat = lax.broadcasted_iota(jnp.int32, (LANES, 1), 0)
    t_row = t0 + lax.broadcasted_iota(jnp.int32, (1, tq), 1)
    kpos_col = lax.broadcasted_iota(jnp.int32, (tk, 1), 0)
    sr = lax.broadcasted_iota(jnp.int32, (tq, tq), 0)
    sc = lax.broadcasted_iota(jnp.int32, (tq, tq), 1)
    later = jnp.where(sc > sr, 1.0, 0.0).astype(BF16)
    qm = []
    for h in range(SB_HEADS):
        in_head = (lane >> _LOG_HEAD) == (h % 2)
        pair = q[:, (h // 2) * LANES:(h // 2 + 1) * LANES]
        qm.append(jnp.where(in_head, pair, jnp.zeros((), BF16)))
    r_scr[...] = jnp.zeros_like(r_scr)
    acc_scr[...] = jnp.zeros_like(acc_scr)

    def cond(c):
        j, r_max = c
        return (j >= 0) & (r_max > SB_EXIT)

    def body(c):
        j, _ = c
        k0 = pl.multiple_of(jnp.maximum(j + 1 - SB_TILES, 0) * tq, tq)
        kt = k_ref[pl.ds(k0, tk), :]
        z = jnp.concatenate([_dot_nt(kt[:, (h // 2) * LANES:(h // 2 + 1) * LANES], qm[h])
                             for h in range(SB_HEADS)], axis=1)
        kpos = k0 + kpos_col
        hidden = jnp.where((kpos < t_row) & (kpos < (j + 1) * tq), 0.0, NEG)
        z = z + jnp.concatenate([hidden] * SB_HEADS, axis=1)
        ls = jnp.minimum(z, 0.0) - jnp.log(1.0 + jnp.exp(-jnp.abs(z)))
        rest = ls - z
        rest_b = rest.astype(BF16)
        r_prev = r_scr[...]
        totals = [jnp.sum(rest[a * tq:(a + 1) * tq], axis=0, keepdims=True) for a in range(SB_TILES)]
        after_tiles = []
        right = r_prev
        for a in reversed(range(SB_TILES)):
            rows = slice(a * tq, (a + 1) * tq)
            after_tiles.append(_dot(later, rest_b[rows]) + right)
            right = right + totals[a]
        after = jnp.concatenate(after_tiles[::-1], axis=0)
        w = jnp.exp(ls + after).astype(BF16)
        for h in range(SB_HEADS):
            cols = slice(h * tq, (h + 1) * tq)
            vt = vt_ref[(h // 2) * LANES:(h // 2 + 1) * LANES, pl.ds(k0, tk)]
            acc_scr[:, cols] = acc_scr[:, cols] + _dot(vt, w[:, cols])
        r_new = right
        r_scr[...] = r_new
        return j - SB_TILES, jnp.max(r_new)

    lax.while_loop(cond, body, (i_tile, jnp.zeros((), F32)))
    for hp in range(SB_HEADS // 2):
        even = acc_scr[:, (2 * hp) * tq:(2 * hp + 1) * tq]
        odd = acc_scr[:, (2 * hp + 1) * tq:(2 * hp + 2) * tq]
        o_ref[:, hp * LANES:(hp + 1) * LANES] = jnp.where(feat < HEAD_DIM, even, odd).T.astype(o_ref.dtype)


def _sb(q3, k3, v_t):
    bsz, seq, w = q3.shape
    tq = SB_TQ
    full = pl.BlockSpec((None, seq, w), lambda b, i: (b, 0, 0))
    full_t = pl.BlockSpec((None, w, seq), lambda b, i: (b, 0, 0))
    tile = pl.BlockSpec((None, tq, w), lambda b, i: (b, i, 0))
    return pl.pallas_call(
        _sb_kernel,
        grid=(bsz, seq // tq),
        in_specs=[tile, full, full_t],
        out_specs=tile,
        out_shape=jax.ShapeDtypeStruct((bsz, seq, w), BF16),
        scratch_shapes=[pltpu.VMEM((1, SB_HEADS * tq), F32), pltpu.VMEM((LANES, SB_HEADS * tq), F32)],
        compiler_params=_cparams(("parallel", "arbitrary")),
        name="sb_attention",
    )(q3, k3, v_t)


def _merge_kernel(x_ref, g_ref, a_ref, b_ref, c_ref, wg_ref, wb_ref, wo_ref, o_ref):
    x = x_ref[...]
    d = x.shape[1]
    h = _rms(x, g_ref[...]).astype(BF16)
    mixed = None
    for n, br in enumerate((a_ref, b_ref, c_ref)):
        gate = jax.nn.sigmoid(_dot(h, wg_ref[:, n * d:(n + 1) * d]))
        y = gate * _dot(br[...], wb_ref[n])
        mixed = y if mixed is None else mixed + y
    o_ref[...] = x + _dot(mixed.astype(BF16), wo_ref[...])


def _merge(x2, layer, g, o_a, o_b, o_c, w_gate, w_branch, w_out):
    t, d = x2.shape
    tm = min(TM_PROJ, t)
    row = lambda i: (i, 0)
    bw = o_a.shape[1]
    return pl.pallas_call(
        _merge_kernel,
        grid=(t // tm,),
        in_specs=[pl.BlockSpec((tm, d), row), _layer_spec(g, layer),
                  pl.BlockSpec((tm, bw), row), pl.BlockSpec((tm, bw), row), pl.BlockSpec((tm, bw), row),
                  _layer_spec(w_gate, layer), _layer_spec(w_branch, layer), _layer_spec(w_out, layer)],
        out_specs=pl.BlockSpec((tm, d), row),
        out_shape=jax.ShapeDtypeStruct((t, d), F32),
        compiler_params=_cparams(("parallel",)),
        name="gated_merge",
    )(x2, g, o_a, o_b, o_c, w_gate, w_branch, w_out)


def _ffn_kernel(x_ref, halo_ref, g_ref, wup_ref, cw_ref, cb_ref, wdn_ref, gout_ref, o_ref, act_scr, *,
                tiles_per_seq, norm_out):
    x = x_ref[...]
    tm = x.shape[0]
    d_ff = wdn_ref.shape[0]
    at_start = (pl.program_id(0) % tiles_per_seq) == 0
    g = g_ref[...]
    h_halo = jnp.where(at_start, 0.0, _rms(halo_ref[...], g))
    h = jnp.concatenate([h_halo, _rms(x, g)], axis=0).astype(BF16)
    for f in range(d_ff // FF_TILE):
        halves = []
        for base in (0, d_ff):
            cols = slice(base + f * FF_TILE, base + (f + 1) * FF_TILE)
            u = _dot(h, wup_ref[:, cols])
            w = cw_ref[:, cols]
            conv = (w[2:3] * u[SUBLANES:]
                    + w[1:2] * pltpu.roll(u, 1, 0)[SUBLANES:]
                    + w[0:1] * pltpu.roll(u, 2, 0)[SUBLANES:]
                    + cb_ref[:, cols])
            halves.append(conv)
        act_scr[:, f * FF_TILE:(f + 1) * FF_TILE] = (jax.nn.silu(halves[0]) * halves[1]).astype(BF16)
    y = x + _dot(act_scr[...], wdn_ref[...])
    o_ref[...] = _rms(y, gout_ref[...]) if norm_out else y


def _ffn(x2, layer, g, w_up, conv_w, conv_b, w_down, g_out, seq, norm_out):
    t, d = x2.shape
    tm = min(TM_FFN, seq)
    tiles_per_seq = seq // tm
    row = lambda i: (i, 0)
    halo = lambda i: (jnp.maximum(i * (tm // SUBLANES) - 1, 0), 0)
    return pl.pallas_call(
        functools.partial(_ffn_kernel, tiles_per_seq=tiles_per_seq, norm_out=norm_out),
        grid=(t // tm,),
        in_specs=[pl.BlockSpec((tm, d), row), pl.BlockSpec((SUBLANES, d), halo), _layer_spec(g, layer),
                  _layer_spec(w_up, layer), _layer_spec(conv_w, layer), _layer_spec(conv_b, layer),
                  _layer_spec(w_down, layer), _const_spec(g_out.shape)],
        out_specs=pl.BlockSpec((tm, d), row),
        out_shape=jax.ShapeDtypeStruct((t, d), F32),
        scratch_shapes=[pltpu.VMEM((tm, w_down.shape[1]), BF16)],
        compiler_params=_cparams(("parallel",)),
        name="conv_ffn",
    )(x2, x2, g, w_up, conv_w, conv_b, w_down, g_out)


def _norm_kernel(x_ref, g_ref, o_ref):
    o_ref[...] = _rms(x_ref[...], g_ref[...])


def _final_norm(x2, g):
    t, d = x2.shape
    tm = min(TM_PROJ, t)
    return pl.pallas_call(
        _norm_kernel,
        grid=(t // tm,),
        in_specs=[pl.BlockSpec((tm, d), lambda i: (i, 0)), _const_spec((1, d))],
        out_specs=pl.BlockSpec((tm, d), lambda i: (i, 0)),
        out_shape=jax.ShapeDtypeStruct((t, d), F32),
        compiler_params=_cparams(("parallel",)),
        name="final_norm",
    )(x2, g)


def _nsa_slot_order():
    return [NSA_GROUP * (s % 2) + s // 2 for s in range(NSA_HEADS)]


def _pack_w_in(w):
    d = w.shape[0]
    o_q, o_kv, o_g, o_sb, o_gm = 512, 1024, 1792, 1816, 3352
    kv_w = NSA_KV_HEADS * HEAD_DIM
    sb_w = SB_HEADS * HEAD_DIM
    u = w[:, 0:o_q]
    q = [w[:, o_q + hq * HEAD_DIM:o_q + (hq + 1) * HEAD_DIM] for hq in _nsa_slot_order()]
    k = [w[:, o_kv + br * 2 * kv_w:o_kv + br * 2 * kv_w + kv_w] for br in range(3)]
    v = [w[:, o_kv + br * 2 * kv_w + kv_w:o_kv + (br + 1) * 2 * kv_w] for br in range(3)]
    gates = [w[:, o_g:o_sb], jnp.zeros((d, LANES - (o_sb - o_g)), w.dtype)]
    packed = jnp.concatenate([u] + q + k + [v[0]] + gates + [w[:, o_sb:o_sb + 2 * sb_w]], axis=1).astype(BF16)
    w_vt = jnp.concatenate([v[1], v[2], w[:, o_sb + 2 * sb_w:o_gm]], axis=1).astype(BF16).T
    return packed, w_vt, w[:, o_gm:].astype(BF16)


def _rope_tables(seq):
    inv_freq = 1.0 / (ROPE_THETA ** (jnp.arange(0, HEAD_DIM, 2, dtype=F32) / HEAD_DIM))
    ang = jnp.arange(seq, dtype=F32)[:, None] * inv_freq[None, :]
    cos, sin = jnp.cos(ang), jnp.sin(ang)
    cos_t = jnp.tile(cos, (1, LANES // (HEAD_DIM // 2)))
    sin_t = jnp.tile(jnp.concatenate([-sin, sin], axis=1), (1, LANES // HEAD_DIM))
    return cos_t, sin_t


def _overlap_t(seq):
    n_rows = seq // CMP_STRIDE
    n_sel = seq // SEL_LEN
    cmp_start = jnp.arange(n_rows) * CMP_STRIDE
    sel_start = jnp.arange(n_sel) * SEL_LEN
    ov = ((cmp_start[None, :] < sel_start[:, None] + SEL_LEN)
          & (cmp_start[None, :] + CMP_LEN > sel_start[:, None])
          & (jnp.arange(n_rows)[None, :] < (seq - CMP_LEN) // CMP_STRIDE + 1))
    return ov.astype(BF16)


def kernel(x, norm_mix, w_in, ssm_lam_re, ssm_lam_im, ssm_log_step, ssm_b_re, ssm_b_im, ssm_c_re, ssm_c_im, ssm_d, ssm_w_glu, cmp_w1_k, cmp_w2_k, cmp_pos_k, cmp_w1_v, cmp_w2_v, cmp_pos_v, w_branch, w_out, norm_ffn, ffn_w_up, ffn_conv_w, ffn_conv_b, ffn_w_down, norm_final):
    bsz, seq, d = x.shape
    depth = w_in.shape[0]
    t = bsz * seq
    assert seq % 512 == 0 and seq // SEL_LEN <= HEAD_DIM
    assert seq >= WINDOW + NSA_TQ
    if depth == 0:
        return _final_norm(x.reshape(t, d).astype(F32), norm_final.reshape(1, d)).reshape(x.shape).astype(x.dtype)
    cos_t, sin_t = _rope_tables(seq)
    ov_t = _overlap_t(seq)
    w_packed, w_vt, w_gate = jax.vmap(_pack_w_in)(w_in)
    s5_tabs = jax.vmap(_s5_tables)(ssm_lam_re, ssm_lam_im, ssm_log_step, ssm_b_re, ssm_b_im, ssm_c_re, ssm_c_im)
    s5_d = ssm_d.reshape(depth, 1, -1).astype(F32)
    s5_glu = ssm_w_glu.astype(BF16)
    cmp_k = jax.vmap(_compress_weights)(cmp_w1_k, cmp_w2_k, cmp_pos_k)
    cmp_v = jax.vmap(functools.partial(_compress_weights, feature_major_out=True))(cmp_w1_v, cmp_w2_v, cmp_pos_v)
    wb_nsa = jnp.concatenate([w_branch[:, 1, hq * HEAD_DIM:(hq + 1) * HEAD_DIM] for hq in _nsa_slot_order()], axis=1)
    wb_all = jnp.stack([w_branch[:, 0], wb_nsa, w_branch[:, 2]], axis=1).astype(BF16)
    wo_all = w_out.astype(BF16)
    g_mix = norm_mix.reshape(depth, 1, d).astype(F32)
    g_ffn = norm_ffn.reshape(depth, 1, d).astype(F32)
    up_all = ffn_w_up.astype(BF16)
    down_all = ffn_w_down.astype(BF16)
    conv_w = ffn_conv_w.astype(F32)
    conv_b = ffn_conv_b.reshape(depth, 1, -1).astype(F32)
    g_final = norm_final.reshape(1, d).astype(F32)

    x2 = x.reshape(t, d).astype(F32)
    b3 = lambda a: a.reshape(bsz, seq, a.shape[-1])
    rows16 = lambda a: a.reshape(bsz, seq // CMP_STRIDE, a.shape[-1])
    for l in range(depth):
        (u, q, kc_raw, ks_aug, kw, vc_raw, vs_aug, vw_aug, gt, sq, sk, sv_t) = _proj_in(
            x2, l, g_mix, w_packed, w_vt, cos_t, sin_t, bsz, seq)
        o_ssm = _s5(b3(u), l, s5_tabs, s5_d, s5_glu)
        kc, vc_t = _compress(rows16(kc_raw), rows16(vc_raw), l, cmp_k, cmp_v)
        o_nsa = _nsa(b3(q), kc, vc_t, ks_aug, vs_aug, b3(kw), vw_aug, b3(gt), ov_t)
        o_sb = _sb(b3(sq), b3(sk), sv_t)
        x2 = _merge(x2, l, g_mix, o_ssm.reshape(t, -1), o_nsa.reshape(t, -1), o_sb.reshape(t, -1),
                    w_gate, wb_all, wo_all)
        x2 = _ffn(x2, l, g_ffn, up_all, conv_w, conv_b, down_all, g_final, seq, norm_out=(l == depth - 1))
    return x2.reshape(bsz, seq, d).astype(x.dtype)
```

```python
import functools
import math

import jax
import jax.numpy as jnp
from jax import lax
from jax.experimental import pallas as pl
from jax.experimental.pallas import tpu as pltpu

F32 = jnp.float32
BF16 = jnp.bfloat16

HEAD_DIM = 64
RMS_EPS = 1e-6
ROPE_THETA = 10000.0
SSM_GROUP = 16
SSM_STATE = 64
NSA_HEADS = 8
NSA_KV_HEADS = 2
NSA_GROUP = NSA_HEADS // NSA_KV_HEADS
CMP_LEN = 32
CMP_STRIDE = 16
CMP_HIDDEN = 128
SEL_LEN = 64
SEL_TOPK = 16
WINDOW = 512
FORCE_BONUS = 1e6
SB_HEADS = 8
CONV_WIDTH = 3
BRANCH_WIDTH = 512

LANES = 128
SUBLANES = 8
VMEM_LIMIT = 56 * 1024 * 1024

TM_PROJ = 512
TM_FFN = 512
FF_TILE = 256
S5_CHUNK = 1024
S5_LANES = 256
S5_MXU_SHIFTS = 4
NSA_TQ = 512
NSA_TK_SEL = 512
SB_TQ = 256
SB_TILES = 2

_LOG_HEAD = HEAD_DIM.bit_length() - 1
_LOG_SEL = SEL_LEN.bit_length() - 1
NEG = -1e30
SB_EXIT = -104.0


def _cparams(sem):
    return pltpu.CompilerParams(dimension_semantics=sem, vmem_limit_bytes=VMEM_LIMIT)


def _const_spec(shape):
    n = len(shape)
    return pl.BlockSpec(shape, lambda *_: (0,) * n, pipeline_mode=pl.Buffered(1))


def _layer_spec(stacked, layer):
    n = stacked.ndim - 1
    return pl.BlockSpec((None,) + stacked.shape[1:], lambda *_: (layer,) + (0,) * n,
                        pipeline_mode=pl.Buffered(1))


def _rms(x, g):
    return x * lax.rsqrt(jnp.mean(x * x, axis=-1, keepdims=True) + RMS_EPS) * g


def _dot(a, b):
    return jnp.dot(a, b, preferred_element_type=F32)


def _dot_nt(a, b):
    return lax.dot_general(a, b, (((1,), (1,)), ((), ())), preferred_element_type=F32)


def _split3(x):
    h = x.astype(BF16)
    r = x - h.astype(F32)
    m = r.astype(BF16)
    lo = (r - m.astype(F32)).astype(BF16)
    return h, m, lo


_C_U = 0
_C_ROPE = 512
_C_V = 1408
_C_G = 1536
_C_SB = 1664
_C_END = 2688
_R_VS, _R_VW, _R_SV, _R_END = 0, 128, 256, 768


def _proj_kernel(x_ref, g_ref, w_ref, wvt_ref, cos_ref, sin_ref,
                 u_ref, q_ref, kc_ref, ksa_ref, kw_ref, vc_ref, vsa_ref, vwa_ref, gt_ref,
                 sq_ref, sk_ref, svt_ref, rows_scr, *, n_pos):
    tm = x_ref.shape[0]
    h = _rms(x_ref[...], g_ref[...]).astype(BF16)

    def store_rows16(val, out_ref):
        rows_scr[...] = val
        for b in range(CMP_STRIDE):
            out_ref[:, b * LANES:(b + 1) * LANES] = rows_scr[pl.ds(b, tm // CMP_STRIDE, stride=CMP_STRIDE), :]

    def mm(a, b):
        return _dot(h, w_ref[:, a:b])

    vt = _dot_nt(wvt_ref[...], h)
    ones_rows = jnp.where(lax.broadcasted_iota(jnp.int32, (HEAD_DIM, tm), 0) == 0, 1.0, 0.0)
    for hh in range(NSA_KV_HEADS):
        vsa_ref[hh] = jnp.concatenate(
            [vt[_R_VS + hh * HEAD_DIM:_R_VS + (hh + 1) * HEAD_DIM], ones_rows], axis=0).astype(BF16)
        vwa_ref[hh] = jnp.concatenate(
            [vt[_R_VW + hh * HEAD_DIM:_R_VW + (hh + 1) * HEAD_DIM], ones_rows], axis=0).astype(BF16)
    svt_ref[...] = vt[_R_SV:_R_END].astype(BF16)

    u_ref[...] = mm(_C_U, _C_ROPE)

    r = mm(_C_ROPE, _C_V)
    cos = cos_ref[...]
    sin = sin_ref[...]
    lane = lax.broadcasted_iota(jnp.int32, cos.shape, 1)
    first = (lane & (HEAD_DIM - 1)) < (HEAD_DIM // 2)
    roped = []
    for c in range((_C_V - _C_ROPE) // LANES):
        rc = r[:, c * LANES:(c + 1) * LANES]
        partner = jnp.where(first, pltpu.roll(rc, LANES - HEAD_DIM // 2, 1),
                            pltpu.roll(rc, HEAD_DIM // 2, 1))
        roped.append(rc * cos + partner * sin)
    scale = HEAD_DIM ** -0.5
    for c in range(4):
        q_ref[:, c * LANES:(c + 1) * LANES] = (roped[c] * scale).astype(BF16)
    store_rows16(roped[4], kc_ref)
    kw_ref[...] = roped[6].astype(BF16)
    pos = (pl.program_id(0) % n_pos) * tm + lax.broadcasted_iota(jnp.int32, (tm, 1), 0)
    blk = pos >> _LOG_SEL
    lane_k = lax.broadcasted_iota(jnp.int32, (1, LANES), 1)
    for hh in range(NSA_KV_HEADS):
        own = (lane_k >> _LOG_HEAD) == hh
        one_hot = jnp.where(blk == (lane_k & (HEAD_DIM - 1)), 1.0, 0.0)
        ksa_ref[hh] = jnp.where(own, roped[5], one_hot).astype(BF16)

    store_rows16(mm(_C_V, _C_G), vc_ref)
    gt_ref[...] = jax.nn.sigmoid(mm(_C_G, _C_SB))

    sb = mm(_C_SB, _C_END)
    sq_ref[...] = (sb[:, 0:512] * scale).astype(BF16)
    sk_ref[...] = sb[:, 512:1024].astype(BF16)


def _proj_in(x2, layer, g, w_packed, w_vt, cos_t, sin_t, bsz, seq):
    t, d = x2.shape
    tm = min(TM_PROJ, seq)
    n_pos = seq // tm
    row = lambda i: (i, 0)
    pos = lambda i: (i % n_pos, 0)
    nh = NSA_KV_HEADS

    def tok(w, dt):
        return pl.BlockSpec((tm, w), row), jax.ShapeDtypeStruct((t, w), dt)

    def feat_major(w, dt):
        return (pl.BlockSpec((None, w, tm), lambda i: (i // n_pos, 0, i % n_pos)),
                jax.ShapeDtypeStruct((bsz, w, seq), dt))

    def head_tok(dt):
        return (pl.BlockSpec((None, nh, tm, LANES), lambda i: (i // n_pos, 0, i % n_pos, 0)),
                jax.ShapeDtypeStruct((bsz, nh, seq, LANES), dt))

    def head_feat(dt):
        return (pl.BlockSpec((None, nh, LANES, tm), lambda i: (i // n_pos, 0, 0, i % n_pos)),
                jax.ShapeDtypeStruct((bsz, nh, LANES, seq), dt))

    def rows16():
        return (pl.BlockSpec((tm // CMP_STRIDE, CMP_STRIDE * LANES), row),
                jax.ShapeDtypeStruct((t // CMP_STRIDE, CMP_STRIDE * LANES), F32))

    outs = [tok(512, F32), tok(512, BF16), rows16(), head_tok(BF16), tok(128, BF16),
            rows16(), head_feat(BF16), head_feat(BF16), tok(128, F32),
            tok(512, BF16), tok(512, BF16), feat_major(512, BF16)]
    return pl.pallas_call(
        functools.partial(_proj_kernel, n_pos=n_pos),
        grid=(t // tm,),
        in_specs=[pl.BlockSpec((tm, d), row),
                  _layer_spec(g, layer),
                  _layer_spec(w_packed, layer),
                  _layer_spec(w_vt, layer),
                  pl.BlockSpec((tm, LANES), pos),
                  pl.BlockSpec((tm, LANES), pos)],
        out_specs=[spec for spec, _ in outs],
        out_shape=[shape for _, shape in outs],
        scratch_shapes=[pltpu.VMEM((tm, LANES), F32)],
        compiler_params=_cparams(("parallel",)),
        name="proj_in",
    )(x2, g, w_packed, w_vt, cos_t, sin_t)


def _s5_kernel(u_ref, b8_ref, ccat_ref, pw_re_ref, pw_im_ref, cy_re_ref, cy_im_ref, d_ref, wglu_ref, o_ref,
               car_re, car_im):
    @pl.when(pl.program_id(1) == 0)
    def _():
        car_re[...] = jnp.zeros_like(car_re)
        car_im[...] = jnp.zeros_like(car_im)

    u = u_ref[...]
    chunk, width = u.shape
    n_tiles = b8_ref.shape[0]
    n_sub = chunk // SUBLANES
    u3 = u.reshape(n_sub, SUBLANES, width)
    row8 = lax.broadcasted_iota(jnp.int32, (1, SUBLANES, 1), 1)
    shifted = [u.astype(BF16)]
    for s in range(1, S5_MXU_SHIFTS):
        shifted.append(jnp.where(row8 >= s, pltpu.roll(u3, s, 1), 0.0).reshape(chunk, width).astype(BF16))
    ucat = [jnp.concatenate([sh[:, b * LANES:(b + 1) * LANES] for sh in shifted], axis=1)
            for b in range(width // LANES)]
    y_blocks = []
    for blk in range(n_tiles // 2):
        pair = (2 * blk, 2 * blk + 1)
        hr, hi = {}, {}
        for j in pair:
            x = _dot(ucat[blk], b8_ref[j])
            xr = x[:, :S5_LANES].reshape(n_sub, SUBLANES, S5_LANES)
            xi = x[:, S5_LANES:].reshape(n_sub, SUBLANES, S5_LANES)
            sr = pltpu.roll(xr, S5_MXU_SHIFTS, 1)
            si = pltpu.roll(xi, S5_MXU_SHIFTS, 1)
            cr = pw_re_ref[j]
            ci = pw_im_ref[j]
            hr[j], hi[j] = xr + cr * sr - ci * si, xi + cr * si + ci * sr
        c_re = {j: car_re[j:j + 1, :] for j in pair}
        c_im = {j: car_im[j:j + 1, :] for j in pair}
        out_re = {j: [] for j in pair}
        out_im = {j: [] for j in pair}
        for v in range(n_sub):
            for j in pair:
                pr = cy_re_ref[j]
                pi = cy_im_ref[j]
                t_re = hr[j][v] + pr * c_re[j] - pi * c_im[j]
                t_im = hi[j][v] + pr * c_im[j] + pi * c_re[j]
                c_re[j] = t_re[SUBLANES - 1:SUBLANES, :]
                c_im[j] = t_im[SUBLANES - 1:SUBLANES, :]
                out_re[j].append(t_re)
                out_im[j].append(t_im)
        y_blk = None
        for j in pair:
            car_re[j:j + 1, :] = c_re[j]
            car_im[j:j + 1, :] = c_im[j]
            hcat = jnp.concatenate([jnp.concatenate(out_re[j], axis=0), jnp.concatenate(out_im[j], axis=0)],
                                   axis=1).astype(BF16)
            yj = _dot(hcat, ccat_ref[j])
            y_blk = yj if y_blk is None else y_blk + yj
        y_blocks.append(y_blk)
    y = jnp.concatenate(y_blocks, axis=1) + d_ref[...] * u
    z = jax.nn.gelu(y)
    o_ref[...] = (z * jax.nn.sigmoid(_dot(z.astype(BF16), wglu_ref[...]))).astype(o_ref.dtype)


def _s5_tables(lam_re, lam_im, log_step, b_re, b_im, c_re, c_im):
    g_n, p_n = lam_re.shape
    i_n = b_re.shape[-1]
    width = g_n * i_n
    lanes_total = g_n * p_n
    n_tiles = lanes_total // S5_LANES
    step = jnp.exp(log_step.astype(F32))[:, None]
    lr = lam_re.astype(F32)
    li = lam_im.astype(F32)
    mag = jnp.exp(lr * step)
    a_re = mag * jnp.cos(li * step)
    a_im = mag * jnp.sin(li * step)
    den = lr * lr + li * li
    nr = a_re - 1.0
    f_re = (nr * lr + a_im * li) / den
    f_im = (a_im * lr - nr * li) / den
    br = b_re.astype(F32)
    bi = b_im.astype(F32)
    bb_re = f_re[..., None] * br - f_im[..., None] * bi
    bb_im = f_re[..., None] * bi + f_im[..., None] * br
    log_mag = (lr * step).reshape(1, lanes_total)
    ang = (li * step).reshape(1, lanes_total)

    def powers(n):
        m = jnp.exp(n * log_mag)
        return m * jnp.cos(n * ang), m * jnp.sin(n * ang)

    gpt = S5_LANES // p_n
    gpb = LANES // i_n
    tile_id = jnp.arange(n_tiles)
    in_tile = (jnp.arange(gpb)[None, :, None]
               == (tile_id[:, None, None] % (gpb // gpt)) * gpt + jnp.arange(gpt)[None, None, :]).astype(F32)

    n_sh = S5_MXU_SHIFTS
    as_re, as_im = powers(jnp.arange(n_sh, dtype=F32)[:, None])
    as_re = as_re.reshape(n_sh, g_n, p_n, 1)
    as_im = as_im.reshape(n_sh, g_n, p_n, 1)
    w_re = as_re * bb_re[None] - as_im * bb_im[None]
    w_im = as_re * bb_im[None] + as_im * bb_re[None]

    def in_tiles(w):
        w = w.reshape(n_sh, n_tiles, gpt, p_n, i_n).transpose(1, 0, 4, 2, 3)
        out = in_tile[:, None, :, None, :, None] * w[:, :, None, :, :, :]
        return out.reshape(n_tiles, n_sh * LANES, S5_LANES)

    b8 = jnp.concatenate([in_tiles(w_re), in_tiles(w_im)], axis=2).astype(BF16)

    def out_tiles(c):
        c = c.astype(F32).reshape(n_tiles, gpt, i_n, p_n).transpose(0, 1, 3, 2)
        out = in_tile.transpose(0, 2, 1)[:, :, None, :, None] * c[:, :, :, None, :]
        return out.reshape(n_tiles, S5_LANES, LANES)

    ccat = jnp.concatenate([out_tiles(c_re), -out_tiles(c_im)], axis=1).astype(BF16)

    cy_re, cy_im = powers(jnp.arange(1, SUBLANES + 1, dtype=F32)[:, None])

    def tiles(a):
        return a.reshape(a.shape[0], n_tiles, S5_LANES).transpose(1, 0, 2)

    live = (jnp.arange(SUBLANES) >= n_sh).astype(F32)[None, :, None]
    pw_re, pw_im = powers(jnp.full((1, 1), float(n_sh), F32))
    return b8, ccat, tiles(pw_re) * live, tiles(pw_im) * live, tiles(cy_re), tiles(cy_im)


def _s5(u3, layer, tables, d_row, wglu):
    bsz, seq, width = u3.shape
    consts = tuple(tables) + (d_row, wglu)
    chunk = min(S5_CHUNK, seq)
    n_tiles = tables[0].shape[1]
    return pl.pallas_call(
        _s5_kernel,
        grid=(bsz, seq // chunk),
        in_specs=[pl.BlockSpec((None, chunk, width), lambda b, c: (b, c, 0))]
                 + [_layer_spec(a, layer) for a in consts],
        out_specs=pl.BlockSpec((None, chunk, width), lambda b, c: (b, c, 0)),
        out_shape=jax.ShapeDtypeStruct((bsz, seq, width), BF16),
        scratch_shapes=[pltpu.VMEM((n_tiles, S5_LANES), F32), pltpu.VMEM((n_tiles, S5_LANES), F32)],
        compiler_params=_cparams(("parallel", "arbitrary")),
        name="s5_scan",
    )(u3, *consts)


def _compress_kernel(k_ref, v_ref, w1a_k, w1b_k, pa_k, pb_k, w2_k, w1a_v, w1b_v, pa_v, pb_v, w2_v,
                     kc_ref, vc_ref):
    def hidden(t_ref, w1a, w1b, pa, pb):
        r = t_ref[...]
        n = r.shape[0]
        first = _dot((r + pa[...]).astype(BF16), w1a[...])
        second = _dot((r + pb[...]).astype(BF16), w1b[...])
        rowi = lax.broadcasted_iota(jnp.int32, second.shape, 0)
        nxt = jnp.where(rowi < n - 1, pltpu.roll(second, n - 1, 0), 0.0)
        return jax.nn.gelu(first + nxt).astype(BF16)

    kc_ref[...] = _dot(hidden(k_ref, w1a_k, w1b_k, pa_k, pb_k), w2_k[...]).astype(kc_ref.dtype)
    vc_ref[...] = _dot_nt(w2_v[...], hidden(v_ref, w1a_v, w1b_v, pa_v, pb_v)).astype(vc_ref.dtype)


def _compress_weights(w1, w2, pos, feature_major_out=False):
    half = CMP_LEN // 2
    eye = jnp.eye(NSA_KV_HEADS, dtype=F32)

    def first_layer(w):
        m = w[:, None, :, None, :] * eye[None, :, None, :, None]
        return m.reshape(half * NSA_KV_HEADS * HEAD_DIM, NSA_KV_HEADS * CMP_HIDDEN).astype(BF16)

    def pos_row(p):
        return jnp.broadcast_to(p[:, None, :], (half, NSA_KV_HEADS, HEAD_DIM)).reshape(1, -1).astype(F32)

    w2_bd = (eye[:, None, :, None] * w2[None, :, None, :]).reshape(
        NSA_KV_HEADS * CMP_HIDDEN, NSA_KV_HEADS * HEAD_DIM).astype(BF16)
    if feature_major_out:
        w2_bd = w2_bd.T
    return (first_layer(w1[:half]), first_layer(w1[half:]), pos_row(pos[:half]), pos_row(pos[half:]), w2_bd)


def _compress(kc_raw, vc_raw, layer, wk, wv):
    bsz, n_rows, feat = kc_raw.shape
    consts = list(wk) + list(wv)
    blk = pl.BlockSpec((None, n_rows, feat), lambda b: (b, 0, 0))
    return pl.pallas_call(
        _compress_kernel,
        grid=(bsz,),
        in_specs=[blk, blk] + [_layer_spec(c, layer) for c in consts],
        out_specs=[pl.BlockSpec((None, n_rows, LANES), lambda b: (b, 0, 0)),
                   pl.BlockSpec((None, LANES, n_rows), lambda b: (b, 0, 0))],
        out_shape=[jax.ShapeDtypeStruct((bsz, n_rows, LANES), BF16),
                   jax.ShapeDtypeStruct((bsz, LANES, n_rows), BF16)],
        compiler_params=_cparams(("parallel",)),
        name="nsa_compress",
    )(kc_raw, vc_raw, *consts)


def _nsa_kernel(q_ref, kc_ref, vct_ref, ksa_ref, vsa_ref, kw_ref, vwa_ref, gt_ref, ovt_ref, o_ref,
                m_scr, acc_scr, out_scr, q4_scr, q4s_scr, cnt_scr, *, seq):
    tq = q_ref.shape[0]
    n_cmp_rows = kc_ref.shape[0]
    n_sel = seq // SEL_LEN
    i_tile = pl.program_id(1)
    t0 = i_tile * tq
    heads = range(NSA_KV_HEADS)

    q = q_ref[...]
    gt_t = gt_ref[...].T
    lane = lax.broadcasted_iota(jnp.int32, (1, LANES), 1)
    t_row = t0 + lax.broadcasted_iota(jnp.int32, (1, tq), 1)

    def tile_lanes(a):
        return jnp.concatenate([a] * NSA_GROUP, axis=1)

    def init_state():
        m_scr[...] = jnp.full_like(m_scr, NEG)
        acc_scr[...] = jnp.zeros_like(acc_scr)

    def online_tile(kts, q_scr, vts, bias):
        scores = [_dot_nt(kts[h], q_scr[h]) for h in heads]
        for h in heads:
            s = scores[h]
            if bias is not None:
                s = s + tile_lanes(bias)
            m_prev = m_scr[h]
            m_next = jnp.maximum(m_prev, jnp.max(s, axis=0, keepdims=True))
            alpha = jnp.exp(m_prev - m_next)
            p = jnp.exp(s - m_next).astype(BF16)
            acc_scr[h] = alpha * acc_scr[h] + _dot(vts[h], p)
            m_scr[h] = m_next

    def finish(h):
        acc = acc_scr[h]
        return acc[0:HEAD_DIM] * (1.0 / acc[HEAD_DIM:HEAD_DIM + 1])

    def gate_row(h, br):
        return jnp.concatenate([gt_t[(h * NSA_GROUP + g) * 3 + br:(h * NSA_GROUP + g) * 3 + br + 1, :]
                                for g in range(NSA_GROUP)], axis=1)

    for h in heads:
        in_head = (lane >> _LOG_HEAD) == h
        q4 = jnp.concatenate(
            [jnp.where(in_head, q[:, g * LANES:(g + 1) * LANES], jnp.zeros((), BF16)) for g in range(NSA_GROUP)],
            axis=0)
        q4_scr[h] = q4

        s = _dot_nt(kc_ref[...], q4)
        n_col = lax.broadcasted_iota(jnp.int32, (n_cmp_rows, 1), 0)
        vis = (n_col * CMP_STRIDE + (CMP_LEN - 1)) <= t_row
        s = s + tile_lanes(jnp.where(vis, 0.0, NEG))
        m = jnp.max(s, axis=0, keepdims=True)
        e = jnp.exp(s - m)
        any_vis = tile_lanes(jnp.where(t_row >= CMP_LEN - 1, 1.0, 0.0))
        p = e * (any_vis / jnp.maximum(jnp.sum(e, axis=0, keepdims=True), 1e-30))
        o_cmp = _dot(vct_ref[...], p.astype(BF16))
        out_scr[h] = gate_row(h, 0) * o_cmp[h * HEAD_DIM:(h + 1) * HEAD_DIM]

        p_sum = p[:, 0:tq]
        for g in range(1, NSA_GROUP):
            p_sum = p_sum + p[:, g * tq:(g + 1) * tq]
        ov_t = ovt_ref[...]
        p_sel_t = sum(_dot(ov_t, part) for part in _split3(p_sum))
        blk = lax.broadcasted_iota(jnp.int32, (n_sel, 1), 0)
        t_blk = t_row >> _LOG_SEL
        valid = blk <= t_blk
        forced = (blk == 0) | (blk == t_blk) | (blk == t_blk - 1)
        rank_t = jnp.where(valid, p_sel_t + FORCE_BONUS * jnp.where(forced, 1.0, 0.0), -FORCE_BONUS)
        n_tiles = n_sel // SUBLANES
        rank_tiles = [rank_t[r * SUBLANES:(r + 1) * SUBLANES] for r in range(n_tiles)]
        blk8 = lax.broadcasted_iota(jnp.int32, (SUBLANES, 1), 0)
        cnt_scr[...] = jnp.zeros_like(cnt_scr)
        last_blk = (t0 + tq - 1) >> _LOG_SEL

        def count_group(grp):
            part = [jnp.zeros((SUBLANES, tq), F32) for _ in range(n_tiles)]
            for ii in range(grp * SUBLANES, (grp + 1) * SUBLANES):
                ri = rank_t[ii:ii + 1, :]
                for r in range(n_tiles):
                    if ii < r * SUBLANES:
                        ahead = ri >= rank_tiles[r]
                    elif ii >= (r + 1) * SUBLANES:
                        ahead = ri > rank_tiles[r]
                    else:
                        tie_ok = jnp.where(blk8 + r * SUBLANES > ii, 1.0, 0.0)
                        ahead = (ri > rank_tiles[r]) | ((ri == rank_tiles[r]) & (tie_ok > 0.5))
                    part[r] = part[r] + jnp.where(ahead, 1.0, 0.0)
            for r in range(n_tiles):
                rows = slice(r * SUBLANES, (r + 1) * SUBLANES)
                cnt_scr[rows, :] = cnt_scr[rows, :] + part[r]

        for grp in range(n_tiles):
            pl.when(grp * SUBLANES <= last_blk)(functools.partial(count_group, grp))
        top = float(min(SEL_TOPK, n_sel))
        sel_bias = jnp.concatenate(
            [jnp.where(cnt_scr[...] < top, 0.0, NEG), jnp.zeros((LANES - n_sel, tq), F32)], axis=0)
        mask_lanes = sel_bias.T
        if h == 0:
            mask_lanes = pltpu.roll(mask_lanes, HEAD_DIM, 1)
        mask_lanes = mask_lanes.astype(BF16)
        q4s_scr[h] = jnp.concatenate(
            [jnp.where(in_head, q[:, g * LANES:(g + 1) * LANES], mask_lanes) for g in range(NSA_GROUP)], axis=0)

    init_state()
    tk = NSA_TK_SEL
    n_full = t0 // tk

    def sel_tile(k0, size, bias):
        online_tile([ksa_ref[h, pl.ds(k0, size), :] for h in heads], q4s_scr,
                    [vsa_ref[h, :, pl.ds(k0, size)] for h in heads], bias)

    def sel_body(j, _):
        sel_tile(pl.multiple_of(j * 2 * tk, 2 * tk), 2 * tk, None)
        return 0

    lax.fori_loop(0, n_full // 2, sel_body, 0)

    @pl.when(n_full % 2 == 1)
    def _():
        sel_tile(pl.multiple_of((n_full - 1) * tk, tk), tk, None)

    k_diag = pl.multiple_of(n_full * tk, tk)
    kpos_sel = k_diag + lax.broadcasted_iota(jnp.int32, (tk, 1), 0)
    sel_tile(k_diag, tk, jnp.where(kpos_sel <= t_row, 0.0, NEG))
    for h in heads:
        out_scr[h] = out_scr[h] + gate_row(h, 1) * finish(h)

    init_state()
    tkw = WINDOW + tq
    k0 = pl.multiple_of(jnp.maximum(t0 - WINDOW, 0), tq)
    dist = t_row - (k0 + lax.broadcasted_iota(jnp.int32, (tkw, 1), 0))
    kt = kw_ref[pl.ds(k0, tkw), :]
    online_tile([kt, kt], q4_scr, [vwa_ref[h, :, pl.ds(k0, tkw)] for h in heads],
                jnp.where((dist >= 0) & (dist < WINDOW), 0.0, NEG))
    both = jnp.concatenate([out_scr[h] + gate_row(h, 2) * finish(h) for h in heads], axis=0)
    for g in range(NSA_GROUP):
        o_ref[:, g * LANES:(g + 1) * LANES] = both[:, g * tq:(g + 1) * tq].T.astype(o_ref.dtype)


def _nsa(q3, kc, vc_t, ks_aug, vs_aug, kw, vw_aug, gt, ov_t):
    bsz, seq, qw = q3.shape
    tq = NSA_TQ
    n_rows = kc.shape[1]
    nh = NSA_KV_HEADS
    full = pl.BlockSpec((None, seq, LANES), lambda b, i: (b, 0, 0))
    head_tok = pl.BlockSpec((None, nh, seq, LANES), lambda b, i: (b, 0, 0, 0))
    head_feat = pl.BlockSpec((None, nh, LANES, seq), lambda b, i: (b, 0, 0, 0))
    tile = lambda w: pl.BlockSpec((None, tq, w), lambda b, i: (b, i, 0))
    cols = NSA_GROUP * tq
    return pl.pallas_call(
        functools.partial(_nsa_kernel, seq=seq),
        grid=(bsz, seq // tq),
        in_specs=[tile(qw),
                  pl.BlockSpec((None, n_rows, LANES), lambda b, i: (b, 0, 0)),
                  pl.BlockSpec((None, LANES, n_rows), lambda b, i: (b, 0, 0)),
                  head_tok, head_feat, full, head_feat, tile(LANES), _const_spec(ov_t.shape)],
        out_specs=tile(qw),
        out_shape=jax.ShapeDtypeStruct((bsz, seq, qw), BF16),
        scratch_shapes=[pltpu.VMEM((nh, 1, cols), F32),
                        pltpu.VMEM((nh, LANES, cols), F32),
                        pltpu.VMEM((nh, HEAD_DIM, cols), F32),
                        pltpu.VMEM((nh, cols, LANES), BF16),
                        pltpu.VMEM((nh, cols, LANES), BF16),
                        pltpu.VMEM((seq // SEL_LEN, tq), F32)],
        compiler_params=_cparams(("parallel", "arbitrary")),
        name="nsa_attention",
    )(q3, kc, vc_t, ks_aug, vs_aug, kw, vw_aug, gt, ov_t)


def _sb_kernel(q_ref, k_ref, vt_ref, o_ref, r_scr, acc_scr):
    tq = q_ref.shape[0]
    tk = SB_TILES * tq
    i_tile = pl.program_id(1)
    t0 = i_tile * tq
    q = q_ref[...]
    lane = lax.broadcasted_iota(jnp.int32, (1, LANES), 1)
    feat = lax.broadcasted_iota(jnp.int32, (LANES, 1), 0)
    t_row = t0 + lax.broadcasted_iota(jnp.int32, (1, tq), 1)
    kpos_col = lax.broadcasted_iota(jnp.int32, (tk, 1), 0)
    sr = lax.broadcasted_iota(jnp.int32, (tq, tq), 0)
    sc = lax.broadcasted_iota(jnp.int32, (tq, tq), 1)
    later = jnp.where(sc > sr, 1.0, 0.0).astype(BF16)
    qm = []
    for h in range(SB_HEADS):
        in_head = (lane >> _LOG_HEAD) == (h % 2)
        pair = q[:, (h // 2) * LANES:(h // 2 + 1) * LANES]
        qm.append(jnp.where(in_head, pair, jnp.zeros((), BF16)))
    r_scr[...] = jnp.zeros_like(r_scr)
    acc_scr[...] = jnp.zeros_like(acc_scr)

    def cond(c):
        j, r_max = c
        return (j >= 0) & (r_max > SB_EXIT)

    def body(c):
        j, _ = c
        k0 = pl.multiple_of(jnp.maximum(j + 1 - SB_TILES, 0) * tq, tq)
        kt = k_ref[pl.ds(k0, tk), :]
        z = jnp.concatenate([_dot_nt(kt[:, (h // 2) * LANES:(h // 2 + 1) * LANES], qm[h])
                             for h in range(SB_HEADS)], axis=1)
        kpos = k0 + kpos_col
        hidden = jnp.where((kpos < t_row) & (kpos < (j + 1) * tq), 0.0, NEG)
        z = z + jnp.concatenate([hidden] * SB_HEADS, axis=1)
        ls = jnp.minimum(z, 0.0) - jnp.log(1.0 + jnp.exp(-jnp.abs(z)))
        rest = ls - z
        rest_b = rest.astype(BF16)
        r_prev = r_scr[...]
        totals = [jnp.sum(rest[a * tq:(a + 1) * tq], axis=0, keepdims=True) for a in range(SB_TILES)]
        after_tiles = []
        right = r_prev
        for a in reversed(range(SB_TILES)):
            rows = slice(a * tq, (a + 1) * tq)
            after_tiles.append(_dot(later, rest_b[rows]) + right)
            right = right + totals[a]
        after = jnp.concatenate(after_tiles[::-1], axis=0)
        w = jnp.exp(ls + after).astype(BF16)
        for h in range(SB_HEADS):
            cols = slice(h * tq, (h + 1) * tq)
            vt = vt_ref[(h // 2) * LANES:(h // 2 + 1) * LANES, pl.ds(k0, tk)]
            acc_scr[:, cols] = acc_scr[:, cols] + _dot(vt, w[:, cols])
        r_new = right
        r_scr[...] = r_new
        return j - SB_TILES, jnp.max(r_new)

    lax.while_loop(cond, body, (i_tile, jnp.zeros((), F32)))
    for hp in range(SB_HEADS // 2):
        even = acc_scr[:, (2 * hp) * tq:(2 * hp + 1) * tq]
        odd = acc_scr[:, (2 * hp + 1) * tq:(2 * hp + 2) * tq]
        o_ref[:, hp * LANES:(hp + 1) * LANES] = jnp.where(feat < HEAD_DIM, even, odd).T.astype(o_ref.dtype)


def _sb(q3, k3, v_t):
    bsz, seq, w = q3.shape
    tq = SB_TQ
    full = pl.BlockSpec((None, seq, w), lambda b, i: (b, 0, 0))
    full_t = pl.BlockSpec((None, w, seq), lambda b, i: (b, 0, 0))
    tile = pl.BlockSpec((None, tq, w), lambda b, i: (b, i, 0))
    return pl.pallas_call(
        _sb_kernel,
        grid=(bsz, seq // tq),
        in_specs=[tile, full, full_t],
        out_specs=tile,
        out_shape=jax.ShapeDtypeStruct((bsz, seq, w), BF16),
        scratch_shapes=[pltpu.VMEM((1, SB_HEADS * tq), F32), pltpu.VMEM((LANES, SB_HEADS * tq), F32)],
        compiler_params=_cparams(("parallel", "arbitrary")),
        name="sb_attention",
    )(q3, k3, v_t)


def _merge_kernel(x_ref, g_ref, a_ref, b_ref, c_ref, wg_ref, wb_ref, wo_ref, o_ref):
    x = x_ref[...]
    d = x.shape[1]
    h = _rms(x, g_ref[...]).astype(BF16)
    mixed = None
    for n, br in enumerate((a_ref, b_ref, c_ref)):
        gate = jax.nn.sigmoid(_dot(h, wg_ref[:, n * d:(n + 1) * d]))
        y = gate * _dot(br[...], wb_ref[n])
        mixed = y if mixed is None else mixed + y
    o_ref[...] = x + _dot(mixed.astype(BF16), wo_ref[...])


def _merge(x2, layer, g, o_a, o_b, o_c, w_gate, w_branch, w_out):
    t, d = x2.shape
    tm = min(TM_PROJ, t)
    row = lambda i: (i, 0)
    bw = o_a.shape[1]
    return pl.pallas_call(
        _merge_kernel,
        grid=(t // tm,),
        in_specs=[pl.BlockSpec((tm, d), row), _layer_spec(g, layer),
                  pl.BlockSpec((tm, bw), row), pl.BlockSpec((tm, bw), row), pl.BlockSpec((tm, bw), row),
                  _layer_spec(w_gate, layer), _layer_spec(w_branch, layer), _layer_spec(w_out, layer)],
        out_specs=pl.BlockSpec((tm, d), row),
        out_shape=jax.ShapeDtypeStruct((t, d), F32),
        compiler_params=_cparams(("parallel",)),
        name="gated_merge",
    )(x2, g, o_a, o_b, o_c, w_gate, w_branch, w_out)


def _ffn_kernel(x_ref, halo_ref, g_ref, wup_ref, cw_ref, cb_ref, wdn_ref, gout_ref, o_ref, act_scr, *,
                tiles_per_seq, norm_out):
    x = x_ref[...]
    tm = x.shape[0]
    d_ff = wdn_ref.shape[0]
    at_start = (pl.program_id(0) % tiles_per_seq) == 0
    g = g_ref[...]
    h_halo = jnp.where(at_start, 0.0, _rms(halo_ref[...], g))
    h = jnp.concatenate([h_halo, _rms(x, g)], axis=0).astype(BF16)
    for f in range(d_ff // FF_TILE):
        halves = []
        for base in (0, d_ff):
            cols = slice(base + f * FF_TILE, base + (f + 1) * FF_TILE)
            u = _dot(h, wup_ref[:, cols])
            w = cw_ref[:, cols]
            conv = (w[2:3] * u[SUBLANES:]
                    + w[1:2] * pltpu.roll(u, 1, 0)[SUBLANES:]
                    + w[0:1] * pltpu.roll(u, 2, 0)[SUBLANES:]
                    + cb_ref[:, cols])
            halves.append(conv)
        act_scr[:, f * FF_TILE:(f + 1) * FF_TILE] = (jax.nn.silu(halves[0]) * halves[1]).astype(BF16)
    y = x + _dot(act_scr[...], wdn_ref[...])
    o_ref[...] = _rms(y, gout_ref[...]) if norm_out else y


def _ffn(x2, layer, g, w_up, conv_w, conv_b, w_down, g_out, seq, norm_out):
    t, d = x2.shape
    tm = min(TM_FFN, seq)
    tiles_per_seq = seq // tm
    row = lambda i: (i, 0)
    halo = lambda i: (jnp.maximum(i * (tm // SUBLANES) - 1, 0), 0)
    return pl.pallas_call(
        functools.partial(_ffn_kernel, tiles_per_seq=tiles_per_seq, norm_out=norm_out),
        grid=(t // tm,),
        in_specs=[pl.BlockSpec((tm, d), row), pl.BlockSpec((SUBLANES, d), halo), _layer_spec(g, layer),
                  _layer_spec(w_up, layer), _layer_spec(conv_w, layer), _layer_spec(conv_b, layer),
                  _layer_spec(w_down, layer), _const_spec(g_out.shape)],
        out_specs=pl.BlockSpec((tm, d), row),
        out_shape=jax.ShapeDtypeStruct((t, d), F32),
        scratch_shapes=[pltpu.VMEM((tm, w_down.shape[1]), BF16)],
        compiler_params=_cparams(("parallel",)),
        name="conv_ffn",
    )(x2, x2, g, w_up, conv_w, conv_b, w_down, g_out)


def _norm_kernel(x_ref, g_ref, o_ref):
    o_ref[...] = _rms(x_ref[...], g_ref[...])


def _final_norm(x2, g):
    t, d = x2.shape
    tm = min(TM_PROJ, t)
    return pl.pallas_call(
        _norm_kernel,
        grid=(t // tm,),
        in_specs=[pl.BlockSpec((tm, d), lambda i: (i, 0)), _const_spec((1, d))],
        out_specs=pl.BlockSpec((tm, d), lambda i: (i, 0)),
        out_shape=jax.ShapeDtypeStruct((t, d), F32),
        compiler_params=_cparams(("parallel",)),
        name="final_norm",
    )(x2, g)


def _nsa_slot_order():
    return [NSA_GROUP * (s % 2) + s // 2 for s in range(NSA_HEADS)]


def _pack_w_in(w):
    d = w.shape[0]
    o_q, o_kv, o_g, o_sb, o_gm = 512, 1024, 1792, 1816, 3352
    kv_w = NSA_KV_HEADS * HEAD_DIM
    sb_w = SB_HEADS * HEAD_DIM
    u = w[:, 0:o_q]
    q = [w[:, o_q + hq * HEAD_DIM:o_q + (hq + 1) * HEAD_DIM] for hq in _nsa_slot_order()]
    k = [w[:, o_kv + br * 2 * kv_w:o_kv + br * 2 * kv_w + kv_w] for br in range(3)]
    v = [w[:, o_kv + br * 2 * kv_w + kv_w:o_kv + (br + 1) * 2 * kv_w] for br in range(3)]
    gates = [w[:, o_g:o_sb], jnp.zeros((d, LANES - (o_sb - o_g)), w.dtype)]
    packed = jnp.concatenate([u] + q + k + [v[0]] + gates + [w[:, o_sb:o_sb + 2 * sb_w]], axis=1).astype(BF16)
    w_vt = jnp.concatenate([v[1], v[2], w[:, o_sb + 2 * sb_w:o_gm]], axis=1).astype(BF16).T
    return packed, w_vt, w[:, o_gm:].astype(BF16)


def _rope_tables(seq):
    inv_freq = 1.0 / (ROPE_THETA ** (jnp.arange(0, HEAD_DIM, 2, dtype=F32) / HEAD_DIM))
    ang = jnp.arange(seq, dtype=F32)[:, None] * inv_freq[None, :]
    cos, sin = jnp.cos(ang), jnp.sin(ang)
    cos_t = jnp.tile(cos, (1, LANES // (HEAD_DIM // 2)))
    sin_t = jnp.tile(jnp.concatenate([-sin, sin], axis=1), (1, LANES // HEAD_DIM))
    return cos_t, sin_t


def _overlap_t(seq):
    n_rows = seq // CMP_STRIDE
    n_sel = seq // SEL_LEN
    cmp_start = jnp.arange(n_rows) * CMP_STRIDE
    sel_start = jnp.arange(n_sel) * SEL_LEN
    ov = ((cmp_start[None, :] < sel_start[:, None] + SEL_LEN)
          & (cmp_start[None, :] + CMP_LEN > sel_start[:, None])
          & (jnp.arange(n_rows)[None, :] < (seq - CMP_LEN) // CMP_STRIDE + 1))
    return ov.astype(BF16)


def kernel(x, norm_mix, w_in, ssm_lam_re, ssm_lam_im, ssm_log_step, ssm_b_re, ssm_b_im, ssm_c_re, ssm_c_im, ssm_d, ssm_w_glu, cmp_w1_k, cmp_w2_k, cmp_pos_k, cmp_w1_v, cmp_w2_v, cmp_pos_v, w_branch, w_out, norm_ffn, ffn_w_up, ffn_conv_w, ffn_conv_b, ffn_w_down, norm_final):
    bsz, seq, d = x.shape
    depth = w_in.shape[0]
    t = bsz * seq
    assert seq % 512 == 0 and seq // SEL_LEN <= HEAD_DIM
    assert seq >= WINDOW + NSA_TQ
    if depth == 0:
        return _final_norm(x.reshape(t, d).astype(F32), norm_final.reshape(1, d)).reshape(x.shape).astype(x.dtype)
    cos_t, sin_t = _rope_tables(seq)
    ov_t = _overlap_t(seq)
    w_packed, w_vt, w_gate = jax.vmap(_pack_w_in)(w_in)
    s5_tabs = jax.vmap(_s5_tables)(ssm_lam_re, ssm_lam_im, ssm_log_step, ssm_b_re, ssm_b_im, ssm_c_re, ssm_c_im)
    s5_d = ssm_d.reshape(depth, 1, -1).astype(F32)
    s5_glu = ssm_w_glu.astype(BF16)
    cmp_k = jax.vmap(_compress_weights)(cmp_w1_k, cmp_w2_k, cmp_pos_k)
    cmp_v = jax.vmap(functools.partial(_compress_weights, feature_major_out=True))(cmp_w1_v, cmp_w2_v, cmp_pos_v)
    wb_nsa = jnp.concatenate([w_branch[:, 1, hq * HEAD_DIM:(hq + 1) * HEAD_DIM] for hq in _nsa_slot_order()], axis=1)
    wb_all = jnp.stack([w_branch[:, 0], wb_nsa, w_branch[:, 2]], axis=1).astype(BF16)
    wo_all = w_out.astype(BF16)
    g_mix = norm_mix.reshape(depth, 1, d).astype(F32)
    g_ffn = norm_ffn.reshape(depth, 1, d).astype(F32)
    up_all = ffn_w_up.astype(BF16)
    down_all = ffn_w_down.astype(BF16)
    conv_w = ffn_conv_w.astype(F32)
    conv_b = ffn_conv_b.reshape(depth, 1, -1).astype(F32)
    g_final = norm_final.reshape(1, d).astype(F32)

    x2 = x.reshape(t, d).astype(F32)
    b3 = lambda a: a.reshape(bsz, seq, a.shape[-1])
    rows16 = lambda a: a.reshape(bsz, seq // CMP_STRIDE, a.shape[-1])
    for l in range(depth):
        (u, q, kc_raw, ks_aug, kw, vc_raw, vs_aug, vw_aug, gt, sq, sk, sv_t) = _proj_in(
            x2, l, g_mix, w_packed, w_vt, cos_t, sin_t, bsz, seq)
        o_ssm = _s5(b3(u), l, s5_tabs, s5_d, s5_glu)
        kc, vc_t = _compress(rows16(kc_raw), rows16(vc_raw), l, cmp_k, cmp_v)
        o_nsa = _nsa(b3(q), kc, vc_t, ks_aug, vs_aug, b3(kw), vw_aug, b3(gt), ov_t)
        o_sb = _sb(b3(sq), b3(sk), sv_t)
        x2 = _merge(x2, l, g_mix, o_ssm.reshape(t, -1), o_nsa.reshape(t, -1), o_sb.reshape(t, -1),
                    w_gate, wb_all, wo_all)
        x2 = _ffn(x2, l, g_ffn, up_all, conv_w, conv_b, down_all, g_final, seq, norm_out=(l == depth - 1))
    return x2.reshape(bsz, seq, d).astype(x.dtype)
```

```python
import functools

import jax
import jax.numpy as jnp
from jax import lax
from jax.experimental import pallas as pl
from jax.experimental.pallas import tpu as pltpu

F32 = jnp.float32
BF16 = jnp.bfloat16

HEAD_DIM = 64
RMS_EPS = 1e-6
ROPE_THETA = 10000.0
SSM_GROUP = 16
SSM_STATE = 64
NSA_HEADS = 8
NSA_KV_HEADS = 2
NSA_GROUP = NSA_HEADS // NSA_KV_HEADS
CMP_LEN = 32
CMP_STRIDE = 16
CMP_HIDDEN = 128
SEL_LEN = 64
SEL_TOPK = 16
WINDOW = 512
FORCE_BONUS = 1e6
SB_HEADS = 8
CONV_WIDTH = 3
BRANCH_WIDTH = 512

LANES = 128
SUBLANES = 8
VMEM_LIMIT = 56 * 1024 * 1024

TM_PROJ = 1024
TM_FFN = 512
FF_TILE = 256
S5_CHUNK = 1024
S5_LANES = 256
S5_MXU_SHIFTS = 4
NSA_TQ = 512
NSA_TK_SEL = 512
SB_TQ = 256
SB_TILES = 2

_LOG_HEAD = HEAD_DIM.bit_length() - 1
_LOG_SEL = SEL_LEN.bit_length() - 1
NEG = -1e30
SB_EXIT = -104.0


def _cparams(sem):
    return pltpu.CompilerParams(dimension_semantics=sem, vmem_limit_bytes=VMEM_LIMIT)


def _const_spec(shape):
    n = len(shape)
    return pl.BlockSpec(shape, lambda *_: (0,) * n, pipeline_mode=pl.Buffered(1))


def _layer_spec(stacked, layer):
    n = stacked.ndim - 1
    return pl.BlockSpec((None,) + stacked.shape[1:], lambda *_: (layer,) + (0,) * n,
                        pipeline_mode=pl.Buffered(1))


def _rms(x, g):
    return x * lax.rsqrt(jnp.mean(x * x, axis=-1, keepdims=True) + RMS_EPS) * g


def _dot(a, b):
    return jnp.dot(a, b, preferred_element_type=F32)


def _dot_nt(a, b):
    return lax.dot_general(a, b, (((1,), (1,)), ((), ())), preferred_element_type=F32)


def _split3(x):
    h = x.astype(BF16)
    r = x - h.astype(F32)
    m = r.astype(BF16)
    lo = (r - m.astype(F32)).astype(BF16)
    return h, m, lo


_C_U = 0
_C_ROPE = 512
_C_V = 1408
_C_G = 1536
_C_SB = 1664
_C_END = 2688
_R_VS, _R_VW, _R_SV, _R_END = 0, 128, 256, 768


def _proj_kernel(x_ref, g_ref, w_ref, wvt_ref, cos_ref, sin_ref,
                 u_ref, q_ref, kc_ref, ksa_ref, kw_ref, vc_ref, vsa_ref, vwa_ref, gt_ref,
                 sq_ref, sk_ref, svt_ref, rows_scr, *, n_pos):
    tm = x_ref.shape[0]
    h = _rms(x_ref[...], g_ref[...]).astype(BF16)

    def store_rows16(val, out_ref):
        rows_scr[...] = val
        for b in range(CMP_STRIDE):
            out_ref[:, b * LANES:(b + 1) * LANES] = rows_scr[pl.ds(b, tm // CMP_STRIDE, stride=CMP_STRIDE), :]

    def mm(a, b):
        return _dot(h, w_ref[:, a:b])

    vt = _dot_nt(wvt_ref[...], h)
    ones_rows = jnp.where(lax.broadcasted_iota(jnp.int32, (HEAD_DIM, tm), 0) == 0, 1.0, 0.0)
    for hh in range(NSA_KV_HEADS):
        vsa_ref[hh] = jnp.concatenate(
            [vt[_R_VS + hh * HEAD_DIM:_R_VS + (hh + 1) * HEAD_DIM], ones_rows], axis=0).astype(BF16)
        vwa_ref[hh] = jnp.concatenate(
            [vt[_R_VW + hh * HEAD_DIM:_R_VW + (hh + 1) * HEAD_DIM], ones_rows], axis=0).astype(BF16)
    svt_ref[...] = vt[_R_SV:_R_END].astype(BF16)

    u_ref[...] = mm(_C_U, _C_ROPE)

    r = mm(_C_ROPE, _C_V)
    cos = cos_ref[...]
    sin = sin_ref[...]
    lane = lax.broadcasted_iota(jnp.int32, cos.shape, 1)
    first = (lane & (HEAD_DIM - 1)) < (HEAD_DIM // 2)
    roped = []
    for c in range((_C_V - _C_ROPE) // LANES):
        rc = r[:, c * LANES:(c + 1) * LANES]
        partner = jnp.where(first, pltpu.roll(rc, LANES - HEAD_DIM // 2, 1),
                            pltpu.roll(rc, HEAD_DIM // 2, 1))
        roped.append(rc * cos + partner * sin)
    scale = HEAD_DIM ** -0.5
    for c in range(4):
        q_ref[:, c * LANES:(c + 1) * LANES] = (roped[c] * scale).astype(BF16)
    store_rows16(roped[4], kc_ref)
    kw_ref[...] = roped[6].astype(BF16)
    pos = (pl.program_id(0) % n_pos) * tm + lax.broadcasted_iota(jnp.int32, (tm, 1), 0)
    blk = pos >> _LOG_SEL
    lane_k = lax.broadcasted_iota(jnp.int32, (1, LANES), 1)
    for hh in range(NSA_KV_HEADS):
        own = (lane_k >> _LOG_HEAD) == hh
        one_hot = jnp.where(blk == (lane_k & (HEAD_DIM - 1)), 1.0, 0.0)
        ksa_ref[hh] = jnp.where(own, roped[5], one_hot).astype(BF16)

    store_rows16(mm(_C_V, _C_G), vc_ref)
    gt_ref[...] = jax.nn.sigmoid(mm(_C_G, _C_SB))

    sb = mm(_C_SB, _C_END)
    sq_ref[...] = (sb[:, 0:512] * scale).astype(BF16)
    sk_ref[...] = sb[:, 512:1024].astype(BF16)


def _proj_in(x2, layer, g, w_packed, w_vt, cos_t, sin_t, bsz, seq):
    t, d = x2.shape
    tm = min(TM_PROJ, seq)
    n_pos = seq // tm
    row = lambda i: (i, 0)
    pos = lambda i: (i % n_pos, 0)
    nh = NSA_KV_HEADS

    def tok(w, dt):
        return pl.BlockSpec((tm, w), row), jax.ShapeDtypeStruct((t, w), dt)

    def feat_major(w, dt):
        return (pl.BlockSpec((None, w, tm), lambda i: (i // n_pos, 0, i % n_pos)),
                jax.ShapeDtypeStruct((bsz, w, seq), dt))

    def head_tok(dt):
        return (pl.BlockSpec((None, nh, tm, LANES), lambda i: (i // n_pos, 0, i % n_pos, 0)),
                jax.ShapeDtypeStruct((bsz, nh, seq, LANES), dt))

    def head_feat(dt):
        return (pl.BlockSpec((None, nh, LANES, tm), lambda i: (i // n_pos, 0, 0, i % n_pos)),
                jax.ShapeDtypeStruct((bsz, nh, LANES, seq), dt))

    def rows16():
        return (pl.BlockSpec((tm // CMP_STRIDE, CMP_STRIDE * LANES), row),
                jax.ShapeDtypeStruct((t // CMP_STRIDE, CMP_STRIDE * LANES), F32))

    outs = [tok(512, F32), tok(512, BF16), rows16(), head_tok(BF16), tok(128, BF16),
            rows16(), head_feat(BF16), head_feat(BF16), tok(128, F32),
            tok(512, BF16), tok(512, BF16), feat_major(512, BF16)]
    return pl.pallas_call(
        functools.partial(_proj_kernel, n_pos=n_pos),
        grid=(t // tm,),
        in_specs=[pl.BlockSpec((tm, d), row),
                  _layer_spec(g, layer),
                  _layer_spec(w_packed, layer),
                  _layer_spec(w_vt, layer),
                  pl.BlockSpec((tm, LANES), pos),
                  pl.BlockSpec((tm, LANES), pos)],
        out_specs=[spec for spec, _ in outs],
        out_shape=[shape for _, shape in outs],
        scratch_shapes=[pltpu.VMEM((tm, LANES), F32)],
        compiler_params=_cparams(("parallel",)),
        name="proj_in",
    )(x2, g, w_packed, w_vt, cos_t, sin_t)


def _s5_kernel(u_ref, b8_ref, ccat_ref, pw_re_ref, pw_im_ref, cy_re_ref, cy_im_ref, d_ref, wglu_ref, o_ref,
               car_re, car_im):
    @pl.when(pl.program_id(1) == 0)
    def _():
        car_re[...] = jnp.zeros_like(car_re)
        car_im[...] = jnp.zeros_like(car_im)

    u = u_ref[...]
    chunk, width = u.shape
    n_tiles = b8_ref.shape[0]
    n_sub = chunk // SUBLANES
    u3 = u.reshape(n_sub, SUBLANES, width)
    row8 = lax.broadcasted_iota(jnp.int32, (1, SUBLANES, 1), 1)
    shifted = [u.astype(BF16)]
    for s in range(1, S5_MXU_SHIFTS):
        shifted.append(jnp.where(row8 >= s, pltpu.roll(u3, s, 1), 0.0).reshape(chunk, width).astype(BF16))
    ucat = [jnp.concatenate([sh[:, b * LANES:(b + 1) * LANES] for sh in shifted], axis=1)
            for b in range(width // LANES)]
    y_blocks = []
    for blk in range(n_tiles // 2):
        pair = (2 * blk, 2 * blk + 1)
        hr, hi = {}, {}
        for j in pair:
            x = _dot(ucat[blk], b8_ref[j])
            xr = x[:, :S5_LANES].reshape(n_sub, SUBLANES, S5_LANES)
            xi = x[:, S5_LANES:].reshape(n_sub, SUBLANES, S5_LANES)
            sr = pltpu.roll(xr, S5_MXU_SHIFTS, 1)
            si = pltpu.roll(xi, S5_MXU_SHIFTS, 1)
            cr = pw_re_ref[j]
            ci = pw_im_ref[j]
            hr[j], hi[j] = xr + cr * sr - ci * si, xi + cr * si + ci * sr
        c_re = {j: car_re[j:j + 1, :] for j in pair}
        c_im = {j: car_im[j:j + 1, :] for j in pair}
        out_re = {j: [] for j in pair}
        out_im = {j: [] for j in pair}
        for v in range(n_sub):
            for j in pair:
                pr = cy_re_ref[j]
                pi = cy_im_ref[j]
                t_re = hr[j][v] + pr * c_re[j] - pi * c_im[j]
                t_im = hi[j][v] + pr * c_im[j] + pi * c_re[j]
                c_re[j] = t_re[SUBLANES - 1:SUBLANES, :]
                c_im[j] = t_im[SUBLANES - 1:SUBLANES, :]
                out_re[j].append(t_re)
                out_im[j].append(t_im)
        y_blk = None
        for j in pair:
            car_re[j:j + 1, :] = c_re[j]
            car_im[j:j + 1, :] = c_im[j]
            hcat = jnp.concatenate([jnp.concatenate(out_re[j], axis=0), jnp.concatenate(out_im[j], axis=0)],
                                   axis=1).astype(BF16)
            yj = _dot(hcat, ccat_ref[j])
            y_blk = yj if y_blk is None else y_blk + yj
        y_blocks.append(y_blk)
    y = jnp.concatenate(y_blocks, axis=1) + d_ref[...] * u
    z = jax.nn.gelu(y)
    o_ref[...] = (z * jax.nn.sigmoid(_dot(z.astype(BF16), wglu_ref[...]))).astype(o_ref.dtype)


def _s5_tables(lam_re, lam_im, log_step, b_re, b_im, c_re, c_im):
    g_n, p_n = lam_re.shape
    i_n = b_re.shape[-1]
    width = g_n * i_n
    lanes_total = g_n * p_n
    n_tiles = lanes_total // S5_LANES
    step = jnp.exp(log_step.astype(F32))[:, None]
    lr = lam_re.astype(F32)
    li = lam_im.astype(F32)
    mag = jnp.exp(lr * step)
    a_re = mag * jnp.cos(li * step)
    a_im = mag * jnp.sin(li * step)
    den = lr * lr + li * li
    nr = a_re - 1.0
    f_re = (nr * lr + a_im * li) / den
    f_im = (a_im * lr - nr * li) / den
    br = b_re.astype(F32)
    bi = b_im.astype(F32)
    bb_re = f_re[..., None] * br - f_im[..., None] * bi
    bb_im = f_re[..., None] * bi + f_im[..., None] * br
    log_mag = (lr * step).reshape(1, lanes_total)
    ang = (li * step).reshape(1, lanes_total)

    def powers(n):
        m = jnp.exp(n * log_mag)
        return m * jnp.cos(n * ang), m * jnp.sin(n * ang)

    gpt = S5_LANES // p_n
    gpb = LANES // i_n
    tile_id = jnp.arange(n_tiles)
    in_tile = (jnp.arange(gpb)[None, :, None]
               == (tile_id[:, None, None] % (gpb // gpt)) * gpt + jnp.arange(gpt)[None, None, :]).astype(F32)

    n_sh = S5_MXU_SHIFTS
    as_re, as_im = powers(jnp.arange(n_sh, dtype=F32)[:, None])
    as_re = as_re.reshape(n_sh, g_n, p_n, 1)
    as_im = as_im.reshape(n_sh, g_n, p_n, 1)
    w_re = as_re * bb_re[None] - as_im * bb_im[None]
    w_im = as_re * bb_im[None] + as_im * bb_re[None]

    def in_tiles(w):
        w = w.reshape(n_sh, n_tiles, gpt, p_n, i_n).transpose(1, 0, 4, 2, 3)
        out = in_tile[:, None, :, None, :, None] * w[:, :, None, :, :, :]
        return out.reshape(n_tiles, n_sh * LANES, S5_LANES)

    b8 = jnp.concatenate([in_tiles(w_re), in_tiles(w_im)], axis=2).astype(BF16)

    def out_tiles(c):
        c = c.astype(F32).reshape(n_tiles, gpt, i_n, p_n).transpose(0, 1, 3, 2)
        out = in_tile.transpose(0, 2, 1)[:, :, None, :, None] * c[:, :, :, None, :]
        return out.reshape(n_tiles, S5_LANES, LANES)

    ccat = jnp.concatenate([out_tiles(c_re), -out_tiles(c_im)], axis=1).astype(BF16)

    cy_re, cy_im = powers(jnp.arange(1, SUBLANES + 1, dtype=F32)[:, None])

    def tiles(a):
        return a.reshape(a.shape[0], n_tiles, S5_LANES).transpose(1, 0, 2)

    live = (jnp.arange(SUBLANES) >= n_sh).astype(F32)[None, :, None]
    pw_re, pw_im = powers(jnp.full((1, 1), float(n_sh), F32))
    return b8, ccat, tiles(pw_re) * live, tiles(pw_im) * live, tiles(cy_re), tiles(cy_im)


def _s5(u3, layer, tables, d_row, wglu):
    bsz, seq, width = u3.shape
    consts = tuple(tables) + (d_row, wglu)
    chunk = min(S5_CHUNK, seq)
    n_tiles = tables[0].shape[1]
    return pl.pallas_call(
        _s5_kernel,
        grid=(bsz, seq // chunk),
        in_specs=[pl.BlockSpec((None, chunk, width), lambda b, c: (b, c, 0))]
                 + [_layer_spec(a, layer) for a in consts],
        out_specs=pl.BlockSpec((None, chunk, width), lambda b, c: (b, c, 0)),
        out_shape=jax.ShapeDtypeStruct((bsz, seq, width), BF16),
        scratch_shapes=[pltpu.VMEM((n_tiles, S5_LANES), F32), pltpu.VMEM((n_tiles, S5_LANES), F32)],
        compiler_params=_cparams(("parallel", "arbitrary")),
        name="s5_scan",
    )(u3, *consts)


def _compress_kernel(k_ref, v_ref, w1a_k, w1b_k, pa_k, pb_k, w2_k, w1a_v, w1b_v, pa_v, pb_v, w2_v,
                     kc_ref, vc_ref):
    def hidden(t_ref, w1a, w1b, pa, pb):
        r = t_ref[...]
        n = r.shape[0]
        first = _dot((r + pa[...]).astype(BF16), w1a[...])
        second = _dot((r + pb[...]).astype(BF16), w1b[...])
        rowi = lax.broadcasted_iota(jnp.int32, second.shape, 0)
        nxt = jnp.where(rowi < n - 1, pltpu.roll(second, n - 1, 0), 0.0)
        return jax.nn.gelu(first + nxt).astype(BF16)

    kc_ref[...] = _dot(hidden(k_ref, w1a_k, w1b_k, pa_k, pb_k), w2_k[...]).astype(kc_ref.dtype)
    vc_ref[...] = _dot_nt(w2_v[...], hidden(v_ref, w1a_v, w1b_v, pa_v, pb_v)).astype(vc_ref.dtype)


def _compress_weights(w1, w2, pos, feature_major_out=False):
    half = CMP_LEN // 2
    eye = jnp.eye(NSA_KV_HEADS, dtype=F32)

    def first_layer(w):
        m = w[:, None, :, None, :] * eye[None, :, None, :, None]
        return m.reshape(half * NSA_KV_HEADS * HEAD_DIM, NSA_KV_HEADS * CMP_HIDDEN).astype(BF16)

    def pos_row(p):
        return jnp.broadcast_to(p[:, None, :], (half, NSA_KV_HEADS, HEAD_DIM)).reshape(1, -1).astype(F32)

    w2_bd = (eye[:, None, :, None] * w2[None, :, None, :]).reshape(
        NSA_KV_HEADS * CMP_HIDDEN, NSA_KV_HEADS * HEAD_DIM).astype(BF16)
    if feature_major_out:
        w2_bd = w2_bd.T
    return (first_layer(w1[:half]), first_layer(w1[half:]), pos_row(pos[:half]), pos_row(pos[half:]), w2_bd)


def _compress(kc_raw, vc_raw, layer, wk, wv):
    bsz, n_rows, feat = kc_raw.shape
    consts = list(wk) + list(wv)
    blk = pl.BlockSpec((None, n_rows, feat), lambda b: (b, 0, 0))
    return pl.pallas_call(
        _compress_kernel,
        grid=(bsz,),
        in_specs=[blk, blk] + [_layer_spec(c, layer) for c in consts],
        out_specs=[pl.BlockSpec((None, n_rows, LANES), lambda b: (b, 0, 0)),
                   pl.BlockSpec((None, LANES, n_rows), lambda b: (b, 0, 0))],
        out_shape=[jax.ShapeDtypeStruct((bsz, n_rows, LANES), BF16),
                   jax.ShapeDtypeStruct((bsz, LANES, n_rows), BF16)],
        compiler_params=_cparams(("parallel",)),
        name="nsa_compress",
    )(kc_raw, vc_raw, *consts)


def _nsa_kernel(q_ref, kc_ref, vct_ref, ksa_ref, vsa_ref, kw_ref, vwa_ref, gt_ref, ovt_ref, o_ref,
                m_scr, acc_scr, out_scr, q4_scr, q4s_scr, cnt_scr, *, seq):
    tq = q_ref.shape[0]
    n_cmp_rows = kc_ref.shape[0]
    n_sel = seq // SEL_LEN
    i_tile = pl.program_id(1)
    t0 = i_tile * tq
    heads = range(NSA_KV_HEADS)

    q = q_ref[...]
    gt_t = gt_ref[...].T
    lane = lax.broadcasted_iota(jnp.int32, (1, LANES), 1)
    t_row = t0 + lax.broadcasted_iota(jnp.int32, (1, tq), 1)

    def tile_lanes(a):
        return jnp.concatenate([a] * NSA_GROUP, axis=1)

    def init_state():
        m_scr[...] = jnp.full_like(m_scr, NEG)
        acc_scr[...] = jnp.zeros_like(acc_scr)

    def online_tile(kts, q_scr, vts, bias):
        scores = [_dot_nt(kts[h], q_scr[h]) for h in heads]
        for h in heads:
            s = scores[h]
            if bias is not None:
                s = s + tile_lanes(bias)
            m_prev = m_scr[h]
            m_next = jnp.maximum(m_prev, jnp.max(s, axis=0, keepdims=True))
            alpha = jnp.exp(m_prev - m_next)
            p = jnp.exp(s - m_next).astype(BF16)
            acc_scr[h] = alpha * acc_scr[h] + _dot(vts[h], p)
            m_scr[h] = m_next

    def finish(h):
        acc = acc_scr[h]
        return acc[0:HEAD_DIM] * (1.0 / acc[HEAD_DIM:HEAD_DIM + 1])

    def gate_row(h, br):
        return jnp.concatenate([gt_t[(h * NSA_GROUP + g) * 3 + br:(h * NSA_GROUP + g) * 3 + br + 1, :]
                                for g in range(NSA_GROUP)], axis=1)

    for h in heads:
        in_head = (lane >> _LOG_HEAD) == h
        q4 = jnp.concatenate(
            [jnp.where(in_head, q[:, g * LANES:(g + 1) * LANES], jnp.zeros((), BF16)) for g in range(NSA_GROUP)],
            axis=0)
        q4_scr[h] = q4

        s = _dot_nt(kc_ref[...], q4)
        n_col = lax.broadcasted_iota(jnp.int32, (n_cmp_rows, 1), 0)
        vis = (n_col * CMP_STRIDE + (CMP_LEN - 1)) <= t_row
        s = s + tile_lanes(jnp.where(vis, 0.0, NEG))
        m = jnp.max(s, axis=0, keepdims=True)
        e = jnp.exp(s - m)
        any_vis = tile_lanes(jnp.where(t_row >= CMP_LEN - 1, 1.0, 0.0))
        p = e * (any_vis / jnp.maximum(jnp.sum(e, axis=0, keepdims=True), 1e-30))
        o_cmp = _dot(vct_ref[...], p.astype(BF16))
        out_scr[h] = gate_row(h, 0) * o_cmp[h * HEAD_DIM:(h + 1) * HEAD_DIM]

        p_sum = p[:, 0:tq]
        for g in range(1, NSA_GROUP):
            p_sum = p_sum + p[:, g * tq:(g + 1) * tq]
        ov_t = ovt_ref[...]
        p_sel_t = sum(_dot(ov_t, part) for part in _split3(p_sum))
        blk = lax.broadcasted_iota(jnp.int32, (n_sel, 1), 0)
        t_blk = t_row >> _LOG_SEL
        valid = blk <= t_blk
        forced = (blk == 0) | (blk == t_blk) | (blk == t_blk - 1)
        rank_t = jnp.where(valid, p_sel_t + FORCE_BONUS * jnp.where(forced, 1.0, 0.0), -FORCE_BONUS)
        n_tiles = n_sel // SUBLANES
        rank_tiles = [rank_t[r * SUBLANES:(r + 1) * SUBLANES] for r in range(n_tiles)]
        blk8 = lax.broadcasted_iota(jnp.int32, (SUBLANES, 1), 0)
        cnt_scr[...] = jnp.zeros_like(cnt_scr)
        last_blk = (t0 + tq - 1) >> _LOG_SEL

        def count_group(grp):
            part = [jnp.zeros((SUBLANES, tq), F32) for _ in range(n_tiles)]
            for ii in range(grp * SUBLANES, (grp + 1) * SUBLANES):
                ri = rank_t[ii:ii + 1, :]
                for r in range(n_tiles):
                    if ii < r * SUBLANES:
                        ahead = ri >= rank_tiles[r]
                    elif ii >= (r + 1) * SUBLANES:
                        ahead = ri > rank_tiles[r]
                    else:
                        tie_ok = jnp.where(blk8 + r * SUBLANES > ii, 1.0, 0.0)
                        ahead = (ri > rank_tiles[r]) | ((ri == rank_tiles[r]) & (tie_ok > 0.5))
                    part[r] = part[r] + jnp.where(ahead, 1.0, 0.0)
            for r in range(n_tiles):
                rows = slice(r * SUBLANES, (r + 1) * SUBLANES)
                cnt_scr[rows, :] = cnt_scr[rows, :] + part[r]

        for grp in range(n_tiles):
            pl.when(grp * SUBLANES <= last_blk)(functools.partial(count_group, grp))
        top = float(min(SEL_TOPK, n_sel))
        sel_bias = jnp.concatenate(
            [jnp.where(cnt_scr[...] < top, 0.0, NEG), jnp.zeros((LANES - n_sel, tq), F32)], axis=0)
        mask_lanes = sel_bias.T
        if h == 0:
            mask_lanes = pltpu.roll(mask_lanes, HEAD_DIM, 1)
        mask_lanes = mask_lanes.astype(BF16)
        q4s_scr[h] = jnp.concatenate(
            [jnp.where(in_head, q[:, g * LANES:(g + 1) * LANES], mask_lanes) for g in range(NSA_GROUP)], axis=0)

    init_state()
    tk = NSA_TK_SEL
    n_full = t0 // tk

    def sel_tile(k0, size, bias):
        online_tile([ksa_ref[h, pl.ds(k0, size), :] for h in heads], q4s_scr,
                    [vsa_ref[h, :, pl.ds(k0, size)] for h in heads], bias)

    def sel_body(j, _):
        sel_tile(pl.multiple_of(j * 2 * tk, 2 * tk), 2 * tk, None)
        return 0

    lax.fori_loop(0, n_full // 2, sel_body, 0)

    @pl.when(n_full % 2 == 1)
    def _():
        sel_tile(pl.multiple_of((n_full - 1) * tk, tk), tk, None)

    k_diag = pl.multiple_of(n_full * tk, tk)
    kpos_sel = k_diag + lax.broadcasted_iota(jnp.int32, (tk, 1), 0)
    sel_tile(k_diag, tk, jnp.where(kpos_sel <= t_row, 0.0, NEG))
    for h in heads:
        out_scr[h] = out_scr[h] + gate_row(h, 1) * finish(h)

    init_state()
    tkw = WINDOW + tq
    k0 = pl.multiple_of(jnp.maximum(t0 - WINDOW, 0), tq)
    dist = t_row - (k0 + lax.broadcasted_iota(jnp.int32, (tkw, 1), 0))
    kt = kw_ref[pl.ds(k0, tkw), :]
    online_tile([kt, kt], q4_scr, [vwa_ref[h, :, pl.ds(k0, tkw)] for h in heads],
                jnp.where((dist >= 0) & (dist < WINDOW), 0.0, NEG))
    both = jnp.concatenate([out_scr[h] + gate_row(h, 2) * finish(h) for h in heads], axis=0)
    for g in range(NSA_GROUP):
        o_ref[:, g * LANES:(g + 1) * LANES] = both[:, g * tq:(g + 1) * tq].T.astype(o_ref.dtype)


def _nsa(q3, kc, vc_t, ks_aug, vs_aug, kw, vw_aug, gt, ov_t):
    bsz, seq, qw = q3.shape
    tq = NSA_TQ
    n_rows = kc.shape[1]
    nh = NSA_KV_HEADS
    full = pl.BlockSpec((None, seq, LANES), lambda b, i: (b, 0, 0))
    head_tok = pl.BlockSpec((None, nh, seq, LANES), lambda b, i: (b, 0, 0, 0))
    head_feat = pl.BlockSpec((None, nh, LANES, seq), lambda b, i: (b, 0, 0, 0))
    tile = lambda w: pl.BlockSpec((None, tq, w), lambda b, i: (b, i, 0))
    cols = NSA_GROUP * tq
    return pl.pallas_call(
        functools.partial(_nsa_kernel, seq=seq),
        grid=(bsz, seq // tq),
        in_specs=[tile(qw),
                  pl.BlockSpec((None, n_rows, LANES), lambda b, i: (b, 0, 0)),
                  pl.BlockSpec((None, LANES, n_rows), lambda b, i: (b, 0, 0)),
                  head_tok, head_feat, full, head_feat, tile(LANES), _const_spec(ov_t.shape)],
        out_specs=tile(qw),
        out_shape=jax.ShapeDtypeStruct((bsz, seq, qw), BF16),
        scratch_shapes=[pltpu.VMEM((nh, 1, cols), F32),
                        pltpu.VMEM((nh, LANES, cols), F32),
                        pltpu.VMEM((nh, HEAD_DIM, cols), F32),
                        pltpu.VMEM((nh, cols, LANES), BF16),
                        pltpu.VMEM((nh, cols, LANES), BF16),
                        pltpu.VMEM((seq // SEL_LEN, tq), F32)],
        compiler_params=_cparams(("parallel", "arbitrary")),
        name="nsa_attention",
    )(q3, kc, vc_t, ks_aug, vs_aug, kw, vw_aug, gt, ov_t)


def _sb_kernel(q_ref, k_ref, vt_ref, o_ref, r_scr, acc_scr):
    tq = q_ref.shape[0]
    tk = SB_TILES * tq
    i_tile = pl.program_id(1)
    t0 = i_tile * tq
    q = q_ref[...]
    lane = lax.broadcasted_iota(jnp.int32, (1, LANES), 1)
    feat = lax.broadcasted_iota(jnp.int32, (LANES, 1), 0)
    t_row = t0 + lax.broadcasted_iota(jnp.int32, (1, tq), 1)
    kpos_col = lax.broadcasted_iota(jnp.int32, (tk, 1), 0)
    sr = lax.broadcasted_iota(jnp.int32, (tq, tq), 0)
    sc = lax.broadcasted_iota(jnp.int32, (tq, tq), 1)
    later = jnp.where(sc > sr, 1.0, 0.0).astype(BF16)
    qm = []
    for h in range(SB_HEADS):
        in_head = (lane >> _LOG_HEAD) == (h % 2)
        pair = q[:, (h // 2) * LANES:(h // 2 + 1) * LANES]
        qm.append(jnp.where(in_head, pair, jnp.zeros((), BF16)))
    r_scr[...] = jnp.zeros_like(r_scr)
    acc_scr[...] = jnp.zeros_like(acc_scr)

    def cond(c):
        j, r_max = c
        return (j >= 0) & (r_max > SB_EXIT)

    def body(c):
        j, _ = c
        k0 = pl.multiple_of(jnp.maximum(j + 1 - SB_TILES, 0) * tq, tq)
        kt = k_ref[pl.ds(k0, tk), :]
        z = jnp.concatenate([_dot_nt(kt[:, (h // 2) * LANES:(h // 2 + 1) * LANES], qm[h])
                             for h in range(SB_HEADS)], axis=1)
        kpos = k0 + kpos_col
        hidden = jnp.where((kpos < t_row) & (kpos < (j + 1) * tq), 0.0, NEG)
        z = z + jnp.concatenate([hidden] * SB_HEADS, axis=1)
        ls = jnp.minimum(z, 0.0) - jnp.log(1.0 + jnp.exp(-jnp.abs(z)))
        rest = ls - z
        rest_b = rest.astype(BF16)
        r_prev = r_scr[...]
        totals = [jnp.sum(rest[a * tq:(a + 1) * tq], axis=0, keepdims=True) for a in range(SB_TILES)]
        after_tiles = []
        right = r_prev
        for a in reversed(range(SB_TILES)):
            rows = slice(a * tq, (a + 1) * tq)
            after_tiles.append(_dot(later, rest_b[rows]) + right)
            right = right + totals[a]
        after = jnp.concatenate(after_tiles[::-1], axis=0)
        w = jnp.exp(ls + after).astype(BF16)
        for h in range(SB_HEADS):
            cols = slice(h * tq, (h + 1) * tq)
            vt = vt_ref[(h // 2) * LANES:(h // 2 + 1) * LANES, pl.ds(k0, tk)]
            acc_scr[:, cols] = acc_scr[:, cols] + _dot(vt, w[:, cols])
        r_new = right
        r_scr[...] = r_new
        return j - SB_TILES, jnp.max(r_new)

    lax.while_loop(cond, body, (i_tile, jnp.zeros((), F32)))
    for hp in range(SB_HEADS // 2):
        even = acc_scr[:, (2 * hp) * tq:(2 * hp + 1) * tq]
        odd = acc_scr[:, (2 * hp + 1) * tq:(2 * hp + 2) * tq]
        o_ref[:, hp * LANES:(hp + 1) * LANES] = jnp.where(feat < HEAD_DIM, even, odd).T.astype(o_ref.dtype)


def _sb(q3, k3, v_t):
    bsz, seq, w = q3.shape
    tq = SB_TQ
    full = pl.BlockSpec((None, seq, w), lambda b, i: (b, 0, 0))
    full_t = pl.BlockSpec((None, w, seq), lambda b, i: (b, 0, 0))
    tile = pl.BlockSpec((None, tq, w), lambda b, i: (b, i, 0))
    return pl.pallas_call(
        _sb_kernel,
        grid=(bsz, seq // tq),
        in_specs=[tile, full, full_t],
        out_specs=tile,
        out_shape=jax.ShapeDtypeStruct((bsz, seq, w), BF16),
        scratch_shapes=[pltpu.VMEM((1, SB_HEADS * tq), F32), pltpu.VMEM((LANES, SB_HEADS * tq), F32)],
        compiler_params=_cparams(("parallel", "arbitrary")),
        name="sb_attention",
    )(q3, k3, v_t)


def _merge_kernel(x_ref, g_ref, a_ref, b_ref, c_ref, wg_ref, wb_ref, wo_ref, o_ref):
    x = x_ref[...]
    d = x.shape[1]
    h = _rms(x, g_ref[...]).astype(BF16)
    mixed = None
    for n, br in enumerate((a_ref, b_ref, c_ref)):
        gate = jax.nn.sigmoid(_dot(h, wg_ref[:, n * d:(n + 1) * d]))
        y = gate * _dot(br[...], wb_ref[n])
        mixed = y if mixed is None else mixed + y
    o_ref[...] = x + _dot(mixed.astype(BF16), wo_ref[...])


def _merge(x2, layer, g, o_a, o_b, o_c, w_gate, w_branch, w_out):
    t, d = x2.shape
    tm = min(TM_PROJ, t)
    row = lambda i: (i, 0)
    bw = o_a.shape[1]
    return pl.pallas_call(
        _merge_kernel,
        grid=(t // tm,),
        in_specs=[pl.BlockSpec((tm, d), row), _layer_spec(g, layer),
                  pl.BlockSpec((tm, bw), row), pl.BlockSpec((tm, bw), row), pl.BlockSpec((tm, bw), row),
                  _layer_spec(w_gate, layer), _layer_spec(w_branch, layer), _layer_spec(w_out, layer)],
        out_specs=pl.BlockSpec((tm, d), row),
        out_shape=jax.ShapeDtypeStruct((t, d), F32),
        compiler_params=_cparams(("parallel",)),
        name="gated_merge",
    )(x2, g, o_a, o_b, o_c, w_gate, w_branch, w_out)


def _ffn_kernel(x_ref, halo_ref, g_ref, wup_ref, cw_ref, cb_ref, wdn_ref, gout_ref, o_ref, act_scr, *,
                tiles_per_seq, norm_out):
    x = x_ref[...]
    tm = x.shape[0]
    d_ff = wdn_ref.shape[0]
    at_start = (pl.program_id(0) % tiles_per_seq) == 0
    g = g_ref[...]
    h_halo = jnp.where(at_start, 0.0, _rms(halo_ref[...], g))
    h = jnp.concatenate([h_halo, _rms(x, g)], axis=0).astype(BF16)
    for f in range(d_ff // FF_TILE):
        halves = []
        for base in (0, d_ff):
            cols = slice(base + f * FF_TILE, base + (f + 1) * FF_TILE)
            u = _dot(h, wup_ref[:, cols])
            w = cw_ref[:, cols]
            conv = (w[2:3] * u[SUBLANES:]
                    + w[1:2] * pltpu.roll(u, 1, 0)[SUBLANES:]
                    + w[0:1] * pltpu.roll(u, 2, 0)[SUBLANES:]
                    + cb_ref[:, cols])
            halves.append(conv)
        act_scr[:, f * FF_TILE:(f + 1) * FF_TILE] = (jax.nn.silu(halves[0]) * halves[1]).astype(BF16)
    y = x + _dot(act_scr[...], wdn_ref[...])
    o_ref[...] = _rms(y, gout_ref[...]) if norm_out else y


def _ffn(x2, layer, g, w_up, conv_w, conv_b, w_down, g_out, seq, norm_out):
    t, d = x2.shape
    tm = min(TM_FFN, seq)
    tiles_per_seq = seq // tm
    row = lambda i: (i, 0)
    halo = lambda i: (jnp.maximum(i * (tm // SUBLANES) - 1, 0), 0)
    return pl.pallas_call(
        functools.partial(_ffn_kernel, tiles_per_seq=tiles_per_seq, norm_out=norm_out),
        grid=(t // tm,),
        in_specs=[pl.BlockSpec((tm, d), row), pl.BlockSpec((SUBLANES, d), halo), _layer_spec(g, layer),
                  _layer_spec(w_up, layer), _layer_spec(conv_w, layer), _layer_spec(conv_b, layer),
                  _layer_spec(w_down, layer), _const_spec(g_out.shape)],
        out_specs=pl.BlockSpec((tm, d), row),
        out_shape=jax.ShapeDtypeStruct((t, d), F32),
        scratch_shapes=[pltpu.VMEM((tm, w_down.shape[1]), BF16)],
        compiler_params=_cparams(("parallel",)),
        name="conv_ffn",
    )(x2, x2, g, w_up, conv_w, conv_b, w_down, g_out)


def _norm_kernel(x_ref, g_ref, o_ref):
    o_ref[...] = _rms(x_ref[...], g_ref[...])


def _final_norm(x2, g):
    t, d = x2.shape
    tm = min(TM_PROJ, t)
    return pl.pallas_call(
        _norm_kernel,
        grid=(t // tm,),
        in_specs=[pl.BlockSpec((tm, d), lambda i: (i, 0)), _const_spec((1, d))],
        out_specs=pl.BlockSpec((tm, d), lambda i: (i, 0)),
        out_shape=jax.ShapeDtypeStruct((t, d), F32),
        compiler_params=_cparams(("parallel",)),
        name="final_norm",
    )(x2, g)


def _nsa_slot_order():
    return [NSA_GROUP * (s % 2) + s // 2 for s in range(NSA_HEADS)]


def _pack_w_in(w):
    d = w.shape[0]
    o_q, o_kv, o_g, o_sb, o_gm = 512, 1024, 1792, 1816, 3352
    kv_w = NSA_KV_HEADS * HEAD_DIM
    sb_w = SB_HEADS * HEAD_DIM
    u = w[:, 0:o_q]
    q = [w[:, o_q + hq * HEAD_DIM:o_q + (hq + 1) * HEAD_DIM] for hq in _nsa_slot_order()]
    k = [w[:, o_kv + br * 2 * kv_w:o_kv + br * 2 * kv_w + kv_w] for br in range(3)]
    v = [w[:, o_kv + br * 2 * kv_w + kv_w:o_kv + (br + 1) * 2 * kv_w] for br in range(3)]
    gates = [w[:, o_g:o_sb], jnp.zeros((d, LANES - (o_sb - o_g)), w.dtype)]
    packed = jnp.concatenate([u] + q + k + [v[0]] + gates + [w[:, o_sb:o_sb + 2 * sb_w]], axis=1).astype(BF16)
    w_vt = jnp.concatenate([v[1], v[2], w[:, o_sb + 2 * sb_w:o_gm]], axis=1).astype(BF16).T
    return packed, w_vt, w[:, o_gm:].astype(BF16)


def _rope_tables(seq):
    inv_freq = 1.0 / (ROPE_THETA ** (jnp.arange(0, HEAD_DIM, 2, dtype=F32) / HEAD_DIM))
    ang = jnp.arange(seq, dtype=F32)[:, None] * inv_freq[None, :]
    cos, sin = jnp.cos(ang), jnp.sin(ang)
    cos_t = jnp.tile(cos, (1, LANES // (HEAD_DIM // 2)))
    sin_t = jnp.tile(jnp.concatenate([-sin, sin], axis=1), (1, LANES // HEAD_DIM))
    return cos_t, sin_t


def _overlap_t(seq):
    n_rows = seq // CMP_STRIDE
    n_sel = seq // SEL_LEN
    cmp_start = jnp.arange(n_rows) * CMP_STRIDE
    sel_start = jnp.arange(n_sel) * SEL_LEN
    ov = ((cmp_start[None, :] < sel_start[:, None] + SEL_LEN)
          & (cmp_start[None, :] + CMP_LEN > sel_start[:, None])
          & (jnp.arange(n_rows)[None, :] < (seq - CMP_LEN) // CMP_STRIDE + 1))
    return ov.astype(BF16)


def kernel(x, norm_mix, w_in, ssm_lam_re, ssm_lam_im, ssm_log_step, ssm_b_re, ssm_b_im, ssm_c_re, ssm_c_im, ssm_d, ssm_w_glu, cmp_w1_k, cmp_w2_k, cmp_pos_k, cmp_w1_v, cmp_w2_v, cmp_pos_v, w_branch, w_out, norm_ffn, ffn_w_up, ffn_conv_w, ffn_conv_b, ffn_w_down, norm_final):
    bsz, seq, d = x.shape
    depth = w_in.shape[0]
    t = bsz * seq
    assert seq % 512 == 0 and seq // SEL_LEN <= HEAD_DIM
    assert seq >= WINDOW + NSA_TQ
    if depth == 0:
        return _final_norm(x.reshape(t, d).astype(F32), norm_final.reshape(1, d)).reshape(x.shape).astype(x.dtype)
    cos_t, sin_t = _rope_tables(seq)
    ov_t = _overlap_t(seq)
    w_packed, w_vt, w_gate = jax.vmap(_pack_w_in)(w_in)
    s5_tabs = jax.vmap(_s5_tables)(ssm_lam_re, ssm_lam_im, ssm_log_step, ssm_b_re, ssm_b_im, ssm_c_re, ssm_c_im)
    s5_d = ssm_d.reshape(depth, 1, -1).astype(F32)
    s5_glu = ssm_w_glu.astype(BF16)
    cmp_k = jax.vmap(_compress_weights)(cmp_w1_k, cmp_w2_k, cmp_pos_k)
    cmp_v = jax.vmap(functools.partial(_compress_weights, feature_major_out=True))(cmp_w1_v, cmp_w2_v, cmp_pos_v)
    wb_nsa = jnp.concatenate([w_branch[:, 1, hq * HEAD_DIM:(hq + 1) * HEAD_DIM] for hq in _nsa_slot_order()], axis=1)
    wb_all = jnp.stack([w_branch[:, 0], wb_nsa, w_branch[:, 2]], axis=1).astype(BF16)
    wo_all = w_out.astype(BF16)
    g_mix = norm_mix.reshape(depth, 1, d).astype(F32)
    g_ffn = norm_ffn.reshape(depth, 1, d).astype(F32)
    up_all = ffn_w_up.astype(BF16)
    down_all = ffn_w_down.astype(BF16)
    conv_w = ffn_conv_w.astype(F32)
    conv_b = ffn_conv_b.reshape(depth, 1, -1).astype(F32)
    g_final = norm_final.reshape(1, d).astype(F32)

    x2 = x.reshape(t, d).astype(F32)
    b3 = lambda a: a.reshape(bsz, seq, a.shape[-1])
    rows16 = lambda a: a.reshape(bsz, seq // CMP_STRIDE, a.shape[-1])
    for l in range(depth):
        (u, q, kc_raw, ks_aug, kw, vc_raw, vs_aug, vw_aug, gt, sq, sk, sv_t) = _proj_in(
            x2, l, g_mix, w_packed, w_vt, cos_t, sin_t, bsz, seq)
        o_ssm = _s5(b3(u), l, s5_tabs, s5_d, s5_glu)
        kc, vc_t = _compress(rows16(kc_raw), rows16(vc_raw), l, cmp_k, cmp_v)
        o_nsa = _nsa(b3(q), kc, vc_t, ks_aug, vs_aug, b3(kw), vw_aug, b3(gt), ov_t)
        o_sb = _sb(b3(sq), b3(sk), sv_t)
        x2 = _merge(x2, l, g_mix, o_ssm.reshape(t, -1), o_nsa.reshape(t, -1), o_sb.reshape(t, -1),
                    w_gate, wb_all, wo_all)
        x2 = _ffn(x2, l, g_ffn, up_all, conv_w, conv_b, down_all, g_final, seq, norm_out=(l == depth - 1))
    return x2.reshape(bsz, seq, d).astype(x.dtype)
```
